```python
import math
import jax
import jax.numpy as jnp
from jax import lax
import numpy as np

D_MODEL = 1024
BATCH = 16
SEQ = 256
DEPTH = 4
DEC_BATCH = 2
DEC_SEQ = 4096
PAST_LEN = 512

GRID_W = 64
N_BRANCH = 3
HY_WIDTH = 256
HY_ORDER = 2
HY_SHORT_K = 3
HY_POS_EMB = 33
HY_POS_BANDS = (HY_POS_EMB - 1) // 2
HY_FILTER_HIDDEN = 64
HY_DECAY_FAST = -math.log(1e-2) / 0.3
HY_DECAY_SLOW = -math.log(1e-2) / 1.5
S5_WIDTH = 256
S5_GROUP = 16
S5_GROUPS = S5_WIDTH // S5_GROUP
S5_STATE = 64
N_HEADS = 8
N_KV_HEADS = 2
Q_PER_KV = N_HEADS // N_KV_HEADS
HEAD_DIM = 64
ATTN_WIDTH = N_HEADS * HEAD_DIM
KV_WIDTH = N_KV_HEADS * HEAD_DIM
WINDOW = 128
BLOCK = 128
ROPE_BASE = 10000.0
FFN_HIDDEN = ((8 * D_MODEL // 3 + 255) // 256) * 256
HY_IN = (HY_ORDER + 1) * HY_WIDTH
GATE_IN = N_BRANCH * D_MODEL
W_IN_COLS = HY_IN + S5_WIDTH + ATTN_WIDTH + 2 * KV_WIDTH + GATE_IN
SPLIT_S5 = HY_IN
SPLIT_Q = SPLIT_S5 + S5_WIDTH
SPLIT_K = SPLIT_Q + ATTN_WIDTH
SPLIT_V = SPLIT_K + KV_WIDTH
SPLIT_G = SPLIT_V + KV_WIDTH
IN_SPLITS = (SPLIT_S5, SPLIT_Q, SPLIT_K, SPLIT_V, SPLIT_G)

F32 = jnp.float32
EPS = 1e-6
NEG_INF = -1e30

kernel_name = 'hybrid_diffusion_trunk_step'


def _rmsnorm(x, g):
    xf = x.astype(F32)
    xf = xf * lax.rsqrt(jnp.mean(xf * xf, axis=-1, keepdims=True) + EPS)
    return (xf * g.astype(F32)).astype(x.dtype)


def _short_conv(u, w, b):
    L = u.shape[1]
    pad = HY_SHORT_K // 2
    up = jnp.pad(u, ((0, 0), (pad, pad), (0, 0)))
    w = w.astype(u.dtype)
    out = b.astype(u.dtype)
    for j in range(HY_SHORT_K):
        out = out + up[:, j:j + L] * w[j]
    return out


def _hyena_filters_fft(L, p):
    t = jnp.linspace(0.0, 1.0, L, dtype=F32)
    w = 2.0 * math.pi * jnp.arange(L, dtype=F32) / L
    bands = jnp.linspace(1e-4, HY_POS_BANDS - 1, HY_POS_BANDS, dtype=F32)
    ang = w[:, None] * bands[None, :]
    feat = jnp.concatenate([t[:, None], jnp.cos(ang), -jnp.sin(ang)], axis=-1)
    freq = p['hy_sin_freq'].astype(F32)
    h = jnp.sin(freq * (feat @ p['hy_pos_w1'].astype(F32) + p['hy_pos_b1'].astype(F32)))
    h = jnp.sin(freq * (h @ p['hy_pos_w2'].astype(F32) + p['hy_pos_b2'].astype(F32)))
    h = h @ p['hy_pos_w3'].astype(F32)
    h = h * jnp.exp(-t[:, None] * jnp.abs(p['hy_decay'].astype(F32)))
    h = h.reshape(L, HY_ORDER, 2, HY_WIDTH)
    h_fwd, h_bwd = h[:, :, 0], h[:, :, 1]
    k2 = jnp.concatenate([h_fwd, jnp.zeros((1, HY_ORDER, HY_WIDTH), F32), jnp.flip(h_bwd[1:], axis=0)], axis=0)
    k2 = k2 / jnp.sum(jnp.abs(k2), axis=0, keepdims=True)
    return jnp.fft.rfft(k2, axis=0)


def _fft_conv(u, kf):
    L = u.shape[1]
    uf = jnp.fft.rfft(u, n=2 * L, axis=1)
    return jnp.fft.irfft(uf * kf[None], n=2 * L, axis=1)[:, :L]


def _hyena(zh, p):
    L = zh.shape[1]
    u = _short_conv(zh, p['hy_conv_w'], p['hy_conv_b']).astype(F32)
    v, *gates = jnp.split(u, HY_ORDER + 1, axis=-1)
    kf = _hyena_filters_fft(L, p)
    skip = p['hy_skip'].astype(F32)
    z = v
    for o, gate in enumerate(gates):
        z = gate * (_fft_conv(z, kf[:, o]) + skip[o] * z)
    return z.astype(zh.dtype)


def _linear_combine(e1, e2):
    a1, b1 = e1
    a2, b2 = e2
    return a1 * a2, a2 * b1 + b2


def _s5(zb, p, h0):
    B, L, _ = zb.shape
    u = zb.astype(F32).reshape(B, L, S5_GROUPS, S5_GROUP)
    uc = u.astype(jnp.complex64)
    y = u * p['s5_skip'].astype(F32).reshape(S5_GROUPS, S5_GROUP)
    finals = []
    for d in range(2):
        lam = lax.complex(p['s5_lam_re'][d].astype(F32), p['s5_lam_im'][d].astype(F32))
        step = jnp.exp(p['s5_log_step'][d].astype(F32))[:, None]
        lam_bar = jnp.exp(lam * step)
        b_mat = lax.complex(p['s5_b_re'][d].astype(F32), p['s5_b_im'][d].astype(F32))
        b_bar = ((lam_bar - 1.0) / lam)[..., None] * b_mat
        bu = jnp.einsum('blgc,gpc->blgp', uc, b_bar)
        reverse = d == 1
        edge = L - 1 if reverse else 0
        if h0 is not None:
            bu = bu.at[:, edge].add(lam_bar * h0[:, d])
        a = jnp.broadcast_to(lam_bar, bu.shape)
        _, hs = lax.associative_scan(_linear_combine, (a, bu), reverse=reverse, axis=1)
        finals.append(hs[:, L - 1 - edge])
        c_mat = lax.complex(p['s5_c_re'][d].astype(F32), p['s5_c_im'][d].astype(F32))
        y = y + jnp.real(jnp.einsum('blgp,gcp->blgc', hs, c_mat))
    y = y.reshape(B, L, S5_WIDTH)
    a_lin, b_gate = jnp.split(y @ p['s5_glu_w'].astype(F32) + p['s5_glu_b'].astype(F32), 2, axis=-1)
    out = a_lin * jax.nn.sigmoid(b_gate)
    return out.astype(zb.dtype), jnp.stack(finals, axis=1)


def _axial_rope(x):
    L = x.shape[1]
    rows = L // GRID_W
    row = jnp.repeat(jnp.arange(rows, dtype=F32), GRID_W)
    col = jnp.tile(jnp.arange(GRID_W, dtype=F32), rows)
    n_freq = HEAD_DIM // 4
    inv = ROPE_BASE ** (-jnp.arange(n_freq, dtype=F32) / n_freq)

    def rot(xp, pos):
        ang = pos[:, None] * inv
        cos = jnp.cos(ang)[None, :, None, :]
        sin = jnp.sin(ang)[None, :, None, :]
        x1, x2 = jnp.split(xp.astype(F32), 2, axis=-1)
        return jnp.concatenate([x1 * cos - x2 * sin, x1 * sin + x2 * cos], axis=-1)

    half = HEAD_DIM // 2
    return jnp.concatenate([rot(x[..., :half], row), rot(x[..., half:], col)], axis=-1).astype(x.dtype)


def _attend(q, k, v, sink, mask):
    s = jnp.einsum('bqgrd,bkgd->bgrqk', q, k).astype(F32) * (HEAD_DIM ** -0.5)
    if mask is not None:
        s = jnp.where(mask, s, NEG_INF)
    sk = jnp.broadcast_to(sink.astype(F32)[None, :, :, None, None], s.shape[:-1] + (1,))
    pr = jax.nn.softmax(jnp.concatenate([s, sk], axis=-1), axis=-1)[..., :-1]
    return jnp.einsum('bgrqk,bkgd->bqgrd', pr.astype(v.dtype), v)


def _context_attention(q, k, v, sink):
    B, L = q.shape[:2]
    nb = L // BLOCK
    qb = q.reshape(B, nb, BLOCK, N_KV_HEADS, Q_PER_KV, HEAD_DIM).swapaxes(0, 1)
    out = lax.map(lambda qi: _attend(qi, k, v, sink, None), qb)
    return out.swapaxes(0, 1).reshape(B, L, ATTN_WIDTH)


def _latent_attention(q, k, v, kc, vc, sink):
    B, L = q.shape[:2]
    nb = L // BLOCK
    span = BLOCK + 2 * WINDOW
    lc = kc.shape[1]
    idx = jnp.arange(nb)[:, None] * BLOCK + jnp.arange(span)[None, :]
    kp = jnp.pad(k, ((0, 0), (WINDOW, WINDOW), (0, 0), (0, 0)))
    vp = jnp.pad(v, ((0, 0), (WINDOW, WINDOW), (0, 0), (0, 0)))
    kw = kp[:, idx].swapaxes(0, 1)
    vw = vp[:, idx].swapaxes(0, 1)
    qpos = jnp.arange(nb)[:, None] * BLOCK + jnp.arange(BLOCK)[None, :]
    kpos = idx - WINDOW
    win = (kpos[:, None, :] >= 0) & (kpos[:, None, :] < L) & (jnp.abs(qpos[:, :, None] - kpos[:, None, :]) <= WINDOW)
    mask = jnp.concatenate([jnp.ones((nb, BLOCK, lc), dtype=bool), win], axis=-1)
    qb = q.reshape(B, nb, BLOCK, N_KV_HEADS, Q_PER_KV, HEAD_DIM).swapaxes(0, 1)

    def one(args):
        qi, kwi, vwi, mi = args
        return _attend(qi, jnp.concatenate([kc, kwi], axis=1), jnp.concatenate([vc, vwi], axis=1), sink, mi)

    out = lax.map(one, (qb, kw, vw, mask))
    return out.swapaxes(0, 1).reshape(B, L, ATTN_WIDTH)


def _layer(x, mod, p, ctx_kv, s5_h0):
    B, L, _ = x.shape
    shift1, scale1, gate1, shift2, scale2, gate2 = jnp.split(mod, 6, axis=-1)
    h = _rmsnorm(x, p['norm1_g']) * (1 + scale1) + shift1
    z = h @ p['w_in']
    za, zb, zq, zk, zv, zg = jnp.split(z, IN_SPLITS, axis=-1)
    ya = _hyena(za, p)
    yb, s5_final = _s5(zb, p, s5_h0)
    q = zq.reshape(B, L, N_HEADS, HEAD_DIM)
    k = zk.reshape(B, L, N_KV_HEADS, HEAD_DIM)
    v = zv.reshape(B, L, N_KV_HEADS, HEAD_DIM)
    sink = p['attn_sink'].reshape(N_KV_HEADS, Q_PER_KV)
    if ctx_kv is None:
        yc = _context_attention(q.reshape(B, L, N_KV_HEADS, Q_PER_KV, HEAD_DIM), k, v, sink)
    else:
        q_rot = _axial_rope(q).reshape(B, L, N_KV_HEADS, Q_PER_KV, HEAD_DIM)
        yc = _latent_attention(q_rot, _axial_rope(k), v, ctx_kv[0], ctx_kv[1], sink)
    ga, gb, gc = jnp.split(jax.nn.sigmoid(zg), N_BRANCH, axis=-1)
    merged = ga * (ya @ p['proj_a']) + gb * (yb @ p['proj_b']) + gc * (yc @ p['proj_c'])
    x = x + gate1 * (merged @ p['w_out'])
    h = _rmsnorm(x, p['norm2_g']) * (1 + scale2) + shift2
    x = x + gate2 * ((jax.nn.silu(h @ p['ffn_w_gate']) * (h @ p['ffn_w_up'])) @ p['ffn_w_down'])
    return x, k, v, s5_final


def setup_inputs(seed: int = 0) -> dict:
    key = jax.random.key(seed)
    ks = iter(jax.random.split(key, 48))

    def nrm(shape, scale=1.0):
        return jax.random.normal(next(ks), shape, F32) * scale

    D = D_MODEL
    hy_decay = jnp.tile(jnp.linspace(HY_DECAY_FAST, HY_DECAY_SLOW, HY_WIDTH, dtype=F32), (DEPTH, 2 * HY_ORDER))
    lam_im = jnp.broadcast_to(math.pi * jnp.arange(S5_STATE, dtype=F32), (DEPTH, 2, S5_GROUPS, S5_STATE))
    return {
        'x_prompt': nrm((BATCH, SEQ, D)),
        'x_sample': nrm((DEC_BATCH, DEC_SEQ, D)),
        'cache_k': nrm((DEC_BATCH, DEPTH, PAST_LEN, N_KV_HEADS, HEAD_DIM)),
        'cache_v': nrm((DEC_BATCH, DEPTH, PAST_LEN, N_KV_HEADS, HEAD_DIM)),
        'state_s5_re': nrm((DEC_BATCH, DEPTH, 2, S5_GROUPS, S5_STATE), 0.1),
        'state_s5_im': nrm((DEC_BATCH, DEPTH, 2, S5_GROUPS, S5_STATE), 0.1),
        'c': nrm((DEC_BATCH, D)),
        'c_ctx': nrm((D,)),
        'ada_w': nrm((DEPTH, D, 6 * D), 0.5 * D ** -0.5),
        'ada_b': nrm((DEPTH, 6 * D), 0.01),
        'norm1_g': 1.0 + nrm((DEPTH, D), 0.05),
        'w_in': nrm((DEPTH, D, W_IN_COLS), D ** -0.5),
        'hy_conv_w': nrm((DEPTH, HY_SHORT_K, HY_IN), HY_SHORT_K ** -0.5),
        'hy_conv_b': nrm((DEPTH, HY_IN), 0.01),
        'hy_pos_w1': nrm((DEPTH, HY_POS_EMB, HY_FILTER_HIDDEN), HY_POS_EMB ** -0.5),
        'hy_pos_b1': nrm((DEPTH, HY_FILTER_HIDDEN), 0.1),
        'hy_pos_w2': nrm((DEPTH, HY_FILTER_HIDDEN, HY_FILTER_HIDDEN), HY_FILTER_HIDDEN ** -0.5),
        'hy_pos_b2': nrm((DEPTH, HY_FILTER_HIDDEN), 0.1),
        'hy_pos_w3': nrm((DEPTH, HY_FILTER_HIDDEN, 2 * HY_ORDER * HY_WIDTH), HY_FILTER_HIDDEN ** -0.5),
        'hy_sin_freq': 1.0 + nrm((DEPTH, HY_FILTER_HIDDEN), 0.1),
        'hy_decay': hy_decay + nrm((DEPTH, 2 * HY_ORDER * HY_WIDTH), 0.1),
        'hy_skip': nrm((DEPTH, HY_ORDER, HY_WIDTH)),
        's5_lam_re': -0.5 + nrm((DEPTH, 2, S5_GROUPS, S5_STATE), 0.01),
        's5_lam_im': lam_im + nrm((DEPTH, 2, S5_GROUPS, S5_STATE), 0.01),
        's5_log_step': jax.random.uniform(next(ks), (DEPTH, 2, S5_GROUPS), F32, math.log(1e-3), math.log(1e-1)),
        's5_b_re': nrm((DEPTH, 2, S5_GROUPS, S5_STATE, S5_GROUP), (2 * S5_GROUP) ** -0.5),
        's5_b_im': nrm((DEPTH, 2, S5_GROUPS, S5_STATE, S5_GROUP), (2 * S5_GROUP) ** -0.5),
        's5_c_re': nrm((DEPTH, 2, S5_GROUPS, S5_GROUP, S5_STATE), S5_STATE ** -0.5),
        's5_c_im': nrm((DEPTH, 2, S5_GROUPS, S5_GROUP, S5_STATE), S5_STATE ** -0.5),
        's5_skip': nrm((DEPTH, S5_WIDTH)),
        's5_glu_w': nrm((DEPTH, S5_WIDTH, 2 * S5_WIDTH), S5_WIDTH ** -0.5),
        's5_glu_b': nrm((DEPTH, 2 * S5_WIDTH), 0.01),
        'attn_sink': nrm((DEPTH, N_HEADS), 0.5),
        'proj_a': nrm((DEPTH, HY_WIDTH, D), HY_WIDTH ** -0.5),
        'proj_b': nrm((DEPTH, S5_WIDTH, D), S5_WIDTH ** -0.5),
        'proj_c': nrm((DEPTH, ATTN_WIDTH, D), ATTN_WIDTH ** -0.5),
        'w_out': nrm((DEPTH, D, D), D ** -0.5),
        'norm2_g': 1.0 + nrm((DEPTH, D), 0.05),
        'ffn_w_gate': nrm((DEPTH, D, FFN_HIDDEN), D ** -0.5),
        'ffn_w_up': nrm((DEPTH, D, FFN_HIDDEN), D ** -0.5),
        'ffn_w_down': nrm((DEPTH, FFN_HIDDEN, D), FFN_HIDDEN ** -0.5),
        'final_norm_g': 1.0 + nrm((D,), 0.05),
    }


def reference(x_prompt, x_sample, cache_k, cache_v, state_s5_re, state_s5_im, c, c_ctx,
              ada_w, ada_b, norm1_g, w_in, hy_conv_w, hy_conv_b, hy_pos_w1, hy_pos_b1,
              hy_pos_w2, hy_pos_b2, hy_pos_w3, hy_sin_freq, hy_decay, hy_skip,
              s5_lam_re, s5_lam_im, s5_log_step, s5_b_re, s5_b_im, s5_c_re, s5_c_im,
              s5_skip, s5_glu_w, s5_glu_b, attn_sink, proj_a, proj_b, proj_c, w_out,
              norm2_g, ffn_w_gate, ffn_w_up, ffn_w_down, final_norm_g):
    y_p = x_prompt
    y_s = x_sample
    ks_out, vs_out, sre_out, sim_out = [], [], [], []
    for l in range(DEPTH):
        p = {
            'norm1_g': norm1_g[l], 'w_in': w_in[l],
            'hy_conv_w': hy_conv_w[l], 'hy_conv_b': hy_conv_b[l],
            'hy_pos_w1': hy_pos_w1[l], 'hy_pos_b1': hy_pos_b1[l],
            'hy_pos_w2': hy_pos_w2[l], 'hy_pos_b2': hy_pos_b2[l], 'hy_pos_w3': hy_pos_w3[l],
            'hy_sin_freq': hy_sin_freq[l], 'hy_decay': hy_decay[l], 'hy_skip': hy_skip[l],
            's5_lam_re': s5_lam_re[l], 's5_lam_im': s5_lam_im[l], 's5_log_step': s5_log_step[l],
            's5_b_re': s5_b_re[l], 's5_b_im': s5_b_im[l], 's5_c_re': s5_c_re[l], 's5_c_im': s5_c_im[l],
            's5_skip': s5_skip[l], 's5_glu_w': s5_glu_w[l], 's5_glu_b': s5_glu_b[l],
            'attn_sink': attn_sink[l], 'proj_a': proj_a[l], 'proj_b': proj_b[l], 'proj_c': proj_c[l],
            'w_out': w_out[l], 'norm2_g': norm2_g[l],
            'ffn_w_gate': ffn_w_gate[l], 'ffn_w_up': ffn_w_up[l], 'ffn_w_down': ffn_w_down[l],
        }
        mod_ctx = (jax.nn.silu(c_ctx) @ ada_w[l] + ada_b[l]).reshape(1, 1, 6 * D_MODEL)
        y_p, k_l, v_l, s5_l = _layer(y_p, mod_ctx, p, None, None)
        ks_out.append(k_l)
        vs_out.append(v_l)
        sre_out.append(jnp.real(s5_l).astype(x_prompt.dtype))
        sim_out.append(jnp.imag(s5_l).astype(x_prompt.dtype))
        mod_lat = (jax.nn.silu(c) @ ada_w[l] + ada_b[l])[:, None, :]
        h0 = lax.complex(state_s5_re[:, l].astype(F32), state_s5_im[:, l].astype(F32))
        y_s, _, _, _ = _layer(y_s, mod_lat, p, (cache_k[:, l], cache_v[:, l]), h0)
    y_prompt = _rmsnorm(y_p, final_norm_g)
    y_sample = _rmsnorm(y_s, final_norm_g)
    new_cache_k = jnp.stack(ks_out, axis=1)
    new_cache_v = jnp.stack(vs_out, axis=1)
    new_state_s5_re = jnp.stack(sre_out, axis=1)
    new_state_s5_im = jnp.stack(sim_out, axis=1)
    return (y_prompt, y_sample, new_cache_k, new_cache_v, new_state_s5_re, new_state_s5_im)
```

```python
import functools
import math

import jax
import jax.numpy as jnp
import numpy as np
from jax import lax
from jax.experimental import pallas as pl
from jax.experimental.pallas import tpu as pltpu

D_MODEL = 1024
BATCH = 16
SEQ = 256
DEPTH = 4
DEC_BATCH = 2
DEC_SEQ = 4096
PAST_LEN = 512
GRID_W = 64
N_BRANCH = 3
HY_WIDTH = 256
HY_ORDER = 2
HY_SHORT_K = 3
HY_POS_EMB = 33
HY_POS_BANDS = (HY_POS_EMB - 1) // 2
HY_FILTER_HIDDEN = 64
S5_WIDTH = 256
S5_GROUP = 16
S5_GROUPS = S5_WIDTH // S5_GROUP
S5_STATE = 64
N_HEADS = 8
N_KV_HEADS = 2
Q_PER_KV = N_HEADS // N_KV_HEADS
HEAD_DIM = 64
ATTN_WIDTH = N_HEADS * HEAD_DIM
KV_WIDTH = N_KV_HEADS * HEAD_DIM
WINDOW = 128
BLOCK = 128
ROPE_BASE = 10000.0
FFN_HIDDEN = ((8 * D_MODEL // 3 + 255) // 256) * 256
HY_IN = (HY_ORDER + 1) * HY_WIDTH
GATE_IN = N_BRANCH * D_MODEL
MAIN_IN = HY_IN + S5_WIDTH + ATTN_WIDTH + 2 * KV_WIDTH
SPLIT_S5 = HY_IN
SPLIT_Q = SPLIT_S5 + S5_WIDTH
SPLIT_K = SPLIT_Q + ATTN_WIDTH
SPLIT_V = SPLIT_K + KV_WIDTH

F32 = jnp.float32
BF16 = jnp.bfloat16
EPS = 1e-6
NEG_INF = -1e30

N_PROMPT_TOK = BATCH * SEQ
N_SAMPLE_TOK = DEC_BATCH * DEC_SEQ
N_TOK = N_PROMPT_TOK + N_SAMPLE_TOK
MOD_ROWS = 8
TM = 256
VMEM_LIMIT = 56 * 1024 * 1024


def _cparams(*sem):
    return pltpu.CompilerParams(dimension_semantics=sem, vmem_limit_bytes=VMEM_LIMIT)


def _mod_row(i):
    n_p = N_PROMPT_TOK // TM
    per_b = DEC_SEQ // TM
    return jnp.where(i < n_p, 0, 1 + (i - n_p) // per_b)


def _mod_spec(layer, k):
    return pl.BlockSpec((None, 1, D_MODEL), lambda i: ((layer * MOD_ROWS + _mod_row(i)) * 6 + k, 0, 0))


def _resident(shape):
    return pl.BlockSpec(shape, lambda i: (0,) * len(shape), pipeline_mode=pl.Buffered(1))


def _rms(x, g):
    return x * lax.rsqrt(jnp.mean(x * x, axis=-1, keepdims=True) + EPS) * g


def _mod_kernel(c_ref, w_ref, b_ref, o_ref):
    c = c_ref[...]
    s = (c * jax.nn.sigmoid(c)).astype(BF16)
    o_ref[...] = jnp.dot(s, w_ref[...].astype(BF16), preferred_element_type=F32) + b_ref[...]


def _modulation(cvec, ada_w, ada_b):
    tn = 1536
    return pl.pallas_call(
        _mod_kernel,
        grid=(DEPTH, 6 * D_MODEL // tn),
        in_specs=[
            pl.BlockSpec((MOD_ROWS, D_MODEL), lambda l, j: (0, 0)),
            pl.BlockSpec((None, D_MODEL, tn), lambda l, j: (l, 0, j)),
            pl.BlockSpec((None, 1, tn), lambda l, j: (l, 0, j)),
        ],
        out_specs=pl.BlockSpec((None, MOD_ROWS, tn), lambda l, j: (l, 0, j)),
        out_shape=jax.ShapeDtypeStruct((DEPTH, MOD_ROWS, 6 * D_MODEL), F32),
        compiler_params=_cparams("arbitrary", "arbitrary"),
        name="adaln_mod",
    )(cvec, ada_w, ada_b.reshape(DEPTH, 1, 6 * D_MODEL))


def _inproj_kernel(x_ref, sh_ref, sc_ref, g_ref, wm_ref, wg_ref, zm_ref, sg_ref):
    h = (_rms(x_ref[...], g_ref[...]) * (1.0 + sc_ref[...]) + sh_ref[...]).astype(BF16)
    zm_ref[...] = jnp.dot(h, wm_ref[...], preferred_element_type=F32)
    for c in range(0, GATE_IN, D_MODEL):
        sg_ref[:, c:c + D_MODEL] = jax.nn.sigmoid(
            jnp.dot(h, wg_ref[:, c:c + D_MODEL], preferred_element_type=F32))


def _in_proj(x, mod, layer, g, w_main, w_gate):
    return pl.pallas_call(
        _inproj_kernel,
        grid=(N_TOK // TM,),
        in_specs=[
            pl.BlockSpec((TM, D_MODEL), lambda i: (i, 0)),
            _mod_spec(layer, 0),
            _mod_spec(layer, 1),
            _resident((1, D_MODEL)),
            _resident((D_MODEL, MAIN_IN)),
            _resident((D_MODEL, GATE_IN)),
        ],
        out_specs=[
            pl.BlockSpec((TM, MAIN_IN), lambda i: (i, 0)),
            pl.BlockSpec((TM, GATE_IN), lambda i: (i, 0)),
        ],
        out_shape=[
            jax.ShapeDtypeStruct((N_TOK, MAIN_IN), F32),
            jax.ShapeDtypeStruct((N_TOK, GATE_IN), F32),
        ],
        compiler_params=_cparams("arbitrary"),
        name="in_proj",
    )(x, mod, mod, g, w_main, w_gate)


def _merge_kernel(ya_ref, yb_ref, yc_ref, sg_ref, x_ref, g1_ref, pa_ref, pb_ref, pc_ref, wo_ref, o_ref):
    m = sg_ref[:, 0:D_MODEL] * jnp.dot(ya_ref[...].astype(BF16), pa_ref[...], preferred_element_type=F32)
    m = m + sg_ref[:, D_MODEL:2 * D_MODEL] * jnp.dot(yb_ref[...].astype(BF16), pb_ref[...], preferred_element_type=F32)
    m = m + sg_ref[:, 2 * D_MODEL:] * jnp.dot(yc_ref[...].astype(BF16), pc_ref[...], preferred_element_type=F32)
    o_ref[...] = x_ref[...] + g1_ref[...] * jnp.dot(m.astype(BF16), wo_ref[...], preferred_element_type=F32)


def _merge(ya, yb, yc, sg, x, mod, layer, pa, pb, pc, wo):
    return pl.pallas_call(
        _merge_kernel,
        grid=(N_TOK // TM,),
        in_specs=[
            pl.BlockSpec((TM, HY_WIDTH), lambda i: (i, 0)),
            pl.BlockSpec((TM, S5_WIDTH), lambda i: (i, 0)),
            pl.BlockSpec((TM, ATTN_WIDTH), lambda i: (i, 0)),
            pl.BlockSpec((TM, GATE_IN), lambda i: (i, 0)),
            pl.BlockSpec((TM, D_MODEL), lambda i: (i, 0)),
            _mod_spec(layer, 2),
            _resident((HY_WIDTH, D_MODEL)),
            _resident((S5_WIDTH, D_MODEL)),
            _resident((ATTN_WIDTH, D_MODEL)),
            _resident((D_MODEL, D_MODEL)),
        ],
        out_specs=pl.BlockSpec((TM, D_MODEL), lambda i: (i, 0)),
        out_shape=jax.ShapeDtypeStruct((N_TOK, D_MODEL), F32),
        compiler_params=_cparams("arbitrary"),
        name="merge_out",
    )(ya, yb, yc, sg, x, mod, pa, pb, pc, wo)


def _ffn_kernel(x_ref, sh_ref, sc_ref, gt_ref, g_ref, wg_ref, wu_ref, wd_ref, o_ref):
    x = x_ref[...]
    h = (_rms(x, g_ref[...]) * (1.0 + sc_ref[...]) + sh_ref[...]).astype(BF16)
    a = jnp.dot(h, wg_ref[...], preferred_element_type=F32)
    b = jnp.dot(h, wu_ref[...], preferred_element_type=F32)
    f = (a * jax.nn.sigmoid(a) * b).astype(BF16)
    o_ref[...] = x + gt_ref[...] * jnp.dot(f, wd_ref[...], preferred_element_type=F32)


def _ffn(x, mod, layer, g, wg, wu, wd):
    return pl.pallas_call(
        _ffn_kernel,
        grid=(N_TOK // TM,),
        in_specs=[
            pl.BlockSpec((TM, D_MODEL), lambda i: (i, 0)),
            _mod_spec(layer, 3),
            _mod_spec(layer, 4),
            _mod_spec(layer, 5),
            _resident((1, D_MODEL)),
            _resident((D_MODEL, FFN_HIDDEN)),
            _resident((D_MODEL, FFN_HIDDEN)),
            _resident((FFN_HIDDEN, D_MODEL)),
        ],
        out_specs=pl.BlockSpec((TM, D_MODEL), lambda i: (i, 0)),
        out_shape=jax.ShapeDtypeStruct((N_TOK, D_MODEL), F32),
        compiler_params=_cparams("arbitrary"),
        name="ffn",
    )(x, mod, mod, mod, g, wg, wu, wd)


def _final_norm_kernel(x_ref, g_ref, o_ref):
    o_ref[...] = _rms(x_ref[...], g_ref[...])


def _final_norm(x, g):
    tm = 512
    return pl.pallas_call(
        _final_norm_kernel,
        grid=(N_TOK // tm,),
        in_specs=[pl.BlockSpec((tm, D_MODEL), lambda i: (i, 0)), _resident((1, D_MODEL))],
        out_specs=pl.BlockSpec((tm, D_MODEL), lambda i: (i, 0)),
        out_shape=jax.ShapeDtypeStruct((N_TOK, D_MODEL), F32),
        compiler_params=_cparams("arbitrary"),
        name="final_norm",
    )(x, g)


def _short_conv(u, w, b):
    L = u.shape[1]
    pad = HY_SHORT_K // 2
    up = jnp.pad(u, ((0, 0), (pad, pad), (0, 0)))
    out = b
    for j in range(HY_SHORT_K):
        out = out + up[:, j:j + L] * w[j]
    return out


def _hyena_filters_fft(L, p):
    t = jnp.linspace(0.0, 1.0, L, dtype=F32)
    w = 2.0 * math.pi * jnp.arange(L, dtype=F32) / L
    bands = jnp.linspace(1e-4, HY_POS_BANDS - 1, HY_POS_BANDS, dtype=F32)
    ang = w[:, None] * bands[None, :]
    feat = jnp.concatenate([t[:, None], jnp.cos(ang), -jnp.sin(ang)], axis=-1)
    freq = p['hy_sin_freq']
    h = jnp.sin(freq * (feat @ p['hy_pos_w1'] + p['hy_pos_b1']))
    h = jnp.sin(freq * (h @ p['hy_pos_w2'] + p['hy_pos_b2']))
    h = h @ p['hy_pos_w3']
    h = h * jnp.exp(-t[:, None] * jnp.abs(p['hy_decay']))
    h = h.reshape(L, HY_ORDER, 2, HY_WIDTH)
    h_fwd, h_bwd = h[:, :, 0], h[:, :, 1]
    k2 = jnp.concatenate([h_fwd, jnp.zeros((1, HY_ORDER, HY_WIDTH), F32), jnp.flip(h_bwd[1:], axis=0)], axis=0)
    k2 = k2 / jnp.sum(jnp.abs(k2), axis=0, keepdims=True)
    return jnp.fft.rfft(k2, axis=0)


def _fft_conv(u, kf):
    L = u.shape[1]
    uf = jnp.fft.rfft(u, n=2 * L, axis=1)
    return jnp.fft.irfft(uf * kf[None], n=2 * L, axis=1)[:, :L]


def _hyena(zh, p):
    L = zh.shape[1]
    u = _short_conv(zh, p['hy_conv_w'], p['hy_conv_b'])
    v, *gates = jnp.split(u, HY_ORDER + 1, axis=-1)
    kf = _hyena_filters_fft(L, p)
    skip = p['hy_skip']
    z = v
    for o, gate in enumerate(gates):
        z = gate * (_fft_conv(z, kf[:, o]) + skip[o] * z)
    return z


def _linear_combine(e1, e2):
    a1, b1 = e1
    a2, b2 = e2
    return a1 * a2, a2 * b1 + b2


def _s5(zb, p, h0):
    B, L, _ = zb.shape
    u = zb.reshape(B, L, S5_GROUPS, S5_GROUP)
    uc = u.astype(jnp.complex64)
    y = u * p['s5_skip'].reshape(S5_GROUPS, S5_GROUP)
    finals = []
    for d in range(2):
        lam = lax.complex(p['s5_lam_re'][d], p['s5_lam_im'][d])
        step = jnp.exp(p['s5_log_step'][d])[:, None]
        lam_bar = jnp.exp(lam * step)
        b_mat = lax.complex(p['s5_b_re'][d], p['s5_b_im'][d])
        b_bar = ((lam_bar - 1.0) / lam)[..., None] * b_mat
        bu = jnp.einsum('blgc,gpc->blgp', uc, b_bar)
        reverse = d == 1
        edge = L - 1 if reverse else 0
        if h0 is not None:
            bu = bu.at[:, edge].add(lam_bar * h0[:, d])
        a = jnp.broadcast_to(lam_bar, bu.shape)
        _, hs = lax.associative_scan(_linear_combine, (a, bu), reverse=reverse, axis=1)
        finals.append(hs[:, L - 1 - edge])
        c_mat = lax.complex(p['s5_c_re'][d], p['s5_c_im'][d])
        y = y + jnp.real(jnp.einsum('blgp,gcp->blgc', hs, c_mat))
    y = y.reshape(B, L, S5_WIDTH)
    a_lin, b_gate = jnp.split(y @ p['s5_glu_w'] + p['s5_glu_b'], 2, axis=-1)
    return a_lin * jax.nn.sigmoid(b_gate), jnp.stack(finals, axis=1)


def _axial_rope(x):
    L = x.shape[1]
    rows = L // GRID_W
    row = jnp.repeat(jnp.arange(rows, dtype=F32), GRID_W)
    col = jnp.tile(jnp.arange(GRID_W, dtype=F32), rows)
    n_freq = HEAD_DIM // 4
    inv = ROPE_BASE ** (-jnp.arange(n_freq, dtype=F32) / n_freq)

    def rot(xp, pos):
        ang = pos[:, None] * inv
        cos = jnp.cos(ang)[None, :, None, :]
        sin = jnp.sin(ang)[None, :, None, :]
        x1, x2 = jnp.split(xp, 2, axis=-1)
        return jnp.concatenate([x1 * cos - x2 * sin, x1 * sin + x2 * cos], axis=-1)

    half = HEAD_DIM // 2
    return jnp.concatenate([rot(x[..., :half], row), rot(x[..., half:], col)], axis=-1)


def _attend(q, k, v, sink, mask):
    s = jnp.einsum('bqgrd,bkgd->bgrqk', q, k) * (HEAD_DIM ** -0.5)
    if mask is not None:
        s = jnp.where(mask, s, NEG_INF)
    sk = jnp.broadcast_to(sink[None, :, :, None, None], s.shape[:-1] + (1,))
    pr = jax.nn.softmax(jnp.concatenate([s, sk], axis=-1), axis=-1)[..., :-1]
    return jnp.einsum('bgrqk,bkgd->bqgrd', pr, v)


def _context_attention(q, k, v, sink):
    B, L = q.shape[:2]
    nb = L // BLOCK
    qb = q.reshape(B, nb, BLOCK, N_KV_HEADS, Q_PER_KV, HEAD_DIM).swapaxes(0, 1)
    out = lax.map(lambda qi: _attend(qi, k, v, sink, None), qb)
    return out.swapaxes(0, 1).reshape(B, L, ATTN_WIDTH)


def _latent_attention(q, k, v, kc, vc, sink):
    B, L = q.shape[:2]
    nb = L // BLOCK
    span = BLOCK + 2 * WINDOW
    lc = kc.shape[1]
    idx = jnp.arange(nb)[:, None] * BLOCK + jnp.arange(span)[None, :]
    kp = jnp.pad(k, ((0, 0), (WINDOW, WINDOW), (0, 0), (0, 0)))
    vp = jnp.pad(v, ((0, 0), (WINDOW, WINDOW), (0, 0), (0, 0)))
    kw = kp[:, idx].swapaxes(0, 1)
    vw = vp[:, idx].swapaxes(0, 1)
    qpos = jnp.arange(nb)[:, None] * BLOCK + jnp.arange(BLOCK)[None, :]
    kpos = idx - WINDOW
    win = (kpos[:, None, :] >= 0) & (kpos[:, None, :] < L) & (jnp.abs(qpos[:, :, None] - kpos[:, None, :]) <= WINDOW)
    mask = jnp.concatenate([jnp.ones((nb, BLOCK, lc), dtype=bool), win], axis=-1)
    qb = q.reshape(B, nb, BLOCK, N_KV_HEADS, Q_PER_KV, HEAD_DIM).swapaxes(0, 1)

    def one(args):
        qi, kwi, vwi, mi = args
        return _attend(qi, jnp.concatenate([kc, kwi], axis=1), jnp.concatenate([vc, vwi], axis=1), sink, mi)

    out = lax.map(one, (qb, kw, vw, mask))
    return out.swapaxes(0, 1).reshape(B, L, ATTN_WIDTH)


def _mixers(zm, p, ctx_kv, s5_h0):
    B, L, _ = zm.shape
    za, zb = zm[..., :SPLIT_S5], zm[..., SPLIT_S5:SPLIT_Q]
    zq, zk, zv = zm[..., SPLIT_Q:SPLIT_K], zm[..., SPLIT_K:SPLIT_V], zm[..., SPLIT_V:]
    ya = _hyena(za, p)
    yb, s5_final = _s5(zb, p, s5_h0)
    q = zq.reshape(B, L, N_HEADS, HEAD_DIM)
    k = zk.reshape(B, L, N_KV_HEADS, HEAD_DIM)
    v = zv.reshape(B, L, N_KV_HEADS, HEAD_DIM)
    sink = p['attn_sink'].reshape(N_KV_HEADS, Q_PER_KV)
    if ctx_kv is None:
        yc = _context_attention(q.reshape(B, L, N_KV_HEADS, Q_PER_KV, HEAD_DIM), k, v, sink)
    else:
        q_rot = _axial_rope(q).reshape(B, L, N_KV_HEADS, Q_PER_KV, HEAD_DIM)
        yc = _latent_attention(q_rot, _axial_rope(k), v, ctx_kv[0], ctx_kv[1], sink)
    return ya, yb, yc, k, v, s5_final


def kernel(x_prompt, x_sample, cache_k, cache_v, state_s5_re, state_s5_im, c, c_ctx, ada_w, ada_b, norm1_g, w_in, hy_conv_w, hy_conv_b, hy_pos_w1, hy_pos_b1, hy_pos_w2, hy_pos_b2, hy_pos_w3, hy_sin_freq, hy_decay, hy_skip, s5_lam_re, s5_lam_im, s5_log_step, s5_b_re, s5_b_im, s5_c_re, s5_c_im, s5_skip, s5_glu_w, s5_glu_b, attn_sink, proj_a, proj_b, proj_c, w_out, norm2_g, ffn_w_gate, ffn_w_up, ffn_w_down, final_norm_g):
    cvec = jnp.concatenate([c_ctx[None], c, jnp.zeros((MOD_ROWS - 1 - DEC_BATCH, D_MODEL), F32)], axis=0)
    mod = _modulation(cvec, ada_w, ada_b).reshape(DEPTH * MOD_ROWS * 6, 1, D_MODEL)

    x = jnp.concatenate([x_prompt.reshape(N_PROMPT_TOK, D_MODEL), x_sample.reshape(N_SAMPLE_TOK, D_MODEL)], axis=0)
    w_main = w_in[:, :, :MAIN_IN].astype(BF16)
    w_gate = w_in[:, :, MAIN_IN:].astype(BF16)
    pa, pb, pc, wo = proj_a.astype(BF16), proj_b.astype(BF16), proj_c.astype(BF16), w_out.astype(BF16)
    wg, wu, wd = ffn_w_gate.astype(BF16), ffn_w_up.astype(BF16), ffn_w_down.astype(BF16)

    ks_out, vs_out, sre_out, sim_out = [], [], [], []
    for l in range(DEPTH):
        p = {
            'hy_conv_w': hy_conv_w[l], 'hy_conv_b': hy_conv_b[l],
            'hy_pos_w1': hy_pos_w1[l], 'hy_pos_b1': hy_pos_b1[l],
            'hy_pos_w2': hy_pos_w2[l], 'hy_pos_b2': hy_pos_b2[l], 'hy_pos_w3': hy_pos_w3[l],
            'hy_sin_freq': hy_sin_freq[l], 'hy_decay': hy_decay[l], 'hy_skip': hy_skip[l],
            's5_lam_re': s5_lam_re[l], 's5_lam_im': s5_lam_im[l], 's5_log_step': s5_log_step[l],
            's5_b_re': s5_b_re[l], 's5_b_im': s5_b_im[l], 's5_c_re': s5_c_re[l], 's5_c_im': s5_c_im[l],
            's5_skip': s5_skip[l], 's5_glu_w': s5_glu_w[l], 's5_glu_b': s5_glu_b[l],
            'attn_sink': attn_sink[l],
        }
        zm, sg = _in_proj(x, mod, l, norm1_g[l].reshape(1, D_MODEL), w_main[l], w_gate[l])
        zm_p = zm[:N_PROMPT_TOK].reshape(BATCH, SEQ, MAIN_IN)
        zm_s = zm[N_PROMPT_TOK:].reshape(DEC_BATCH, DEC_SEQ, MAIN_IN)
        ya_p, yb_p, yc_p, k_l, v_l, s5_l = _mixers(zm_p, p, None, None)
        h0 = lax.complex(state_s5_re[:, l], state_s5_im[:, l])
        ya_s, yb_s, yc_s, _, _, _ = _mixers(zm_s, p, (cache_k[:, l], cache_v[:, l]), h0)
        ks_out.append(k_l)
        vs_out.append(v_l)
        sre_out.append(jnp.real(s5_l))
        sim_out.append(jnp.imag(s5_l))
        ya = jnp.concatenate([ya_p.reshape(N_PROMPT_TOK, -1), ya_s.reshape(N_SAMPLE_TOK, -1)], axis=0)
        yb = jnp.concatenate([yb_p.reshape(N_PROMPT_TOK, -1), yb_s.reshape(N_SAMPLE_TOK, -1)], axis=0)
        yc = jnp.concatenate([yc_p.reshape(N_PROMPT_TOK, -1), yc_s.reshape(N_SAMPLE_TOK, -1)], axis=0)
        x = _merge(ya, yb, yc, sg, x, mod, l, pa[l], pb[l], pc[l], wo[l])
        x = _ffn(x, mod, l, norm2_g[l].reshape(1, D_MODEL), wg[l], wu[l], wd[l])

    y = _final_norm(x, final_norm_g.reshape(1, D_MODEL))
    y_prompt = y[:N_PROMPT_TOK].reshape(BATCH, SEQ, D_MODEL)
    y_sample = y[N_PROMPT_TOK:].reshape(DEC_BATCH, DEC_SEQ, D_MODEL)
    return (y_prompt, y_sample, jnp.stack(ks_out, axis=1), jnp.stack(vs_out, axis=1),
            jnp.stack(sre_out, axis=1), jnp.stack(sim_out, axis=1))
```

```python
import functools
import math

import jax
import jax.numpy as jnp
import numpy as np
from jax import lax
from jax.experimental import pallas as pl
from jax.experimental.pallas import tpu as pltpu

D_MODEL = 1024
BATCH = 16
SEQ = 256
DEPTH = 4
DEC_BATCH = 2
DEC_SEQ = 4096
PAST_LEN = 512
GRID_W = 64
N_BRANCH = 3
HY_WIDTH = 256
HY_ORDER = 2
HY_SHORT_K = 3
HY_POS_EMB = 33
HY_POS_BANDS = (HY_POS_EMB - 1) // 2
HY_FILTER_HIDDEN = 64
S5_WIDTH = 256
S5_GROUP = 16
S5_GROUPS = S5_WIDTH // S5_GROUP
S5_STATE = 64
N_HEADS = 8
N_KV_HEADS = 2
Q_PER_KV = N_HEADS // N_KV_HEADS
HEAD_DIM = 64
ATTN_WIDTH = N_HEADS * HEAD_DIM
KV_WIDTH = N_KV_HEADS * HEAD_DIM
WINDOW = 128
BLOCK = 128
ROPE_BASE = 10000.0
FFN_HIDDEN = ((8 * D_MODEL // 3 + 255) // 256) * 256
HY_IN = (HY_ORDER + 1) * HY_WIDTH
GATE_IN = N_BRANCH * D_MODEL
MAIN_IN = HY_IN + S5_WIDTH + ATTN_WIDTH + 2 * KV_WIDTH
SPLIT_S5 = HY_IN
SPLIT_Q = SPLIT_S5 + S5_WIDTH
SPLIT_K = SPLIT_Q + ATTN_WIDTH
SPLIT_V = SPLIT_K + KV_WIDTH

F32 = jnp.float32
BF16 = jnp.bfloat16
EPS = 1e-6
NEG_INF = -1e30

N_PROMPT_TOK = BATCH * SEQ
N_SAMPLE_TOK = DEC_BATCH * DEC_SEQ
N_TOK = N_PROMPT_TOK + N_SAMPLE_TOK
LANE = 128
MOD_ROWS = 8
TM = 256
VMEM_LIMIT = 56 * 1024 * 1024


def _cparams(*sem):
    return pltpu.CompilerParams(dimension_semantics=sem, vmem_limit_bytes=VMEM_LIMIT)


def _mod_row(i):
    n_p = N_PROMPT_TOK // TM
    per_b = DEC_SEQ // TM
    return jnp.where(i < n_p, 0, 1 + (i - n_p) // per_b)


def _mod_spec(layer, k):
    return pl.BlockSpec((None, 1, D_MODEL), lambda i: ((layer * MOD_ROWS + _mod_row(i)) * 6 + k, 0, 0))


def _resident(shape):
    return pl.BlockSpec(shape, lambda i: (0,) * len(shape), pipeline_mode=pl.Buffered(1))


def _rms(x, g):
    return x * lax.rsqrt(jnp.mean(x * x, axis=-1, keepdims=True) + EPS) * g


def _mod_kernel(c_ref, w_ref, b_ref, o_ref):
    c = c_ref[...]
    s = (c * jax.nn.sigmoid(c)).astype(BF16)
    o_ref[...] = jnp.dot(s, w_ref[...].astype(BF16), preferred_element_type=F32) + b_ref[...]


def _modulation(cvec, ada_w, ada_b):
    tn = 1536
    return pl.pallas_call(
        _mod_kernel,
        grid=(DEPTH, 6 * D_MODEL // tn),
        in_specs=[
            pl.BlockSpec((MOD_ROWS, D_MODEL), lambda l, j: (0, 0)),
            pl.BlockSpec((None, D_MODEL, tn), lambda l, j: (l, 0, j)),
            pl.BlockSpec((None, 1, tn), lambda l, j: (l, 0, j)),
        ],
        out_specs=pl.BlockSpec((None, MOD_ROWS, tn), lambda l, j: (l, 0, j)),
        out_shape=jax.ShapeDtypeStruct((DEPTH, MOD_ROWS, 6 * D_MODEL), F32),
        compiler_params=_cparams("arbitrary", "arbitrary"),
        name="adaln_mod",
    )(cvec, ada_w, ada_b.reshape(DEPTH, 1, 6 * D_MODEL))


def _inproj_kernel(x_ref, sh_ref, sc_ref, g_ref, wm_ref, wg_ref, zm_ref, sg_ref):
    h = (_rms(x_ref[...], g_ref[...]) * (1.0 + sc_ref[...]) + sh_ref[...]).astype(BF16)
    zm_ref[...] = jnp.dot(h, wm_ref[...], preferred_element_type=F32)
    for c in range(0, GATE_IN, D_MODEL):
        sg_ref[:, c:c + D_MODEL] = jax.nn.sigmoid(
            jnp.dot(h, wg_ref[:, c:c + D_MODEL], preferred_element_type=F32))


def _in_proj(x, mod, layer, g, w_main, w_gate):
    return pl.pallas_call(
        _inproj_kernel,
        grid=(N_TOK // TM,),
        in_specs=[
            pl.BlockSpec((TM, D_MODEL), lambda i: (i, 0)),
            _mod_spec(layer, 0),
            _mod_spec(layer, 1),
            _resident((1, D_MODEL)),
            _resident((D_MODEL, MAIN_IN)),
            _resident((D_MODEL, GATE_IN)),
        ],
        out_specs=[
            pl.BlockSpec((TM, MAIN_IN), lambda i: (i, 0)),
            pl.BlockSpec((TM, GATE_IN), lambda i: (i, 0)),
        ],
        out_shape=[
            jax.ShapeDtypeStruct((N_TOK, MAIN_IN), F32),
            jax.ShapeDtypeStruct((N_TOK, GATE_IN), F32),
        ],
        compiler_params=_cparams("arbitrary"),
        name="in_proj",
    )(x, mod, mod, g, w_main, w_gate)


def _merge_kernel(ya_ref, yb_ref, yc_ref, sg_ref, x_ref, g1_ref, pa_ref, pb_ref, pc_ref, wo_ref, o_ref):
    m = sg_ref[:, 0:D_MODEL] * jnp.dot(ya_ref[...].astype(BF16), pa_ref[...], preferred_element_type=F32)
    m = m + sg_ref[:, D_MODEL:2 * D_MODEL] * jnp.dot(yb_ref[...].astype(BF16), pb_ref[...], preferred_element_type=F32)
    m = m + sg_ref[:, 2 * D_MODEL:] * jnp.dot(yc_ref[...].astype(BF16), pc_ref[...], preferred_element_type=F32)
    o_ref[...] = x_ref[...] + g1_ref[...] * jnp.dot(m.astype(BF16), wo_ref[...], preferred_element_type=F32)


def _merge(ya, yb, yc, sg, x, mod, layer, pa, pb, pc, wo):
    return pl.pallas_call(
        _merge_kernel,
        grid=(N_TOK // TM,),
        in_specs=[
            pl.BlockSpec((TM, HY_WIDTH), lambda i: (i, 0)),
            pl.BlockSpec((TM, S5_WIDTH), lambda i: (i, 0)),
            pl.BlockSpec((TM, ATTN_WIDTH), lambda i: (i, 0)),
            pl.BlockSpec((TM, GATE_IN), lambda i: (i, 0)),
            pl.BlockSpec((TM, D_MODEL), lambda i: (i, 0)),
            _mod_spec(layer, 2),
            _resident((HY_WIDTH, D_MODEL)),
            _resident((S5_WIDTH, D_MODEL)),
            _resident((ATTN_WIDTH, D_MODEL)),
            _resident((D_MODEL, D_MODEL)),
        ],
        out_specs=pl.BlockSpec((TM, D_MODEL), lambda i: (i, 0)),
        out_shape=jax.ShapeDtypeStruct((N_TOK, D_MODEL), F32),
        compiler_params=_cparams("arbitrary"),
        name="merge_out",
    )(ya, yb, yc, sg, x, mod, pa, pb, pc, wo)


def _ffn_kernel(x_ref, sh_ref, sc_ref, gt_ref, g_ref, wg_ref, wu_ref, wd_ref, o_ref):
    x = x_ref[...]
    h = (_rms(x, g_ref[...]) * (1.0 + sc_ref[...]) + sh_ref[...]).astype(BF16)
    a = jnp.dot(h, wg_ref[...], preferred_element_type=F32)
    b = jnp.dot(h, wu_ref[...], preferred_element_type=F32)
    f = (a * jax.nn.sigmoid(a) * b).astype(BF16)
    o_ref[...] = x + gt_ref[...] * jnp.dot(f, wd_ref[...], preferred_element_type=F32)


def _ffn(x, mod, layer, g, wg, wu, wd):
    return pl.pallas_call(
        _ffn_kernel,
        grid=(N_TOK // TM,),
        in_specs=[
            pl.BlockSpec((TM, D_MODEL), lambda i: (i, 0)),
            _mod_spec(layer, 3),
            _mod_spec(layer, 4),
            _mod_spec(layer, 5),
            _resident((1, D_MODEL)),
            _resident((D_MODEL, FFN_HIDDEN)),
            _resident((D_MODEL, FFN_HIDDEN)),
            _resident((FFN_HIDDEN, D_MODEL)),
        ],
        out_specs=pl.BlockSpec((TM, D_MODEL), lambda i: (i, 0)),
        out_shape=jax.ShapeDtypeStruct((N_TOK, D_MODEL), F32),
        compiler_params=_cparams("arbitrary"),
        name="ffn",
    )(x, mod, mod, mod, g, wg, wu, wd)


def _final_norm_kernel(x_ref, g_ref, o_ref):
    o_ref[...] = _rms(x_ref[...], g_ref[...])


def _final_norm(x, g):
    tm = 512
    return pl.pallas_call(
        _final_norm_kernel,
        grid=(N_TOK // tm,),
        in_specs=[pl.BlockSpec((tm, D_MODEL), lambda i: (i, 0)), _resident((1, D_MODEL))],
        out_specs=pl.BlockSpec((tm, D_MODEL), lambda i: (i, 0)),
        out_shape=jax.ShapeDtypeStruct((N_TOK, D_MODEL), F32),
        compiler_params=_cparams("arbitrary"),
        name="final_norm",
    )(x, g)


def _short_conv(u, w, b):
    L = u.shape[1]
    pad = HY_SHORT_K // 2
    up = jnp.pad(u, ((0, 0), (pad, pad), (0, 0)))
    out = b
    for j in range(HY_SHORT_K):
        out = out + up[:, j:j + L] * w[j]
    return out


def _hyena_filters_fft(L, p):
    t = jnp.linspace(0.0, 1.0, L, dtype=F32)
    w = 2.0 * math.pi * jnp.arange(L, dtype=F32) / L
    bands = jnp.linspace(1e-4, HY_POS_BANDS - 1, HY_POS_BANDS, dtype=F32)
    ang = w[:, None] * bands[None, :]
    feat = jnp.concatenate([t[:, None], jnp.cos(ang), -jnp.sin(ang)], axis=-1)
    freq = p['hy_sin_freq']
    h = jnp.sin(freq * (feat @ p['hy_pos_w1'] + p['hy_pos_b1']))
    h = jnp.sin(freq * (h @ p['hy_pos_w2'] + p['hy_pos_b2']))
    h = h @ p['hy_pos_w3']
    h = h * jnp.exp(-t[:, None] * jnp.abs(p['hy_decay']))
    h = h.reshape(L, HY_ORDER, 2, HY_WIDTH)
    h_fwd, h_bwd = h[:, :, 0], h[:, :, 1]
    k2 = jnp.concatenate([h_fwd, jnp.zeros((1, HY_ORDER, HY_WIDTH), F32), jnp.flip(h_bwd[1:], axis=0)], axis=0)
    k2 = k2 / jnp.sum(jnp.abs(k2), axis=0, keepdims=True)
    return jnp.fft.rfft(k2, axis=0)


def _fft_conv(u, kf):
    L = u.shape[1]
    uf = jnp.fft.rfft(u, n=2 * L, axis=1)
    return jnp.fft.irfft(uf * kf[None], n=2 * L, axis=1)[:, :L]


def _hyena(zh, p):
    L = zh.shape[1]
    u = _short_conv(zh, p['hy_conv_w'], p['hy_conv_b'])
    v, *gates = jnp.split(u, HY_ORDER + 1, axis=-1)
    kf = _hyena_filters_fft(L, p)
    skip = p['hy_skip']
    z = v
    for o, gate in enumerate(gates):
        z = gate * (_fft_conv(z, kf[:, o]) + skip[o] * z)
    return z


S5_LANES = S5_GROUPS * S5_STATE
S5_SEGS = 8


def _s5_kernel(n_tiles, tile, u0_ref, u1_ref, h0_ref, lam_ref, bd_ref, cd_ref, skip_ref, gw_ref, gb_ref,
               y_ref, fin_ref, bu_ref, ybwd_ref, p_ref, cm_ref, fl_ref, carry_ref, up_ref, st0_ref, st1_ref):
    seg = tile // S5_SEGS
    n = S5_LANES
    b = pl.program_id(0)
    j = pl.program_id(1)

    @pl.when((b == 0) & (j == 0))
    def _powers():
        for d in range(2):
            lr = lam_ref[2 * d:2 * d + 1, :]
            li = lam_ref[2 * d + 1:2 * d + 2, :]
            p_ref[d, 0:1, 0:n] = lr
            p_ref[d, 0:1, n:2 * n] = li

            def body(k, c, d=d, lr=lr, li=li):
                pr, pi = c
                nr = pr * lr - pi * li
                ni = pr * li + pi * lr
                p_ref[d, pl.ds(k, 1), 0:n] = nr
                p_ref[d, pl.ds(k, 1), n:2 * n] = ni
                return nr, ni

            lax.fori_loop(1, seg, body, (lr, li))

    def sweep(d, t):
        first, last = (0, n_tiles - 1) if d == 0 else (n_tiles - 1, 0)

        @pl.when(t == first)
        def _init():
            carry_ref[0:1, 0:n] = h0_ref[2 * d:2 * d + 1, :]
            carry_ref[0:1, n:2 * n] = h0_ref[2 * d + 1:2 * d + 2, :]

        def gather(i, c):
            r = pl.ds(pl.multiple_of(i * S5_SEGS, S5_SEGS), S5_SEGS)
            up_ref[r, 0:LANE] = u0_ref[pl.ds(i, S5_SEGS, stride=seg), :]
            up_ref[r, LANE:2 * LANE] = u1_ref[pl.ds(i, S5_SEGS, stride=seg), :]
            return c

        lax.fori_loop(0, seg, gather, 0)
        up = up_ref[...]
        ub = up.astype(BF16)
        for c0 in range(0, 2 * n, 512):
            bu_ref[:, c0:c0 + 512] = jnp.dot(ub, bd_ref[d, :, c0:c0 + 512], preferred_element_type=F32)

        lr8 = jnp.broadcast_to(lam_ref[2 * d:2 * d + 1, :], (S5_SEGS, n))
        li8 = jnp.broadcast_to(lam_ref[2 * d + 1:2 * d + 2, :], (S5_SEGS, n))

        def rows(i):
            k = i if d == 0 else seg - 1 - i
            return pl.ds(pl.multiple_of(k * S5_SEGS, S5_SEGS), S5_SEGS)

        def local(i, c):
            hr, hi = c
            r = rows(i)
            nr = lr8 * hr - li8 * hi + bu_ref[r, 0:n]
            ni = lr8 * hi + li8 * hr + bu_ref[r, n:2 * n]
            bu_ref[r, 0:n] = nr
            bu_ref[r, n:2 * n] = ni
            return nr, ni

        zero = jnp.zeros((S5_SEGS, n), F32)
        fr, fi = lax.fori_loop(0, seg, local, (zero, zero))
        fl_ref[:, 0:n] = fr
        fl_ref[:, n:2 * n] = fi

        plr = p_ref[d, seg - 1:seg, 0:n]
        pli = p_ref[d, seg - 1:seg, n:2 * n]
        cr = carry_ref[0:1, 0:n]
        ci = carry_ref[0:1, n:2 * n]
        for s in (range(S5_SEGS) if d == 0 else reversed(range(S5_SEGS))):
            cm_ref[s:s + 1, 0:n] = cr
            cm_ref[s:s + 1, n:2 * n] = ci
            flr = fl_ref[s:s + 1, 0:n]
            fli = fl_ref[s:s + 1, n:2 * n]
            cr, ci = plr * cr - pli * ci + flr, plr * ci + pli * cr + fli
        carry_ref[0:1, 0:n] = cr
        carry_ref[0:1, n:2 * n] = ci

        @pl.when(t == last)
        def _final():
            fin_ref[2 * d:2 * d + 1, :] = cr
            fin_ref[2 * d + 1:2 * d + 2, :] = ci

        cmr = cm_ref[:, 0:n]
        cmi = cm_ref[:, n:2 * n]

        def fix(i, c):
            r = rows(i)
            pr = p_ref[d, pl.ds(i, 1), 0:n]
            pi = p_ref[d, pl.ds(i, 1), n:2 * n]
            bu_ref[r, 0:n] = bu_ref[r, 0:n] + (pr * cmr - pi * cmi)
            bu_ref[r, n:2 * n] = bu_ref[r, n:2 * n] + (pr * cmi + pi * cmr)
            return c

        lax.fori_loop(0, seg, fix, 0)

        y = jnp.dot(bu_ref[...].astype(BF16), cd_ref[d], preferred_element_type=F32)
        trow = pl.ds(pl.multiple_of(t * tile, tile), tile)
        if d == 1:
            ybwd_ref[trow, :] = y
        else:
            tot = up * skip_ref[...] + y + ybwd_ref[trow, :]
            g = jnp.dot(tot.astype(BF16), gw_ref[...], preferred_element_type=F32) + gb_ref[...]
            res = g[:, :S5_WIDTH] * jax.nn.sigmoid(g[:, S5_WIDTH:])
            st0_ref[...] = res[:, 0:LANE]
            st1_ref[...] = res[:, LANE:2 * LANE]
            per_seg = seg // S5_SEGS

            def unpermute(m, c):
                s = m // per_seg
                i0 = (m % per_seg) * S5_SEGS
                src = pl.ds(i0 * S5_SEGS + s, S5_SEGS, stride=S5_SEGS)
                dst = pl.ds(pl.multiple_of(m * S5_SEGS, S5_SEGS), S5_SEGS)
                y_ref[dst, 0:LANE] = st0_ref[src, :]
                y_ref[dst, LANE:2 * LANE] = st1_ref[src, :]
                return c

            lax.fori_loop(0, tile // S5_SEGS, unpermute, 0)

    @pl.when(j < n_tiles)
    def _bwd():
        sweep(1, n_tiles - 1 - j)

    @pl.when(j >= n_tiles)
    def _fwd():
        sweep(0, j - n_tiles)


def _s5_branch(zm, row0, n_seq, seq_len, tile, h0, lam, bd, cd, skip, gw, gb):
    n_tiles = seq_len // tile
    blk0 = row0 // tile
    col = SPLIT_S5 // LANE
    seg = tile // S5_SEGS

    def tile_of(j):
        return jnp.where(j < n_tiles, n_tiles - 1 - j, j - n_tiles)

    return pl.pallas_call(
        functools.partial(_s5_kernel, n_tiles, tile),
        grid=(n_seq, 2 * n_tiles),
        in_specs=[
            pl.BlockSpec((tile, LANE), lambda b, j: (blk0 + b * n_tiles + tile_of(j), col)),
            pl.BlockSpec((tile, LANE), lambda b, j: (blk0 + b * n_tiles + tile_of(j), col + 1)),
            pl.BlockSpec((None, 4, S5_LANES), lambda b, j: (b, 0, 0)),
            pl.BlockSpec((4, S5_LANES), lambda b, j: (0, 0)),
            pl.BlockSpec((2, S5_WIDTH, 2 * S5_LANES), lambda b, j: (0, 0, 0)),
            pl.BlockSpec((2, 2 * S5_LANES, S5_WIDTH), lambda b, j: (0, 0, 0)),
            pl.BlockSpec((1, S5_WIDTH), lambda b, j: (0, 0)),
            pl.BlockSpec((S5_WIDTH, 2 * S5_WIDTH), lambda b, j: (0, 0)),
            pl.BlockSpec((1, 2 * S5_WIDTH), lambda b, j: (0, 0)),
        ],
        out_specs=[
            pl.BlockSpec((tile, S5_WIDTH), lambda b, j: (b * n_tiles + jnp.maximum(j - n_tiles, 0), 0)),
            pl.BlockSpec((None, 4, S5_LANES), lambda b, j: (b, 0, 0)),
        ],
        out_shape=[
            jax.ShapeDtypeStruct((n_seq * seq_len, S5_WIDTH), F32),
            jax.ShapeDtypeStruct((n_seq, 4, S5_LANES), F32),
        ],
        scratch_shapes=[
            pltpu.VMEM((tile, 2 * S5_LANES), F32),
            pltpu.VMEM((seq_len, S5_WIDTH), F32),
            pltpu.VMEM((2, seg, 2 * S5_LANES), F32),
            pltpu.VMEM((S5_SEGS, 2 * S5_LANES), F32),
            pltpu.VMEM((S5_SEGS, 2 * S5_LANES), F32),
            pltpu.VMEM((1, 2 * S5_LANES), F32),
            pltpu.VMEM((tile, S5_WIDTH), F32),
            pltpu.VMEM((tile, LANE), F32),
            pltpu.VMEM((tile, LANE), F32),
        ],
        compiler_params=_cparams("arbitrary", "arbitrary"),
        name="s5_scan",
    )(zm, zm, h0, lam, bd, cd, skip, gw, gb)


def _s5_params(lam_re, lam_im, log_step, b_re, b_im, c_re, c_im):
    lam = lax.complex(lam_re, lam_im)
    lam_bar = jnp.exp(lam * jnp.exp(log_step)[..., None])
    b_bar = ((lam_bar - 1.0) / lam)[..., None] * lax.complex(b_re, b_im)
    eye = jnp.eye(S5_GROUPS, dtype=F32)

    def in_mat(x):
        xt = jnp.swapaxes(x, 2, 3)
        return (xt[:, :, :, None, :] * eye[None, :, None, :, None]).reshape(2, S5_WIDTH, S5_LANES)

    def out_mat(x):
        xt = jnp.swapaxes(x, 2, 3)
        return (xt[:, :, :, None, :] * eye[None, :, None, :, None]).reshape(2, S5_LANES, S5_WIDTH)

    bd = jnp.concatenate([in_mat(jnp.real(b_bar)), in_mat(jnp.imag(b_bar))], axis=2).astype(BF16)
    cd = jnp.concatenate([out_mat(c_re), out_mat(-c_im)], axis=1).astype(BF16)
    lam_rows = jnp.stack([jnp.real(lam_bar[0]), jnp.imag(lam_bar[0]), jnp.real(lam_bar[1]), jnp.imag(lam_bar[1])])
    return lam_rows.reshape(4, S5_LANES), bd, cd


def _axial_rope(x):
    L = x.shape[1]
    rows = L // GRID_W
    row = jnp.repeat(jnp.arange(rows, dtype=F32), GRID_W)
    col = jnp.tile(jnp.arange(GRID_W, dtype=F32), rows)
    n_freq = HEAD_DIM // 4
    inv = ROPE_BASE ** (-jnp.arange(n_freq, dtype=F32) / n_freq)

    def rot(xp, pos):
        ang = pos[:, None] * inv
        cos = jnp.cos(ang)[None, :, None, :]
        sin = jnp.sin(ang)[None, :, None, :]
        x1, x2 = jnp.split(xp, 2, axis=-1)
        return jnp.concatenate([x1 * cos - x2 * sin, x1 * sin + x2 * cos], axis=-1)

    half = HEAD_DIM // 2
    return jnp.concatenate([rot(x[..., :half], row), rot(x[..., half:], col)], axis=-1)


def _attend(q, k, v, sink, mask):
    s = jnp.einsum('bqgrd,bkgd->bgrqk', q, k) * (HEAD_DIM ** -0.5)
    if mask is not None:
        s = jnp.where(mask, s, NEG_INF)
    sk = jnp.broadcast_to(sink[None, :, :, None, None], s.shape[:-1] + (1,))
    pr = jax.nn.softmax(jnp.concatenate([s, sk], axis=-1), axis=-1)[..., :-1]
    return jnp.einsum('bgrqk,bkgd->bqgrd', pr, v)


def _context_attention(q, k, v, sink):
    B, L = q.shape[:2]
    nb = L // BLOCK
    qb = q.reshape(B, nb, BLOCK, N_KV_HEADS, Q_PER_KV, HEAD_DIM).swapaxes(0, 1)
    out = lax.map(lambda qi: _attend(qi, k, v, sink, None), qb)
    return out.swapaxes(0, 1).reshape(B, L, ATTN_WIDTH)


def _latent_attention(q, k, v, kc, vc, sink):
    B, L = q.shape[:2]
    nb = L // BLOCK
    span = BLOCK + 2 * WINDOW
    lc = kc.shape[1]
    idx = jnp.arange(nb)[:, None] * BLOCK + jnp.arange(span)[None, :]
    kp = jnp.pad(k, ((0, 0), (WINDOW, WINDOW), (0, 0), (0, 0)))
    vp = jnp.pad(v, ((0, 0), (WINDOW, WINDOW), (0, 0), (0, 0)))
    kw = kp[:, idx].swapaxes(0, 1)
    vw = vp[:, idx].swapaxes(0, 1)
    qpos = jnp.arange(nb)[:, None] * BLOCK + jnp.arange(BLOCK)[None, :]
    kpos = idx - WINDOW
    win = (kpos[:, None, :] >= 0) & (kpos[:, None, :] < L) & (jnp.abs(qpos[:, :, None] - kpos[:, None, :]) <= WINDOW)
    mask = jnp.concatenate([jnp.ones((nb, BLOCK, lc), dtype=bool), win], axis=-1)
    qb = q.reshape(B, nb, BLOCK, N_KV_HEADS, Q_PER_KV, HEAD_DIM).swapaxes(0, 1)

    def one(args):
        qi, kwi, vwi, mi = args
        return _attend(qi, jnp.concatenate([kc, kwi], axis=1), jnp.concatenate([vc, vwi], axis=1), sink, mi)

    out = lax.map(one, (qb, kw, vw, mask))
    return out.swapaxes(0, 1).reshape(B, L, ATTN_WIDTH)


def _mixers(zm, p, ctx_kv):
    B, L, _ = zm.shape
    za = zm[..., :SPLIT_S5]
    zq, zk, zv = zm[..., SPLIT_Q:SPLIT_K], zm[..., SPLIT_K:SPLIT_V], zm[..., SPLIT_V:]
    ya = _hyena(za, p)
    q = zq.reshape(B, L, N_HEADS, HEAD_DIM)
    k = zk.reshape(B, L, N_KV_HEADS, HEAD_DIM)
    v = zv.reshape(B, L, N_KV_HEADS, HEAD_DIM)
    sink = p['attn_sink'].reshape(N_KV_HEADS, Q_PER_KV)
    if ctx_kv is None:
        yc = _context_attention(q.reshape(B, L, N_KV_HEADS, Q_PER_KV, HEAD_DIM), k, v, sink)
    else:
        q_rot = _axial_rope(q).reshape(B, L, N_KV_HEADS, Q_PER_KV, HEAD_DIM)
        yc = _latent_attention(q_rot, _axial_rope(k), v, ctx_kv[0], ctx_kv[1], sink)
    return ya, yc, k, v


def kernel(x_prompt, x_sample, cache_k, cache_v, state_s5_re, state_s5_im, c, c_ctx, ada_w, ada_b, norm1_g, w_in, hy_conv_w, hy_conv_b, hy_pos_w1, hy_pos_b1, hy_pos_w2, hy_pos_b2, hy_pos_w3, hy_sin_freq, hy_decay, hy_skip, s5_lam_re, s5_lam_im, s5_log_step, s5_b_re, s5_b_im, s5_c_re, s5_c_im, s5_skip, s5_glu_w, s5_glu_b, attn_sink, proj_a, proj_b, proj_c, w_out, norm2_g, ffn_w_gate, ffn_w_up, ffn_w_down, final_norm_g):
    cvec = jnp.concatenate([c_ctx[None], c, jnp.zeros((MOD_ROWS - 1 - DEC_BATCH, D_MODEL), F32)], axis=0)
    mod = _modulation(cvec, ada_w, ada_b).reshape(DEPTH * MOD_ROWS * 6, 1, D_MODEL)

    x = jnp.concatenate([x_prompt.reshape(N_PROMPT_TOK, D_MODEL), x_sample.reshape(N_SAMPLE_TOK, D_MODEL)], axis=0)
    w_main = w_in[:, :, :MAIN_IN].astype(BF16)
    w_gate = w_in[:, :, MAIN_IN:].astype(BF16)
    pa, pb, pc, wo = proj_a.astype(BF16), proj_b.astype(BF16), proj_c.astype(BF16), w_out.astype(BF16)
    wg, wu, wd = ffn_w_gate.astype(BF16), ffn_w_up.astype(BF16), ffn_w_down.astype(BF16)

    ks_out, vs_out, sre_out, sim_out = [], [], [], []
    for l in range(DEPTH):
        p = {
            'hy_conv_w': hy_conv_w[l], 'hy_conv_b': hy_conv_b[l],
            'hy_pos_w1': hy_pos_w1[l], 'hy_pos_b1': hy_pos_b1[l],
            'hy_pos_w2': hy_pos_w2[l], 'hy_pos_b2': hy_pos_b2[l], 'hy_pos_w3': hy_pos_w3[l],
            'hy_sin_freq': hy_sin_freq[l], 'hy_decay': hy_decay[l], 'hy_skip': hy_skip[l],
            'attn_sink': attn_sink[l],
        }
        zm, sg = _in_proj(x, mod, l, norm1_g[l].reshape(1, D_MODEL), w_main[l], w_gate[l])
        zm_p = zm[:N_PROMPT_TOK].reshape(BATCH, SEQ, MAIN_IN)
        zm_s = zm[N_PROMPT_TOK:].reshape(DEC_BATCH, DEC_SEQ, MAIN_IN)
        ya_p, yc_p, k_l, v_l = _mixers(zm_p, p, None)
        ya_s, yc_s, _, _ = _mixers(zm_s, p, (cache_k[:, l], cache_v[:, l]))

        lam, bd, cd = _s5_params(s5_lam_re[l], s5_lam_im[l], s5_log_step[l], s5_b_re[l], s5_b_im[l],
                                 s5_c_re[l], s5_c_im[l])
        s5_w = (lam, bd, cd, s5_skip[l].reshape(1, S5_WIDTH), s5_glu_w[l].astype(BF16),
                s5_glu_b[l].reshape(1, 2 * S5_WIDTH))
        h0_s = jnp.stack([state_s5_re[:, l, 0], state_s5_im[:, l, 0], state_s5_re[:, l, 1], state_s5_im[:, l, 1]],
                         axis=1).reshape(DEC_BATCH, 4, S5_LANES)
        yb_p, fin_p = _s5_branch(zm, 0, BATCH, SEQ, SEQ, jnp.zeros((BATCH, 4, S5_LANES), F32), *s5_w)
        yb_s, _ = _s5_branch(zm, N_PROMPT_TOK, DEC_BATCH, DEC_SEQ, 512, h0_s, *s5_w)
        fin_p = fin_p.reshape(BATCH, 2, 2, S5_GROUPS, S5_STATE)

        ks_out.append(k_l)
        vs_out.append(v_l)
        sre_out.append(fin_p[:, :, 0])
        sim_out.append(fin_p[:, :, 1])
        ya = jnp.concatenate([ya_p.reshape(N_PROMPT_TOK, -1), ya_s.reshape(N_SAMPLE_TOK, -1)], axis=0)
        yb = jnp.concatenate([yb_p, yb_s], axis=0)
        yc = jnp.concatenate([yc_p.reshape(N_PROMPT_TOK, -1), yc_s.reshape(N_SAMPLE_TOK, -1)], axis=0)
        x = _merge(ya, yb, yc, sg, x, mod, l, pa[l], pb[l], pc[l], wo[l])
        x = _ffn(x, mod, l, norm2_g[l].reshape(1, D_MODEL), wg[l], wu[l], wd[l])

    y = _final_norm(x, final_norm_g.reshape(1, D_MODEL))
    y_prompt = y[:N_PROMPT_TOK].reshape(BATCH, SEQ, D_MODEL)
    y_sample = y[N_PROMPT_TOK:].reshape(DEC_BATCH, DEC_SEQ, D_MODEL)
    return (y_prompt, y_sample, jnp.stack(ks_out, axis=1), jnp.stack(vs_out, axis=1),
            jnp.stack(sre_out, axis=1), jnp.stack(sim_out, axis=1))
```

```python
import functools
import math

import jax
import jax.numpy as jnp
import numpy as np
from jax import lax
from jax.experimental import pallas as pl
from jax.experimental.pallas import tpu as pltpu

D_MODEL = 1024
BATCH = 16
SEQ = 256
DEPTH = 4
DEC_BATCH = 2
DEC_SEQ = 4096
PAST_LEN = 512
GRID_W = 64
N_BRANCH = 3
HY_WIDTH = 256
HY_ORDER = 2
HY_SHORT_K = 3
HY_POS_EMB = 33
HY_POS_BANDS = (HY_POS_EMB - 1) // 2
HY_FILTER_HIDDEN = 64
S5_WIDTH = 256
S5_GROUP = 16
S5_GROUPS = S5_WIDTH // S5_GROUP
S5_STATE = 64
N_HEADS = 8
N_KV_HEADS = 2
Q_PER_KV = N_HEADS // N_KV_HEADS
HEAD_DIM = 64
ATTN_WIDTH = N_HEADS * HEAD_DIM
KV_WIDTH = N_KV_HEADS * HEAD_DIM
WINDOW = 128
BLOCK = 128
ROPE_BASE = 10000.0
FFN_HIDDEN = ((8 * D_MODEL // 3 + 255) // 256) * 256
HY_IN = (HY_ORDER + 1) * HY_WIDTH
GATE_IN = N_BRANCH * D_MODEL
MAIN_IN = HY_IN + S5_WIDTH + ATTN_WIDTH + 2 * KV_WIDTH
SPLIT_S5 = HY_IN
SPLIT_Q = SPLIT_S5 + S5_WIDTH
SPLIT_K = SPLIT_Q + ATTN_WIDTH
SPLIT_V = SPLIT_K + KV_WIDTH

F32 = jnp.float32
BF16 = jnp.bfloat16
EPS = 1e-6
NEG_INF = -1e30

N_PROMPT_TOK = BATCH * SEQ
N_SAMPLE_TOK = DEC_BATCH * DEC_SEQ
N_TOK = N_PROMPT_TOK + N_SAMPLE_TOK
LANE = 128
MOD_ROWS = 8
TM = 256
VMEM_LIMIT = 56 * 1024 * 1024


def _cparams(*sem):
    return pltpu.CompilerParams(dimension_semantics=sem, vmem_limit_bytes=VMEM_LIMIT)


def _mod_row(i):
    n_p = N_PROMPT_TOK // TM
    per_b = DEC_SEQ // TM
    return jnp.where(i < n_p, 0, 1 + (i - n_p) // per_b)


def _mod_spec(layer, k):
    return pl.BlockSpec((None, 1, D_MODEL), lambda i: ((layer * MOD_ROWS + _mod_row(i)) * 6 + k, 0, 0))


def _resident(shape):
    return pl.BlockSpec(shape, lambda i: (0,) * len(shape), pipeline_mode=pl.Buffered(1))


def _rms(x, g):
    return x * lax.rsqrt(jnp.mean(x * x, axis=-1, keepdims=True) + EPS) * g


def _mod_kernel(c_ref, w_ref, b_ref, o_ref):
    c = c_ref[...]
    s = (c * jax.nn.sigmoid(c)).astype(BF16)
    o_ref[...] = jnp.dot(s, w_ref[...].astype(BF16), preferred_element_type=F32) + b_ref[...]


def _modulation(cvec, ada_w, ada_b):
    tn = 1536
    return pl.pallas_call(
        _mod_kernel,
        grid=(DEPTH, 6 * D_MODEL // tn),
        in_specs=[
            pl.BlockSpec((MOD_ROWS, D_MODEL), lambda l, j: (0, 0)),
            pl.BlockSpec((None, D_MODEL, tn), lambda l, j: (l, 0, j)),
            pl.BlockSpec((None, 1, tn), lambda l, j: (l, 0, j)),
        ],
        out_specs=pl.BlockSpec((None, MOD_ROWS, tn), lambda l, j: (l, 0, j)),
        out_shape=jax.ShapeDtypeStruct((DEPTH, MOD_ROWS, 6 * D_MODEL), F32),
        compiler_params=_cparams("arbitrary", "arbitrary"),
        name="adaln_mod",
    )(cvec, ada_w, ada_b.reshape(DEPTH, 1, 6 * D_MODEL))


def _inproj_kernel(x_ref, sh_ref, sc_ref, g_ref, wm_ref, wg_ref, zm_ref, sg_ref):
    h = (_rms(x_ref[...], g_ref[...]) * (1.0 + sc_ref[...]) + sh_ref[...]).astype(BF16)
    zm_ref[...] = jnp.dot(h, wm_ref[...], preferred_element_type=F32)
    for c in range(0, GATE_IN, D_MODEL):
        sg_ref[:, c:c + D_MODEL] = jax.nn.sigmoid(
            jnp.dot(h, wg_ref[:, c:c + D_MODEL], preferred_element_type=F32))


def _in_proj(x, mod, layer, g, w_main, w_gate):
    return pl.pallas_call(
        _inproj_kernel,
        grid=(N_TOK // TM,),
        in_specs=[
            pl.BlockSpec((TM, D_MODEL), lambda i: (i, 0)),
            _mod_spec(layer, 0),
            _mod_spec(layer, 1),
            _resident((1, D_MODEL)),
            _resident((D_MODEL, MAIN_IN)),
            _resident((D_MODEL, GATE_IN)),
        ],
        out_specs=[
            pl.BlockSpec((TM, MAIN_IN), lambda i: (i, 0)),
            pl.BlockSpec((TM, GATE_IN), lambda i: (i, 0)),
        ],
        out_shape=[
            jax.ShapeDtypeStruct((N_TOK, MAIN_IN), F32),
            jax.ShapeDtypeStruct((N_TOK, GATE_IN), F32),
        ],
        compiler_params=_cparams("arbitrary"),
        name="in_proj",
    )(x, mod, mod, g, w_main, w_gate)


def _merge_kernel(ya_ref, yb_ref, yc_ref, sg_ref, x_ref, g1_ref, pa_ref, pb_ref, pc_ref, wo_ref, o_ref):
    m = sg_ref[:, 0:D_MODEL] * jnp.dot(ya_ref[...].astype(BF16), pa_ref[...], preferred_element_type=F32)
    m = m + sg_ref[:, D_MODEL:2 * D_MODEL] * jnp.dot(yb_ref[...].astype(BF16), pb_ref[...], preferred_element_type=F32)
    m = m + sg_ref[:, 2 * D_MODEL:] * jnp.dot(yc_ref[...].astype(BF16), pc_ref[...], preferred_element_type=F32)
    o_ref[...] = x_ref[...] + g1_ref[...] * jnp.dot(m.astype(BF16), wo_ref[...], preferred_element_type=F32)


def _merge(ya, yb, yc, sg, x, mod, layer, pa, pb, pc, wo):
    return pl.pallas_call(
        _merge_kernel,
        grid=(N_TOK // TM,),
        in_specs=[
            pl.BlockSpec((TM, HY_WIDTH), lambda i: (i, 0)),
            pl.BlockSpec((TM, S5_WIDTH), lambda i: (i, 0)),
            pl.BlockSpec((TM, ATTN_WIDTH), lambda i: (i, 0)),
            pl.BlockSpec((TM, GATE_IN), lambda i: (i, 0)),
            pl.BlockSpec((TM, D_MODEL), lambda i: (i, 0)),
            _mod_spec(layer, 2),
            _resident((HY_WIDTH, D_MODEL)),
            _resident((S5_WIDTH, D_MODEL)),
            _resident((ATTN_WIDTH, D_MODEL)),
            _resident((D_MODEL, D_MODEL)),
        ],
        out_specs=pl.BlockSpec((TM, D_MODEL), lambda i: (i, 0)),
        out_shape=jax.ShapeDtypeStruct((N_TOK, D_MODEL), F32),
        compiler_params=_cparams("arbitrary"),
        name="merge_out",
    )(ya, yb, yc, sg, x, mod, pa, pb, pc, wo)


def _ffn_kernel(x_ref, sh_ref, sc_ref, gt_ref, g_ref, wg_ref, wu_ref, wd_ref, o_ref):
    x = x_ref[...]
    h = (_rms(x, g_ref[...]) * (1.0 + sc_ref[...]) + sh_ref[...]).astype(BF16)
    a = jnp.dot(h, wg_ref[...], preferred_element_type=F32)
    b = jnp.dot(h, wu_ref[...], preferred_element_type=F32)
    f = (a * jax.nn.sigmoid(a) * b).astype(BF16)
    o_ref[...] = x + gt_ref[...] * jnp.dot(f, wd_ref[...], preferred_element_type=F32)


def _ffn(x, mod, layer, g, wg, wu, wd):
    return pl.pallas_call(
        _ffn_kernel,
        grid=(N_TOK // TM,),
        in_specs=[
            pl.BlockSpec((TM, D_MODEL), lambda i: (i, 0)),
            _mod_spec(layer, 3),
            _mod_spec(layer, 4),
            _mod_spec(layer, 5),
            _resident((1, D_MODEL)),
            _resident((D_MODEL, FFN_HIDDEN)),
            _resident((D_MODEL, FFN_HIDDEN)),
            _resident((FFN_HIDDEN, D_MODEL)),
        ],
        out_specs=pl.BlockSpec((TM, D_MODEL), lambda i: (i, 0)),
        out_shape=jax.ShapeDtypeStruct((N_TOK, D_MODEL), F32),
        compiler_params=_cparams("arbitrary"),
        name="ffn",
    )(x, mod, mod, mod, g, wg, wu, wd)


def _final_norm_kernel(x_ref, g_ref, o_ref):
    o_ref[...] = _rms(x_ref[...], g_ref[...])


def _final_norm(x, g):
    tm = 512
    return pl.pallas_call(
        _final_norm_kernel,
        grid=(N_TOK // tm,),
        in_specs=[pl.BlockSpec((tm, D_MODEL), lambda i: (i, 0)), _resident((1, D_MODEL))],
        out_specs=pl.BlockSpec((tm, D_MODEL), lambda i: (i, 0)),
        out_shape=jax.ShapeDtypeStruct((N_TOK, D_MODEL), F32),
        compiler_params=_cparams("arbitrary"),
        name="final_norm",
    )(x, g)


SUBLANE = 8
HY_TILE = 256


def _dft_mats(seq_len):
    n = 2 * seq_len
    r = 1 << (seq_len.bit_length() // 2)
    k = jnp.arange(seq_len, dtype=jnp.int32)[:, None]
    t1 = jnp.arange(seq_len // r, dtype=jnp.int32)[None, :] * r
    t0 = jnp.arange(r, dtype=jnp.int32)[None, :]

    def cs(m):
        ang = (2.0 * math.pi / n) * ((m % n).astype(F32))
        return jnp.cos(ang), jnp.sin(ang)

    ac, as_ = cs(k * t1)
    bc, bs = cs(k * t0)
    cos = (ac[:, :, None] * bc[:, None, :] - as_[:, :, None] * bs[:, None, :]).reshape(seq_len, seq_len)
    sin = (as_[:, :, None] * bc[:, None, :] + ac[:, :, None] * bs[:, None, :]).reshape(seq_len, seq_len)
    alt = jnp.where(jnp.arange(seq_len) % 2 == 0, 1.0, -1.0).astype(F32)
    row0 = (jnp.arange(seq_len) == 0)[:, None]
    return jnp.stack([cos, jnp.where(row0, alt[None, :], -sin)]).astype(BF16)


def _short_conv_kernel(x_ref, xp_ref, xn_ref, w_ref, b_ref, u_ref, vb_ref):
    i = pl.program_id(0)
    n_p = N_PROMPT_TOK // HY_TILE
    per_p, per_s = SEQ // HY_TILE, DEC_SEQ // HY_TILE
    idx = jnp.where(i < n_p, i % per_p, (i - n_p) % per_s)
    per = jnp.where(i < n_p, per_p, per_s)
    has_prev = (idx > 0).astype(F32)
    has_next = (idx < per - 1).astype(F32)
    x = x_ref[...]
    row = lax.broadcasted_iota(jnp.int32, (HY_TILE, 1), 0)
    x_prev = jnp.where(row == 0, xp_ref[SUBLANE - 1:SUBLANE, :] * has_prev, pltpu.roll(x, 1, axis=0))
    x_next = jnp.where(row == HY_TILE - 1, xn_ref[0:1, :] * has_next, pltpu.roll(x, HY_TILE - 1, axis=0))
    u = b_ref[...] + x_prev * w_ref[0:1, :] + x * w_ref[1:2, :] + x_next * w_ref[2:3, :]
    u_ref[...] = u
    vb_ref[...] = u[:, :HY_WIDTH].astype(BF16)


def _short_conv(zm, w, b):
    per8 = HY_TILE // SUBLANE
    last8 = N_TOK // SUBLANE - 1
    return pl.pallas_call(
        _short_conv_kernel,
        grid=(N_TOK // HY_TILE,),
        in_specs=[
            pl.BlockSpec((HY_TILE, HY_IN), lambda i: (i, 0)),
            pl.BlockSpec((SUBLANE, HY_IN), lambda i: (jnp.maximum(i * per8 - 1, 0), 0)),
            pl.BlockSpec((SUBLANE, HY_IN), lambda i: (jnp.minimum((i + 1) * per8, last8), 0)),
            pl.BlockSpec((HY_SHORT_K, HY_IN), lambda i: (0, 0)),
            pl.BlockSpec((1, HY_IN), lambda i: (0, 0)),
        ],
        out_specs=[
            pl.BlockSpec((HY_TILE, HY_IN), lambda i: (i, 0)),
            pl.BlockSpec((HY_TILE, HY_WIDTH), lambda i: (i, 0)),
        ],
        out_shape=[
            jax.ShapeDtypeStruct((N_TOK, HY_IN), F32),
            jax.ShapeDtypeStruct((N_TOK, HY_WIDTH), BF16),
        ],
        compiler_params=_cparams("arbitrary"),
        name="hyena_short_conv",
    )(zm, zm, zm, w, b)


def _dft_fwd_kernel(seq_len, row0, has_tables, z_ref, f_ref, *rest):
    b = pl.program_id(1)
    z = z_ref[pl.ds(pl.multiple_of(row0 + b * seq_len, seq_len), seq_len), :]
    xr = jnp.dot(f_ref[0], z, preferred_element_type=F32)
    xi = jnp.dot(f_ref[1], z, preferred_element_type=F32)
    if has_tables:
        t_ref, o_ref = rest
        o_ref[0] = (xr * t_ref[0] - xi * t_ref[1]).astype(BF16)
        o_ref[1] = (xr * t_ref[1] + xi * t_ref[2]).astype(BF16)
    else:
        (o_ref,) = rest
        o_ref[0] = xr
        o_ref[1] = xi


def _dft_fwd(z, row0, n_seq, seq_len, mats, tables):
    tm = min(seq_len, 512)
    has_tables = tables is not None
    in_specs = [
        pl.BlockSpec(z.shape, lambda i, b: (0, 0)),
        pl.BlockSpec((2, tm, seq_len), lambda i, b: (0, i, 0)),
    ]
    args = [z, mats]
    if has_tables:
        in_specs.append(pl.BlockSpec((3, tm, HY_WIDTH), lambda i, b: (0, i, 0)))
        args.append(tables)
    return pl.pallas_call(
        functools.partial(_dft_fwd_kernel, seq_len, row0, has_tables),
        grid=(seq_len // tm, n_seq),
        in_specs=in_specs,
        out_specs=pl.BlockSpec((None, 2, tm, HY_WIDTH), lambda i, b: (b, 0, i, 0)),
        out_shape=jax.ShapeDtypeStruct((n_seq, 2, seq_len, HY_WIDTH), BF16 if has_tables else F32),
        compiler_params=_cparams("arbitrary", "arbitrary"),
        name="hyena_dft_fwd",
    )(*args)


def _dft_inv_kernel(tm, last, y_ref, f_ref, g_ref, zp_ref, skip_ref, o_ref, *rest):
    i = pl.program_id(0)
    b = pl.program_id(1)
    yr = y_ref[b, 0]
    yi = y_ref[b, 1]
    conv_c = jnp.dot(f_ref[0], yr, preferred_element_type=F32)
    conv_s = jnp.dot(f_ref[1], yi, preferred_element_type=F32)
    t = i * tm + lax.broadcasted_iota(jnp.int32, (tm, 1), 0)
    alt = jnp.where(t % 2 == 0, 1.0, -1.0).astype(F32)
    conv = conv_c + jnp.where(t == 0, 0.0, conv_s) + alt * yi[0:1, :].astype(F32)
    z = g_ref[...] * (conv + skip_ref[...] * zp_ref[...])
    o_ref[...] = z
    if not last:
        rest[0][...] = z.astype(BF16)


def _dft_inv(y, mats, u, gate_col, z_prev, z_prev_row0, skip, row0, n_seq, seq_len, last):
    tm = min(seq_len, 512)
    n_m = seq_len // tm

    def rows(base):
        return lambda i, b: (base // tm + b * n_m + i, 0)

    out_spec = pl.BlockSpec((tm, HY_WIDTH), rows(0))
    out_shape = [jax.ShapeDtypeStruct((n_seq * seq_len, HY_WIDTH), F32)]
    if not last:
        out_shape.append(jax.ShapeDtypeStruct((n_seq * seq_len, HY_WIDTH), BF16))
    return pl.pallas_call(
        functools.partial(_dft_inv_kernel, tm, last),
        grid=(n_m, n_seq),
        in_specs=[
            pl.BlockSpec(y.shape, lambda i, b: (0, 0, 0, 0)),
            pl.BlockSpec((2, tm, seq_len), lambda i, b: (0, i, 0)),
            pl.BlockSpec((tm, HY_WIDTH), lambda i, b: (row0 // tm + b * n_m + i, gate_col)),
            pl.BlockSpec((tm, HY_WIDTH), rows(z_prev_row0)),
            pl.BlockSpec((1, HY_WIDTH), lambda i, b: (0, 0)),
        ],
        out_specs=[out_spec] * len(out_shape),
        out_shape=out_shape,
        compiler_params=_cparams("arbitrary", "arbitrary"),
        name="hyena_dft_inv",
    )(y, mats, u, z_prev, skip)


def _hyena_taps(seq_len, p):
    t = jnp.linspace(0.0, 1.0, seq_len, dtype=F32)
    w = 2.0 * math.pi * jnp.arange(seq_len, dtype=F32) / seq_len
    bands = jnp.linspace(1e-4, HY_POS_BANDS - 1, HY_POS_BANDS, dtype=F32)
    ang = w[:, None] * bands[None, :]
    feat = jnp.concatenate([t[:, None], jnp.cos(ang), -jnp.sin(ang)], axis=-1)
    freq = p['hy_sin_freq']
    h = jnp.sin(freq * (feat @ p['hy_pos_w1'] + p['hy_pos_b1']))
    h = jnp.sin(freq * (h @ p['hy_pos_w2'] + p['hy_pos_b2']))
    h = h @ p['hy_pos_w3']
    h = h * jnp.exp(-t[:, None] * jnp.abs(p['hy_decay']))
    h = h.reshape(seq_len, HY_ORDER, 2, HY_WIDTH)
    h_fwd = h[:, :, 0]
    h_bwd = jnp.where((jnp.arange(seq_len) == 0)[:, None, None], 0.0, h[:, :, 1])
    norm = jnp.sum(jnp.abs(h_fwd), axis=0, keepdims=True) + jnp.sum(jnp.abs(h_bwd), axis=0, keepdims=True)
    taps = jnp.stack([h_fwd / norm, h_bwd / norm])
    return jnp.transpose(taps, (0, 2, 1, 3)).reshape(2 * HY_ORDER * seq_len, HY_WIDTH).astype(BF16)


def _hyena_tables(seq_len, mats, p):
    spec = _dft_fwd(_hyena_taps(seq_len, p), 0, 2 * HY_ORDER, seq_len, mats, None)
    row0 = (jnp.arange(seq_len) == 0)[:, None]
    scale = jnp.where(row0, 1.0, 2.0) / (2 * seq_len)
    tables = []
    for o in range(HY_ORDER):
        kr = spec[o, 0] + spec[HY_ORDER + o, 0]
        ki = spec[o, 1] + jnp.where(row0, 1.0, -1.0) * spec[HY_ORDER + o, 1]
        tables.append(jnp.stack([scale * kr, scale * jnp.where(row0, 0.0, ki), scale * jnp.where(row0, ki, kr)]))
    return tables


def _hyena_path(u, vb, row0, n_seq, seq_len, mats, tables, skip):
    y0 = _dft_fwd(vb, row0, n_seq, seq_len, mats, tables[0])
    z1, z1b = _dft_inv(y0, mats, u, 1, u, row0, skip[0:1], row0, n_seq, seq_len, False)
    y1 = _dft_fwd(z1b, 0, n_seq, seq_len, mats, tables[1])
    (ya,) = _dft_inv(y1, mats, u, 2, z1, 0, skip[1:2], row0, n_seq, seq_len, True)
    return ya


S5_LANES = S5_GROUPS * S5_STATE
S5_SEGS = 8


def _s5_kernel(n_tiles, tile, u0_ref, u1_ref, h0_ref, lam_ref, bd_ref, cd_ref, skip_ref, gw_ref, gb_ref,
               y_ref, fin_ref, bu_ref, ybwd_ref, p_ref, cm_ref, fl_ref, carry_ref, up_ref, st0_ref, st1_ref):
    seg = tile // S5_SEGS
    n = S5_LANES
    b = pl.program_id(0)
    j = pl.program_id(1)

    @pl.when((b == 0) & (j == 0))
    def _powers():
        for d in range(2):
            lr = lam_ref[2 * d:2 * d + 1, :]
            li = lam_ref[2 * d + 1:2 * d + 2, :]
            p_ref[d, 0:1, 0:n] = lr
            p_ref[d, 0:1, n:2 * n] = li

            def body(k, c, d=d, lr=lr, li=li):
                pr, pi = c
                nr = pr * lr - pi * li
                ni = pr * li + pi * lr
                p_ref[d, pl.ds(k, 1), 0:n] = nr
                p_ref[d, pl.ds(k, 1), n:2 * n] = ni
                return nr, ni

            lax.fori_loop(1, seg, body, (lr, li))

    def sweep(d, t):
        first, last = (0, n_tiles - 1) if d == 0 else (n_tiles - 1, 0)

        @pl.when(t == first)
        def _init():
            carry_ref[0:1, 0:n] = h0_ref[2 * d:2 * d + 1, :]
            carry_ref[0:1, n:2 * n] = h0_ref[2 * d + 1:2 * d + 2, :]

        def gather(i, c):
            r = pl.ds(pl.multiple_of(i * S5_SEGS, S5_SEGS), S5_SEGS)
            up_ref[r, 0:LANE] = u0_ref[pl.ds(i, S5_SEGS, stride=seg), :]
            up_ref[r, LANE:2 * LANE] = u1_ref[pl.ds(i, S5_SEGS, stride=seg), :]
            return c

        lax.fori_loop(0, seg, gather, 0)
        up = up_ref[...]
        ub = up.astype(BF16)
        for c0 in range(0, 2 * n, 512):
            bu_ref[:, c0:c0 + 512] = jnp.dot(ub, bd_ref[d, :, c0:c0 + 512], preferred_element_type=F32)

        lr8 = jnp.broadcast_to(lam_ref[2 * d:2 * d + 1, :], (S5_SEGS, n))
        li8 = jnp.broadcast_to(lam_ref[2 * d + 1:2 * d + 2, :], (S5_SEGS, n))

        def rows(i):
            k = i if d == 0 else seg - 1 - i
            return pl.ds(pl.multiple_of(k * S5_SEGS, S5_SEGS), S5_SEGS)

        def local(i, c):
            hr, hi = c
            r = rows(i)
            nr = lr8 * hr - li8 * hi + bu_ref[r, 0:n]
            ni = lr8 * hi + li8 * hr + bu_ref[r, n:2 * n]
            bu_ref[r, 0:n] = nr
            bu_ref[r, n:2 * n] = ni
            return nr, ni

        zero = jnp.zeros((S5_SEGS, n), F32)
        fr, fi = lax.fori_loop(0, seg, local, (zero, zero))
        fl_ref[:, 0:n] = fr
        fl_ref[:, n:2 * n] = fi

        plr = p_ref[d, seg - 1:seg, 0:n]
        pli = p_ref[d, seg - 1:seg, n:2 * n]
        cr = carry_ref[0:1, 0:n]
        ci = carry_ref[0:1, n:2 * n]
        for s in (range(S5_SEGS) if d == 0 else reversed(range(S5_SEGS))):
            cm_ref[s:s + 1, 0:n] = cr
            cm_ref[s:s + 1, n:2 * n] = ci
            flr = fl_ref[s:s + 1, 0:n]
            fli = fl_ref[s:s + 1, n:2 * n]
            cr, ci = plr * cr - pli * ci + flr, plr * ci + pli * cr + fli
        carry_ref[0:1, 0:n] = cr
        carry_ref[0:1, n:2 * n] = ci

        @pl.when(t == last)
        def _final():
            fin_ref[2 * d:2 * d + 1, :] = cr
            fin_ref[2 * d + 1:2 * d + 2, :] = ci

        cmr = cm_ref[:, 0:n]
        cmi = cm_ref[:, n:2 * n]

        def fix(i, c):
            r = rows(i)
            pr = p_ref[d, pl.ds(i, 1), 0:n]
            pi = p_ref[d, pl.ds(i, 1), n:2 * n]
            bu_ref[r, 0:n] = bu_ref[r, 0:n] + (pr * cmr - pi * cmi)
            bu_ref[r, n:2 * n] = bu_ref[r, n:2 * n] + (pr * cmi + pi * cmr)
            return c

        lax.fori_loop(0, seg, fix, 0)

        y = jnp.dot(bu_ref[...].astype(BF16), cd_ref[d], preferred_element_type=F32)
        trow = pl.ds(pl.multiple_of(t * tile, tile), tile)
        if d == 1:
            ybwd_ref[trow, :] = y
        else:
            tot = up * skip_ref[...] + y + ybwd_ref[trow, :]
            g = jnp.dot(tot.astype(BF16), gw_ref[...], preferred_element_type=F32) + gb_ref[...]
            res = g[:, :S5_WIDTH] * jax.nn.sigmoid(g[:, S5_WIDTH:])
            st0_ref[...] = res[:, 0:LANE]
            st1_ref[...] = res[:, LANE:2 * LANE]
            per_seg = seg // S5_SEGS

            def unpermute(m, c):
                s = m // per_seg
                i0 = (m % per_seg) * S5_SEGS
                src = pl.ds(i0 * S5_SEGS + s, S5_SEGS, stride=S5_SEGS)
                dst = pl.ds(pl.multiple_of(m * S5_SEGS, S5_SEGS), S5_SEGS)
                y_ref[dst, 0:LANE] = st0_ref[src, :]
                y_ref[dst, LANE:2 * LANE] = st1_ref[src, :]
                return c

            lax.fori_loop(0, tile // S5_SEGS, unpermute, 0)

    @pl.when(j < n_tiles)
    def _bwd():
        sweep(1, n_tiles - 1 - j)

    @pl.when(j >= n_tiles)
    def _fwd():
        sweep(0, j - n_tiles)


def _s5_branch(zm, row0, n_seq, seq_len, tile, h0, lam, bd, cd, skip, gw, gb):
    n_tiles = seq_len // tile
    blk0 = row0 // tile
    col = SPLIT_S5 // LANE
    seg = tile // S5_SEGS

    def tile_of(j):
        return jnp.where(j < n_tiles, n_tiles - 1 - j, j - n_tiles)

    return pl.pallas_call(
        functools.partial(_s5_kernel, n_tiles, tile),
        grid=(n_seq, 2 * n_tiles),
        in_specs=[
            pl.BlockSpec((tile, LANE), lambda b, j: (blk0 + b * n_tiles + tile_of(j), col)),
            pl.BlockSpec((tile, LANE), lambda b, j: (blk0 + b * n_tiles + tile_of(j), col + 1)),
            pl.BlockSpec((None, 4, S5_LANES), lambda b, j: (b, 0, 0)),
            pl.BlockSpec((4, S5_LANES), lambda b, j: (0, 0)),
            pl.BlockSpec((2, S5_WIDTH, 2 * S5_LANES), lambda b, j: (0, 0, 0)),
            pl.BlockSpec((2, 2 * S5_LANES, S5_WIDTH), lambda b, j: (0, 0, 0)),
            pl.BlockSpec((1, S5_WIDTH), lambda b, j: (0, 0)),
            pl.BlockSpec((S5_WIDTH, 2 * S5_WIDTH), lambda b, j: (0, 0)),
            pl.BlockSpec((1, 2 * S5_WIDTH), lambda b, j: (0, 0)),
        ],
        out_specs=[
            pl.BlockSpec((tile, S5_WIDTH), lambda b, j: (b * n_tiles + jnp.maximum(j - n_tiles, 0), 0)),
            pl.BlockSpec((None, 4, S5_LANES), lambda b, j: (b, 0, 0)),
        ],
        out_shape=[
            jax.ShapeDtypeStruct((n_seq * seq_len, S5_WIDTH), F32),
            jax.ShapeDtypeStruct((n_seq, 4, S5_LANES), F32),
        ],
        scratch_shapes=[
            pltpu.VMEM((tile, 2 * S5_LANES), F32),
            pltpu.VMEM((seq_len, S5_WIDTH), F32),
            pltpu.VMEM((2, seg, 2 * S5_LANES), F32),
            pltpu.VMEM((S5_SEGS, 2 * S5_LANES), F32),
            pltpu.VMEM((S5_SEGS, 2 * S5_LANES), F32),
            pltpu.VMEM((1, 2 * S5_LANES), F32),
            pltpu.VMEM((tile, S5_WIDTH), F32),
            pltpu.VMEM((tile, LANE), F32),
            pltpu.VMEM((tile, LANE), F32),
        ],
        compiler_params=_cparams("arbitrary", "arbitrary"),
        name="s5_scan",
    )(zm, zm, h0, lam, bd, cd, skip, gw, gb)


def _s5_params(lam_re, lam_im, log_step, b_re, b_im, c_re, c_im):
    lam = lax.complex(lam_re, lam_im)
    lam_bar = jnp.exp(lam * jnp.exp(log_step)[..., None])
    b_bar = ((lam_bar - 1.0) / lam)[..., None] * lax.complex(b_re, b_im)
    eye = jnp.eye(S5_GROUPS, dtype=F32)

    def in_mat(x):
        xt = jnp.swapaxes(x, 2, 3)
        return (xt[:, :, :, None, :] * eye[None, :, None, :, None]).reshape(2, S5_WIDTH, S5_LANES)

    def out_mat(x):
        xt = jnp.swapaxes(x, 2, 3)
        return (xt[:, :, :, None, :] * eye[None, :, None, :, None]).reshape(2, S5_LANES, S5_WIDTH)

    bd = jnp.concatenate([in_mat(jnp.real(b_bar)), in_mat(jnp.imag(b_bar))], axis=2).astype(BF16)
    cd = jnp.concatenate([out_mat(c_re), out_mat(-c_im)], axis=1).astype(BF16)
    lam_rows = jnp.stack([jnp.real(lam_bar[0]), jnp.imag(lam_bar[0]), jnp.real(lam_bar[1]), jnp.imag(lam_bar[1])])
    return lam_rows.reshape(4, S5_LANES), bd, cd


ROPE_HALF = HEAD_DIM // 4


def _rope_tables(seq_len):
    t = jnp.arange(seq_len)
    row = (t // GRID_W).astype(F32)
    col = (t % GRID_W).astype(F32)
    inv = ROPE_BASE ** (-jnp.arange(ROPE_HALF, dtype=F32) / ROPE_HALF)
    ang_r = row[:, None] * inv
    ang_c = col[:, None] * inv
    cos = jnp.concatenate([jnp.cos(ang_r), jnp.cos(ang_r), jnp.cos(ang_c), jnp.cos(ang_c)], axis=-1)
    sin = jnp.concatenate([-jnp.sin(ang_r), jnp.sin(ang_r), -jnp.sin(ang_c), jnp.sin(ang_c)], axis=-1)
    return jnp.tile(cos, (1, LANE // HEAD_DIM)), jnp.tile(sin, (1, LANE // HEAD_DIM))


def _rope(x, cos, sin):
    lane = lax.broadcasted_iota(jnp.int32, x.shape, 1)
    first = (lane % (2 * ROPE_HALF)) < ROPE_HALF
    partner = jnp.where(first, pltpu.roll(x, LANE - ROPE_HALF, axis=1), pltpu.roll(x, ROPE_HALF, axis=1))
    return x * cos + partner * sin


def _group_attention(qg, ks, vs, masks, sink_col):
    qb = qg.astype(BF16)
    ss = []
    m = sink_col
    for k, mk in zip(ks, masks):
        s = lax.dot_general(qb, k.astype(BF16), (((1,), (1,)), ((), ())), preferred_element_type=F32)
        s = s * (HEAD_DIM ** -0.5)
        if mk is not None:
            s = jnp.where(mk, s, NEG_INF)
        ss.append(s)
        m = jnp.maximum(m, jnp.max(s, axis=-1, keepdims=True))
    den = jnp.exp(sink_col - m)
    acc = jnp.zeros((qg.shape[0], HEAD_DIM), F32)
    for s, v in zip(ss, vs):
        p = jnp.exp(s - m)
        den = den + jnp.sum(p, axis=-1, keepdims=True)
        acc = acc + jnp.dot(p.astype(BF16), v.astype(BF16), preferred_element_type=F32)
    return acc / den


def _heads_to_rows(q, g):
    return jnp.concatenate([q[:, (Q_PER_KV * g + r) * HEAD_DIM:(Q_PER_KV * g + r + 1) * HEAD_DIM]
                            for r in range(Q_PER_KV)], axis=0)


def _sink_col(sink_ref, g, t):
    return jnp.concatenate([jnp.full((t, 1), sink_ref[Q_PER_KV * g + r], F32) for r in range(Q_PER_KV)], axis=0)


def _rows_to_heads(outs, t):
    return jnp.concatenate([outs[g][r * t:(r + 1) * t] for g in range(N_KV_HEADS) for r in range(Q_PER_KV)], axis=1)


def _ctx_attn_kernel(q_ref, k_ref, v_ref, sink_ref, o_ref):
    q, k, v = q_ref[...], k_ref[...], v_ref[...]
    t = q.shape[0]
    outs = []
    for g in range(N_KV_HEADS):
        sl = slice(g * HEAD_DIM, (g + 1) * HEAD_DIM)
        outs.append(_group_attention(_heads_to_rows(q, g), [k[:, sl]], [v[:, sl]], [None], _sink_col(sink_ref, g, t)))
    o_ref[...] = _rows_to_heads(outs, t)


def _context_attention(zm, sink):
    return pl.pallas_call(
        _ctx_attn_kernel,
        grid=(BATCH,),
        in_specs=[
            pl.BlockSpec((SEQ, ATTN_WIDTH), lambda b: (b, SPLIT_Q // ATTN_WIDTH)),
            pl.BlockSpec((SEQ, KV_WIDTH), lambda b: (b, SPLIT_K // KV_WIDTH)),
            pl.BlockSpec((SEQ, KV_WIDTH), lambda b: (b, SPLIT_V // KV_WIDTH)),
            pl.BlockSpec(memory_space=pltpu.SMEM),
        ],
        out_specs=pl.BlockSpec((SEQ, ATTN_WIDTH), lambda b: (b, 0)),
        out_shape=jax.ShapeDtypeStruct((N_PROMPT_TOK, ATTN_WIDTH), F32),
        compiler_params=_cparams("arbitrary"),
        name="ctx_attention",
    )(zm, zm, zm, sink)


def _latent_attn_kernel(q_ref, kl_ref, kc_ref, kr_ref, vl_ref, vc_ref, vr_ref, ck_ref, cv_ref, cos_ref, sin_ref,
                        sink_ref, o_ref):
    i = pl.program_id(1)
    nb = DEC_SEQ // BLOCK

    def table(ref, blk):
        return ref[pl.ds(pl.multiple_of(blk * BLOCK, BLOCK), BLOCK), :]

    left, right = jnp.maximum(i - 1, 0), jnp.minimum(i + 1, nb - 1)
    cos_q, sin_q = table(cos_ref, i), table(sin_ref, i)
    q = jnp.concatenate([_rope(q_ref[:, c:c + LANE], cos_q, sin_q) for c in range(0, ATTN_WIDTH, LANE)], axis=1)
    kw = jnp.concatenate([
        _rope(kl_ref[...], table(cos_ref, left), table(sin_ref, left)),
        _rope(kc_ref[...], cos_q, sin_q),
        _rope(kr_ref[...], table(cos_ref, right), table(sin_ref, right)),
    ], axis=0)
    vw = jnp.concatenate([vl_ref[...], vc_ref[...], vr_ref[...]], axis=0)

    m_rows = Q_PER_KV * BLOCK
    r = lax.broadcasted_iota(jnp.int32, (m_rows, 3 * BLOCK), 0) % BLOCK
    c = lax.broadcasted_iota(jnp.int32, (m_rows, 3 * BLOCK), 1)
    kpos = c + (i - 1) * BLOCK
    win = (kpos >= 0) & (kpos < DEC_SEQ) & (jnp.abs(r + BLOCK - c) <= WINDOW)

    ck, cv = ck_ref[...], cv_ref[...]
    outs = []
    for g in range(N_KV_HEADS):
        sl = slice(g * HEAD_DIM, (g + 1) * HEAD_DIM)
        outs.append(_group_attention(_heads_to_rows(q, g), [ck[:, sl], kw[:, sl]], [cv[:, sl], vw[:, sl]],
                                     [None, win], _sink_col(sink_ref, g, BLOCK)))
    o_ref[...] = _rows_to_heads(outs, BLOCK)


def _latent_attention(zm, cache_k, cache_v, layer, cos, sin, sink):
    nb = DEC_SEQ // BLOCK
    blk0 = N_PROMPT_TOK // BLOCK
    kcol, vcol = SPLIT_K // KV_WIDTH, SPLIT_V // KV_WIDTH

    def row(b, i):
        return blk0 + b * nb + i

    def win_specs(col):
        return [
            pl.BlockSpec((BLOCK, KV_WIDTH), lambda b, i: (row(b, jnp.maximum(i - 1, 0)), col)),
            pl.BlockSpec((BLOCK, KV_WIDTH), lambda b, i: (row(b, i), col)),
            pl.BlockSpec((BLOCK, KV_WIDTH), lambda b, i: (row(b, jnp.minimum(i + 1, nb - 1)), col)),
        ]

    ctx_spec = pl.BlockSpec((None, None, PAST_LEN, KV_WIDTH), lambda b, i: (b, layer, 0, 0))
    tab_spec = pl.BlockSpec((DEC_SEQ, LANE), lambda b, i: (0, 0))
    return pl.pallas_call(
        _latent_attn_kernel,
        grid=(DEC_BATCH, nb),
        in_specs=[pl.BlockSpec((BLOCK, ATTN_WIDTH), lambda b, i: (row(b, i), SPLIT_Q // ATTN_WIDTH))]
        + win_specs(kcol) + win_specs(vcol)
        + [ctx_spec, ctx_spec, tab_spec, tab_spec, pl.BlockSpec(memory_space=pltpu.SMEM)],
        out_specs=pl.BlockSpec((BLOCK, ATTN_WIDTH), lambda b, i: (b * nb + i, 0)),
        out_shape=jax.ShapeDtypeStruct((N_SAMPLE_TOK, ATTN_WIDTH), F32),
        compiler_params=_cparams("arbitrary", "arbitrary"),
        name="latent_attention",
    )(zm, zm, zm, zm, zm, zm, zm, cache_k, cache_v, cos, sin, sink)


def kernel(x_prompt, x_sample, cache_k, cache_v, state_s5_re, state_s5_im, c, c_ctx, ada_w, ada_b, norm1_g, w_in, hy_conv_w, hy_conv_b, hy_pos_w1, hy_pos_b1, hy_pos_w2, hy_pos_b2, hy_pos_w3, hy_sin_freq, hy_decay, hy_skip, s5_lam_re, s5_lam_im, s5_log_step, s5_b_re, s5_b_im, s5_c_re, s5_c_im, s5_skip, s5_glu_w, s5_glu_b, attn_sink, proj_a, proj_b, proj_c, w_out, norm2_g, ffn_w_gate, ffn_w_up, ffn_w_down, final_norm_g):
    cvec = jnp.concatenate([c_ctx[None], c, jnp.zeros((MOD_ROWS - 1 - DEC_BATCH, D_MODEL), F32)], axis=0)
    mod = _modulation(cvec, ada_w, ada_b).reshape(DEPTH * MOD_ROWS * 6, 1, D_MODEL)

    x = jnp.concatenate([x_prompt.reshape(N_PROMPT_TOK, D_MODEL), x_sample.reshape(N_SAMPLE_TOK, D_MODEL)], axis=0)
    w_main = w_in[:, :, :MAIN_IN].astype(BF16)
    w_gate = w_in[:, :, MAIN_IN:].astype(BF16)
    pa, pb, pc, wo = proj_a.astype(BF16), proj_b.astype(BF16), proj_c.astype(BF16), w_out.astype(BF16)
    wg, wu, wd = ffn_w_gate.astype(BF16), ffn_w_up.astype(BF16), ffn_w_down.astype(BF16)

    ctx_k = cache_k.reshape(DEC_BATCH, DEPTH, PAST_LEN, KV_WIDTH)
    ctx_v = cache_v.reshape(DEC_BATCH, DEPTH, PAST_LEN, KV_WIDTH)
    rope_cos, rope_sin = _rope_tables(DEC_SEQ)
    mats_p, mats_s = _dft_mats(SEQ), _dft_mats(DEC_SEQ)

    ks_out, vs_out, sre_out, sim_out = [], [], [], []
    for l in range(DEPTH):
        p = {
            'hy_pos_w1': hy_pos_w1[l], 'hy_pos_b1': hy_pos_b1[l],
            'hy_pos_w2': hy_pos_w2[l], 'hy_pos_b2': hy_pos_b2[l], 'hy_pos_w3': hy_pos_w3[l],
            'hy_sin_freq': hy_sin_freq[l], 'hy_decay': hy_decay[l],
        }
        zm, sg = _in_proj(x, mod, l, norm1_g[l].reshape(1, D_MODEL), w_main[l], w_gate[l])
        u, vb = _short_conv(zm, hy_conv_w[l], hy_conv_b[l].reshape(1, HY_IN))
        ya_p = _hyena_path(u, vb, 0, BATCH, SEQ, mats_p, _hyena_tables(SEQ, mats_p, p), hy_skip[l])
        ya_s = _hyena_path(u, vb, N_PROMPT_TOK, DEC_BATCH, DEC_SEQ, mats_s, _hyena_tables(DEC_SEQ, mats_s, p),
                           hy_skip[l])
        k_l = zm[:N_PROMPT_TOK, SPLIT_K:SPLIT_V].reshape(BATCH, SEQ, N_KV_HEADS, HEAD_DIM)
        v_l = zm[:N_PROMPT_TOK, SPLIT_V:].reshape(BATCH, SEQ, N_KV_HEADS, HEAD_DIM)
        yc_p = _context_attention(zm, attn_sink[l])
        yc_s = _latent_attention(zm, ctx_k, ctx_v, l, rope_cos, rope_sin, attn_sink[l])

        lam, bd, cd = _s5_params(s5_lam_re[l], s5_lam_im[l], s5_log_step[l], s5_b_re[l], s5_b_im[l],
                                 s5_c_re[l], s5_c_im[l])
        s5_w = (lam, bd, cd, s5_skip[l].reshape(1, S5_WIDTH), s5_glu_w[l].astype(BF16),
                s5_glu_b[l].reshape(1, 2 * S5_WIDTH))
        h0_s = jnp.stack([state_s5_re[:, l, 0], state_s5_im[:, l, 0], state_s5_re[:, l, 1], state_s5_im[:, l, 1]],
                         axis=1).reshape(DEC_BATCH, 4, S5_LANES)
        yb_p, fin_p = _s5_branch(zm, 0, BATCH, SEQ, SEQ, jnp.zeros((BATCH, 4, S5_LANES), F32), *s5_w)
        yb_s, _ = _s5_branch(zm, N_PROMPT_TOK, DEC_BATCH, DEC_SEQ, 512, h0_s, *s5_w)
        fin_p = fin_p.reshape(BATCH, 2, 2, S5_GROUPS, S5_STATE)

        ks_out.append(k_l)
        vs_out.append(v_l)
        sre_out.append(fin_p[:, :, 0])
        sim_out.append(fin_p[:, :, 1])
        ya = jnp.concatenate([ya_p, ya_s], axis=0)
        yb = jnp.concatenate([yb_p, yb_s], axis=0)
        yc = jnp.concatenate([yc_p, yc_s], axis=0)
        x = _merge(ya, yb, yc, sg, x, mod, l, pa[l], pb[l], pc[l], wo[l])
        x = _ffn(x, mod, l, norm2_g[l].reshape(1, D_MODEL), wg[l], wu[l], wd[l])

    y = _final_norm(x, final_norm_g.reshape(1, D_MODEL))
    y_prompt = y[:N_PROMPT_TOK].reshape(BATCH, SEQ, D_MODEL)
    y_sample = y[N_PROMPT_TOK:].reshape(DEC_BATCH, DEC_SEQ, D_MODEL)
    return (y_prompt, y_sample, jnp.stack(ks_out, axis=1), jnp.stack(vs_out, axis=1),
            jnp.stack(sre_out, axis=1), jnp.stack(sim_out, axis=1))
```

```python
import functools
import math

import jax
import jax.numpy as jnp
import numpy as np
from jax import lax
from jax.experimental import pallas as pl
from jax.experimental.pallas import tpu as pltpu

D_MODEL = 1024
BATCH = 16
SEQ = 256
DEPTH = 4
DEC_BATCH = 2
DEC_SEQ = 4096
PAST_LEN = 512
GRID_W = 64
N_BRANCH = 3
HY_WIDTH = 256
HY_ORDER = 2
HY_SHORT_K = 3
HY_POS_EMB = 33
HY_POS_BANDS = (HY_POS_EMB - 1) // 2
HY_FILTER_HIDDEN = 64
S5_WIDTH = 256
S5_GROUP = 16
S5_GROUPS = S5_WIDTH // S5_GROUP
S5_STATE = 64
N_HEADS = 8
N_KV_HEADS = 2
Q_PER_KV = N_HEADS // N_KV_HEADS
HEAD_DIM = 64
ATTN_WIDTH = N_HEADS * HEAD_DIM
KV_WIDTH = N_KV_HEADS * HEAD_DIM
WINDOW = 128
BLOCK = 128
ROPE_BASE = 10000.0
FFN_HIDDEN = ((8 * D_MODEL // 3 + 255) // 256) * 256
HY_IN = (HY_ORDER + 1) * HY_WIDTH
GATE_IN = N_BRANCH * D_MODEL
MAIN_IN = HY_IN + S5_WIDTH + ATTN_WIDTH + 2 * KV_WIDTH
SPLIT_S5 = HY_IN
SPLIT_Q = SPLIT_S5 + S5_WIDTH
SPLIT_K = SPLIT_Q + ATTN_WIDTH
SPLIT_V = SPLIT_K + KV_WIDTH

F32 = jnp.float32
BF16 = jnp.bfloat16
EPS = 1e-6
NEG_INF = -1e30

N_PROMPT_TOK = BATCH * SEQ
N_SAMPLE_TOK = DEC_BATCH * DEC_SEQ
N_TOK = N_PROMPT_TOK + N_SAMPLE_TOK
LANE = 128
MOD_ROWS = 8
TM = 256
VMEM_LIMIT = 56 * 1024 * 1024


def _cparams(*sem):
    return pltpu.CompilerParams(dimension_semantics=sem, vmem_limit_bytes=VMEM_LIMIT)


def _mod_row(i):
    n_p = N_PROMPT_TOK // TM
    per_b = DEC_SEQ // TM
    return jnp.where(i < n_p, 0, 1 + (i - n_p) // per_b)


def _mod_spec(layer, k):
    return pl.BlockSpec((None, 1, D_MODEL), lambda i: ((layer * MOD_ROWS + _mod_row(i)) * 6 + k, 0, 0))


def _resident(shape):
    return pl.BlockSpec(shape, lambda i: (0,) * len(shape), pipeline_mode=pl.Buffered(1))


def _rms(x, g):
    return x * lax.rsqrt(jnp.mean(x * x, axis=-1, keepdims=True) + EPS) * g


def _mod_kernel(c_ref, w_ref, b_ref, o_ref):
    c = c_ref[...]
    s = (c * jax.nn.sigmoid(c)).astype(BF16)
    o_ref[...] = jnp.dot(s, w_ref[...].astype(BF16), preferred_element_type=F32) + b_ref[...]


def _modulation(cvec, ada_w, ada_b):
    tn = 1536
    return pl.pallas_call(
        _mod_kernel,
        grid=(DEPTH, 6 * D_MODEL // tn),
        in_specs=[
            pl.BlockSpec((MOD_ROWS, D_MODEL), lambda l, j: (0, 0)),
            pl.BlockSpec((None, D_MODEL, tn), lambda l, j: (l, 0, j)),
            pl.BlockSpec((None, 1, tn), lambda l, j: (l, 0, j)),
        ],
        out_specs=pl.BlockSpec((None, MOD_ROWS, tn), lambda l, j: (l, 0, j)),
        out_shape=jax.ShapeDtypeStruct((DEPTH, MOD_ROWS, 6 * D_MODEL), F32),
        compiler_params=_cparams("arbitrary", "arbitrary"),
        name="adaln_mod",
    )(cvec, ada_w, ada_b.reshape(DEPTH, 1, 6 * D_MODEL))


def _inproj_kernel(x_ref, sh_ref, sc_ref, g_ref, wm_ref, zm_ref):
    h = (_rms(x_ref[...], g_ref[...]) * (1.0 + sc_ref[...]) + sh_ref[...]).astype(BF16)
    zm_ref[...] = jnp.dot(h, wm_ref[...], preferred_element_type=F32)


def _in_proj(x, mod, layer, g, w_main):
    return pl.pallas_call(
        _inproj_kernel,
        grid=(N_TOK // TM,),
        in_specs=[
            pl.BlockSpec((TM, D_MODEL), lambda i: (i, 0)),
            _mod_spec(layer, 0),
            _mod_spec(layer, 1),
            _resident((1, D_MODEL)),
            _resident((D_MODEL, MAIN_IN)),
        ],
        out_specs=pl.BlockSpec((TM, MAIN_IN), lambda i: (i, 0)),
        out_shape=jax.ShapeDtypeStruct((N_TOK, MAIN_IN), F32),
        compiler_params=_cparams("arbitrary"),
        name="in_proj",
    )(x, mod, mod, g, w_main)


N_PROMPT_TILES = N_PROMPT_TOK // TM


def _path_specs(width):
    return [pl.BlockSpec((TM, width), lambda i: (jnp.minimum(i, N_PROMPT_TILES - 1), 0)),
            pl.BlockSpec((TM, width), lambda i: (jnp.maximum(i - N_PROMPT_TILES, 0), 0))]


def _merge_kernel(yap_ref, yas_ref, ybp_ref, ybs_ref, ycp_ref, ycs_ref, x_ref, sh_ref, sc_ref, g1_ref, g_ref,
                  wg_ref, pa_ref, pb_ref, pc_ref, wo_ref, o_ref):
    is_prompt = pl.program_id(0) < N_PROMPT_TILES
    x = x_ref[...]
    h = (_rms(x, g_ref[...]) * (1.0 + sc_ref[...]) + sh_ref[...]).astype(BF16)
    branches = ((yap_ref, yas_ref, pa_ref), (ybp_ref, ybs_ref, pb_ref), (ycp_ref, ycs_ref, pc_ref))
    m = jnp.zeros((TM, D_MODEL), F32)
    for k, (p_ref, s_ref, w_ref) in enumerate(branches):
        y = jnp.where(is_prompt, p_ref[...], s_ref[...]).astype(BF16)
        gate = jax.nn.sigmoid(jnp.dot(h, wg_ref[:, k * D_MODEL:(k + 1) * D_MODEL], preferred_element_type=F32))
        m = m + gate * jnp.dot(y, w_ref[...], preferred_element_type=F32)
    o_ref[...] = x + g1_ref[...] * jnp.dot(m.astype(BF16), wo_ref[...], preferred_element_type=F32)


def _merge(ya, yb, yc, x, mod, layer, g, w_gate, pa, pb, pc, wo):
    return pl.pallas_call(
        _merge_kernel,
        grid=(N_TOK // TM,),
        in_specs=_path_specs(HY_WIDTH) + _path_specs(S5_WIDTH) + _path_specs(ATTN_WIDTH) + [
            pl.BlockSpec((TM, D_MODEL), lambda i: (i, 0)),
            _mod_spec(layer, 0),
            _mod_spec(layer, 1),
            _mod_spec(layer, 2),
            _resident((1, D_MODEL)),
            _resident((D_MODEL, GATE_IN)),
            _resident((HY_WIDTH, D_MODEL)),
            _resident((S5_WIDTH, D_MODEL)),
            _resident((ATTN_WIDTH, D_MODEL)),
            _resident((D_MODEL, D_MODEL)),
        ],
        out_specs=pl.BlockSpec((TM, D_MODEL), lambda i: (i, 0)),
        out_shape=jax.ShapeDtypeStruct((N_TOK, D_MODEL), F32),
        compiler_params=_cparams("arbitrary"),
        name="merge_out",
    )(*ya, *yb, *yc, x, mod, mod, mod, g, w_gate, pa, pb, pc, wo)


def _ffn_kernel(x_ref, sh_ref, sc_ref, gt_ref, g_ref, wg_ref, wu_ref, wd_ref, o_ref):
    x = x_ref[...]
    h = (_rms(x, g_ref[...]) * (1.0 + sc_ref[...]) + sh_ref[...]).astype(BF16)
    a = jnp.dot(h, wg_ref[...], preferred_element_type=F32)
    b = jnp.dot(h, wu_ref[...], preferred_element_type=F32)
    f = (a * jax.nn.sigmoid(a) * b).astype(BF16)
    o_ref[...] = x + gt_ref[...] * jnp.dot(f, wd_ref[...], preferred_element_type=F32)


def _ffn(x, mod, layer, g, wg, wu, wd):
    return pl.pallas_call(
        _ffn_kernel,
        grid=(N_TOK // TM,),
        in_specs=[
            pl.BlockSpec((TM, D_MODEL), lambda i: (i, 0)),
            _mod_spec(layer, 3),
            _mod_spec(layer, 4),
            _mod_spec(layer, 5),
            _resident((1, D_MODEL)),
            _resident((D_MODEL, FFN_HIDDEN)),
            _resident((D_MODEL, FFN_HIDDEN)),
            _resident((FFN_HIDDEN, D_MODEL)),
        ],
        out_specs=pl.BlockSpec((TM, D_MODEL), lambda i: (i, 0)),
        out_shape=jax.ShapeDtypeStruct((N_TOK, D_MODEL), F32),
        compiler_params=_cparams("arbitrary"),
        name="ffn",
    )(x, mod, mod, mod, g, wg, wu, wd)


def _final_norm_kernel(x_ref, g_ref, op_ref, os_ref):
    y = _rms(x_ref[...], g_ref[...])
    is_prompt = pl.program_id(0) < N_PROMPT_TILES

    @pl.when(is_prompt)
    def _():
        op_ref[...] = y

    @pl.when(jnp.logical_not(is_prompt))
    def _():
        os_ref[...] = y


def _final_norm(x, g):
    return pl.pallas_call(
        _final_norm_kernel,
        grid=(N_TOK // TM,),
        in_specs=[pl.BlockSpec((TM, D_MODEL), lambda i: (i, 0)), _resident((1, D_MODEL))],
        out_specs=_path_specs(D_MODEL),
        out_shape=[jax.ShapeDtypeStruct((N_PROMPT_TOK, D_MODEL), F32),
                   jax.ShapeDtypeStruct((N_SAMPLE_TOK, D_MODEL), F32)],
        compiler_params=_cparams("arbitrary"),
        name="final_norm",
    )(x, g)


SUBLANE = 8
HY_TILE = 256
HY_BLOCK = 512
HY_ROWS = 64


def _dft_mats(blk):
    n = 2 * blk
    idx = jnp.arange(blk, dtype=jnp.int32)
    ang = (2.0 * math.pi / n) * ((idx[:, None] * idx[None, :]) % n).astype(F32)
    alt = jnp.where(idx % 2 == 0, 1.0, -1.0).astype(F32)
    return jnp.stack([jnp.cos(ang), jnp.where((idx == 0)[:, None], alt[None, :], -jnp.sin(ang))]).astype(BF16)


def _short_conv_kernel(x_ref, xp_ref, xn_ref, w_ref, b_ref, u_ref, vb_ref):
    i = pl.program_id(0)
    n_p = N_PROMPT_TOK // HY_TILE
    per_p, per_s = SEQ // HY_TILE, DEC_SEQ // HY_TILE
    idx = jnp.where(i < n_p, i % per_p, (i - n_p) % per_s)
    per = jnp.where(i < n_p, per_p, per_s)
    has_prev = (idx > 0).astype(F32)
    has_next = (idx < per - 1).astype(F32)
    x = x_ref[...]
    row = lax.broadcasted_iota(jnp.int32, (HY_TILE, 1), 0)
    x_prev = jnp.where(row == 0, xp_ref[SUBLANE - 1:SUBLANE, :] * has_prev, pltpu.roll(x, 1, axis=0))
    x_next = jnp.where(row == HY_TILE - 1, xn_ref[0:1, :] * has_next, pltpu.roll(x, HY_TILE - 1, axis=0))
    u = b_ref[...] + x_prev * w_ref[0:1, :] + x * w_ref[1:2, :] + x_next * w_ref[2:3, :]
    u_ref[...] = u
    vb_ref[...] = u[:, :HY_WIDTH].astype(BF16)


def _short_conv(zm, w, b):
    per8 = HY_TILE // SUBLANE
    last8 = N_TOK // SUBLANE - 1
    return pl.pallas_call(
        _short_conv_kernel,
        grid=(N_TOK // HY_TILE,),
        in_specs=[
            pl.BlockSpec((HY_TILE, HY_IN), lambda i: (i, 0)),
            pl.BlockSpec((SUBLANE, HY_IN), lambda i: (jnp.maximum(i * per8 - 1, 0), 0)),
            pl.BlockSpec((SUBLANE, HY_IN), lambda i: (jnp.minimum((i + 1) * per8, last8), 0)),
            pl.BlockSpec((HY_SHORT_K, HY_IN), lambda i: (0, 0)),
            pl.BlockSpec((1, HY_IN), lambda i: (0, 0)),
        ],
        out_specs=[
            pl.BlockSpec((HY_TILE, HY_IN), lambda i: (i, 0)),
            pl.BlockSpec((HY_TILE, HY_WIDTH), lambda i: (i, 0)),
        ],
        out_shape=[
            jax.ShapeDtypeStruct((N_TOK, HY_IN), F32),
            jax.ShapeDtypeStruct((N_TOK, HY_WIDTH), BF16),
        ],
        compiler_params=_cparams("arbitrary"),
        name="hyena_short_conv",
    )(zm, zm, zm, w, b)


def _dft_fwd_kernel(z_ref, f_ref, o_ref):
    z = z_ref[...]
    o_ref[0] = jnp.dot(f_ref[0], z, preferred_element_type=F32)
    o_ref[1] = jnp.dot(f_ref[1], z, preferred_element_type=F32)


def _dft_fwd(z, row0, n_blk, mats):
    blk = mats.shape[-1]
    return pl.pallas_call(
        _dft_fwd_kernel,
        grid=(n_blk,),
        in_specs=[
            pl.BlockSpec((blk, HY_WIDTH), lambda b: (row0 // blk + b, 0)),
            pl.BlockSpec((2, blk, blk), lambda b: (0, 0, 0)),
        ],
        out_specs=pl.BlockSpec((None, 2, blk, HY_WIDTH), lambda b: (b, 0, 0, 0)),
        out_shape=jax.ShapeDtypeStruct((n_blk, 2, blk, HY_WIDTH), F32),
        compiler_params=_cparams("arbitrary"),
        name="hyena_dft_fwd",
    )(z, mats)


def _block_conv_kernel(n_blk, last, z_ref, k_ref, f_ref, g_ref, zp_ref, skip_ref, o_ref, *rest):
    s_ref = rest[-1]
    n = pl.program_id(1)
    blk = f_ref.shape[-1]

    def product(r, first_row_packed):
        acc_r = acc_i = jnp.zeros((r.size, HY_WIDTH), F32)
        fix_r = fix_i = acc_r
        for i in range(n_blk):
            d = n - i + n_blk - 1
            zr, zi = z_ref[i, 0, r, :], z_ref[i, 1, r, :]
            kr, ki = k_ref[d, 0, r, :], k_ref[d, 1, r, :]
            acc_r = acc_r + (zr * kr - zi * ki)
            acc_i = acc_i + (zr * ki + zi * kr)
            if first_row_packed:
                fix_r = fix_r + zr * kr
                fix_i = fix_i + zi * ki
        if first_row_packed:
            row0 = lax.broadcasted_iota(jnp.int32, (r.size, 1), 0) == 0
            acc_r = jnp.where(row0, fix_r, acc_r)
            acc_i = jnp.where(row0, fix_i, acc_i)
        s_ref[0, r, :] = acc_r.astype(BF16)
        s_ref[1, r, :] = acc_i.astype(BF16)

    def chunk(c, carry):
        product(pl.ds(pl.multiple_of(c * HY_ROWS, HY_ROWS), HY_ROWS), False)
        return carry

    lax.fori_loop(0, blk // HY_ROWS, chunk, 0)
    product(pl.ds(0, 2 * SUBLANE), True)

    conv_c = jnp.dot(f_ref[0], s_ref[0], preferred_element_type=F32)
    conv_s = jnp.dot(f_ref[1], s_ref[1], preferred_element_type=F32)
    t = lax.broadcasted_iota(jnp.int32, (blk, 1), 0)
    alt = jnp.where(t % 2 == 0, 1.0, -1.0).astype(F32)
    conv = conv_c + jnp.where(t == 0, 0.0, conv_s) + alt * s_ref[1, 0:1, :].astype(F32)
    z = g_ref[...] * (conv + skip_ref[...] * zp_ref[...])
    o_ref[...] = z
    if not last:
        rest[0][...] = z.astype(BF16)


def _block_conv(zspec, tables, mats, u, gate_col, z_prev, z_prev_row0, skip, row0, n_seq, last):
    blk = mats.shape[-1]
    n_blk = zspec.shape[0] // n_seq

    def rows(base):
        return lambda b, n: (base // blk + b * n_blk + n, 0)

    out_spec = pl.BlockSpec((blk, HY_WIDTH), rows(0))
    out_shape = [jax.ShapeDtypeStruct((n_seq * n_blk * blk, HY_WIDTH), F32)]
    if not last:
        out_shape.append(jax.ShapeDtypeStruct((n_seq * n_blk * blk, HY_WIDTH), BF16))
    return pl.pallas_call(
        functools.partial(_block_conv_kernel, n_blk, last),
        grid=(n_seq, n_blk),
        in_specs=[
            pl.BlockSpec((n_blk, 2, blk, HY_WIDTH), lambda b, n: (b, 0, 0, 0)),
            pl.BlockSpec(tables.shape, lambda b, n: (0, 0, 0, 0), pipeline_mode=pl.Buffered(1)),
            pl.BlockSpec((2, blk, blk), lambda b, n: (0, 0, 0)),
            pl.BlockSpec((blk, HY_WIDTH), lambda b, n: (row0 // blk + b * n_blk + n, gate_col)),
            pl.BlockSpec((blk, HY_WIDTH), rows(z_prev_row0)),
            pl.BlockSpec((1, HY_WIDTH), lambda b, n: (0, 0)),
        ],
        out_specs=[out_spec] * len(out_shape),
        out_shape=out_shape,
        scratch_shapes=[pltpu.VMEM((2, blk, HY_WIDTH), BF16)],
        compiler_params=_cparams("arbitrary", "arbitrary"),
        name="hyena_block_conv",
    )(zspec, tables, mats, u, z_prev, skip)


def _hyena_taps(seq_len, p):
    t = jnp.linspace(0.0, 1.0, seq_len, dtype=F32)
    w = 2.0 * math.pi * jnp.arange(seq_len, dtype=F32) / seq_len
    bands = jnp.linspace(1e-4, HY_POS_BANDS - 1, HY_POS_BANDS, dtype=F32)
    ang = w[:, None] * bands[None, :]
    feat = jnp.concatenate([t[:, None], jnp.cos(ang), -jnp.sin(ang)], axis=-1)
    freq = p['hy_sin_freq']
    h = jnp.sin(freq * (feat @ p['hy_pos_w1'] + p['hy_pos_b1']))
    h = jnp.sin(freq * (h @ p['hy_pos_w2'] + p['hy_pos_b2']))
    h = h @ p['hy_pos_w3']
    h = h * jnp.exp(-t[:, None] * jnp.abs(p['hy_decay']))
    h = h.reshape(seq_len, HY_ORDER, 2, HY_WIDTH)
    k_lag = jnp.concatenate([jnp.zeros((1, HY_ORDER, HY_WIDTH), F32), jnp.flip(h[1:, :, 1], axis=0), h[:, :, 0]], axis=0)
    k_lag = k_lag / jnp.sum(jnp.abs(k_lag), axis=0, keepdims=True)
    return jnp.swapaxes(k_lag, 0, 1).reshape(HY_ORDER * 2 * seq_len, HY_WIDTH).astype(BF16)


def _hyena_tables(seq_len, mats, p):
    blk = mats.shape[-1]
    n_lag = 2 * seq_len // blk
    spec = _dft_fwd(_hyena_taps(seq_len, p), 0, HY_ORDER * n_lag, mats).reshape(HY_ORDER, n_lag, 2, blk, HY_WIDTH)
    f = jnp.arange(blk)[:, None]
    alt = jnp.where(f % 2 == 0, 1.0, -1.0)
    scale = jnp.where(f == 0, 1.0, 2.0) / (2 * blk)
    return (spec[:, 1:] + alt * spec[:, :-1]) * scale


def _hyena_path(u, vb, row0, n_seq, seq_len, mats, tables, skip):
    n_blk = n_seq * seq_len // mats.shape[-1]
    z1, z1b = _block_conv(_dft_fwd(vb, row0, n_blk, mats), tables[0], mats, u, 1, u, row0, skip[0:1], row0, n_seq,
                          False)
    (ya,) = _block_conv(_dft_fwd(z1b, 0, n_blk, mats), tables[1], mats, u, 2, z1, 0, skip[1:2], row0, n_seq, True)
    return ya


S5_LANES = S5_GROUPS * S5_STATE
S5_SEGS = 8


def _s5_kernel(n_tiles, tile, u0_ref, u1_ref, h0_ref, lam_ref, bd_ref, cd_ref, skip_ref, gw_ref, gb_ref,
               y_ref, fin_ref, bu_ref, ybwd_ref, p_ref, cm_ref, fl_ref, carry_ref, up_ref, st0_ref, st1_ref):
    seg = tile // S5_SEGS
    n = S5_LANES
    b = pl.program_id(0)
    j = pl.program_id(1)

    @pl.when((b == 0) & (j == 0))
    def _powers():
        for d in range(2):
            lr = lam_ref[2 * d:2 * d + 1, :]
            li = lam_ref[2 * d + 1:2 * d + 2, :]
            p_ref[d, 0:1, 0:n] = lr
            p_ref[d, 0:1, n:2 * n] = li

            def body(k, c, d=d, lr=lr, li=li):
                pr, pi = c
                nr = pr * lr - pi * li
                ni = pr * li + pi * lr
                p_ref[d, pl.ds(k, 1), 0:n] = nr
                p_ref[d, pl.ds(k, 1), n:2 * n] = ni
                return nr, ni

            lax.fori_loop(1, seg, body, (lr, li))

    def sweep(d, t):
        first, last = (0, n_tiles - 1) if d == 0 else (n_tiles - 1, 0)

        @pl.when(t == first)
        def _init():
            carry_ref[0:1, 0:n] = h0_ref[2 * d:2 * d + 1, :]
            carry_ref[0:1, n:2 * n] = h0_ref[2 * d + 1:2 * d + 2, :]

        def gather(i, c):
            r = pl.ds(pl.multiple_of(i * S5_SEGS, S5_SEGS), S5_SEGS)
            up_ref[r, 0:LANE] = u0_ref[pl.ds(i, S5_SEGS, stride=seg), :]
            up_ref[r, LANE:2 * LANE] = u1_ref[pl.ds(i, S5_SEGS, stride=seg), :]
            return c

        lax.fori_loop(0, seg, gather, 0)
        up = up_ref[...]
        ub = up.astype(BF16)
        for c0 in range(0, 2 * n, 512):
            bu_ref[:, c0:c0 + 512] = jnp.dot(ub, bd_ref[d, :, c0:c0 + 512], preferred_element_type=F32)

        lr8 = jnp.broadcast_to(lam_ref[2 * d:2 * d + 1, :], (S5_SEGS, n))
        li8 = jnp.broadcast_to(lam_ref[2 * d + 1:2 * d + 2, :], (S5_SEGS, n))

        def rows(i):
            k = i if d == 0 else seg - 1 - i
            return pl.ds(pl.multiple_of(k * S5_SEGS, S5_SEGS), S5_SEGS)

        def local(i, c):
            hr, hi = c
            r = rows(i)
            nr = lr8 * hr - li8 * hi + bu_ref[r, 0:n]
            ni = lr8 * hi + li8 * hr + bu_ref[r, n:2 * n]
            bu_ref[r, 0:n] = nr
            bu_ref[r, n:2 * n] = ni
            return nr, ni

        zero = jnp.zeros((S5_SEGS, n), F32)
        fr, fi = lax.fori_loop(0, seg, local, (zero, zero))
        fl_ref[:, 0:n] = fr
        fl_ref[:, n:2 * n] = fi

        plr = p_ref[d, seg - 1:seg, 0:n]
        pli = p_ref[d, seg - 1:seg, n:2 * n]
        cr = carry_ref[0:1, 0:n]
        ci = carry_ref[0:1, n:2 * n]
        for s in (range(S5_SEGS) if d == 0 else reversed(range(S5_SEGS))):
            cm_ref[s:s + 1, 0:n] = cr
            cm_ref[s:s + 1, n:2 * n] = ci
            flr = fl_ref[s:s + 1, 0:n]
            fli = fl_ref[s:s + 1, n:2 * n]
            cr, ci = plr * cr - pli * ci + flr, plr * ci + pli * cr + fli
        carry_ref[0:1, 0:n] = cr
        carry_ref[0:1, n:2 * n] = ci

        @pl.when(t == last)
        def _final():
            fin_ref[2 * d:2 * d + 1, :] = cr
            fin_ref[2 * d + 1:2 * d + 2, :] = ci

        cmr = cm_ref[:, 0:n]
        cmi = cm_ref[:, n:2 * n]

        def fix(i, c):
            r = rows(i)
            pr = p_ref[d, pl.ds(i, 1), 0:n]
            pi = p_ref[d, pl.ds(i, 1), n:2 * n]
            bu_ref[r, 0:n] = bu_ref[r, 0:n] + (pr * cmr - pi * cmi)
            bu_ref[r, n:2 * n] = bu_ref[r, n:2 * n] + (pr * cmi + pi * cmr)
            return c

        lax.fori_loop(0, seg, fix, 0)

        y = jnp.dot(bu_ref[...].astype(BF16), cd_ref[d], preferred_element_type=F32)
        trow = pl.ds(pl.multiple_of(t * tile, tile), tile)
        if d == 1:
            ybwd_ref[trow, :] = y
        else:
            tot = up * skip_ref[...] + y + ybwd_ref[trow, :]
            g = jnp.dot(tot.astype(BF16), gw_ref[...], preferred_element_type=F32) + gb_ref[...]
            res = g[:, :S5_WIDTH] * jax.nn.sigmoid(g[:, S5_WIDTH:])
            st0_ref[...] = res[:, 0:LANE]
            st1_ref[...] = res[:, LANE:2 * LANE]
            per_seg = seg // S5_SEGS

            def unpermute(m, c):
                s = m // per_seg
                i0 = (m % per_seg) * S5_SEGS
                src = pl.ds(i0 * S5_SEGS + s, S5_SEGS, stride=S5_SEGS)
                dst = pl.ds(pl.multiple_of(m * S5_SEGS, S5_SEGS), S5_SEGS)
                y_ref[dst, 0:LANE] = st0_ref[src, :]
                y_ref[dst, LANE:2 * LANE] = st1_ref[src, :]
                return c

            lax.fori_loop(0, tile // S5_SEGS, unpermute, 0)

    @pl.when(j < n_tiles)
    def _bwd():
        sweep(1, n_tiles - 1 - j)

    @pl.when(j >= n_tiles)
    def _fwd():
        sweep(0, j - n_tiles)


def _s5_branch(zm, row0, n_seq, seq_len, tile, h0, lam, bd, cd, skip, gw, gb):
    n_tiles = seq_len // tile
    blk0 = row0 // tile
    col = SPLIT_S5 // LANE
    seg = tile // S5_SEGS

    def tile_of(j):
        return jnp.where(j < n_tiles, n_tiles - 1 - j, j - n_tiles)

    return pl.pallas_call(
        functools.partial(_s5_kernel, n_tiles, tile),
        grid=(n_seq, 2 * n_tiles),
        in_specs=[
            pl.BlockSpec((tile, LANE), lambda b, j: (blk0 + b * n_tiles + tile_of(j), col)),
            pl.BlockSpec((tile, LANE), lambda b, j: (blk0 + b * n_tiles + tile_of(j), col + 1)),
            pl.BlockSpec((None, 4, S5_LANES), lambda b, j: (b, 0, 0)),
            pl.BlockSpec((4, S5_LANES), lambda b, j: (0, 0)),
            pl.BlockSpec((2, S5_WIDTH, 2 * S5_LANES), lambda b, j: (0, 0, 0)),
            pl.BlockSpec((2, 2 * S5_LANES, S5_WIDTH), lambda b, j: (0, 0, 0)),
            pl.BlockSpec((1, S5_WIDTH), lambda b, j: (0, 0)),
            pl.BlockSpec((S5_WIDTH, 2 * S5_WIDTH), lambda b, j: (0, 0)),
            pl.BlockSpec((1, 2 * S5_WIDTH), lambda b, j: (0, 0)),
        ],
        out_specs=[
            pl.BlockSpec((tile, S5_WIDTH), lambda b, j: (b * n_tiles + jnp.maximum(j - n_tiles, 0), 0)),
            pl.BlockSpec((None, 4, S5_LANES), lambda b, j: (b, 0, 0)),
        ],
        out_shape=[
            jax.ShapeDtypeStruct((n_seq * seq_len, S5_WIDTH), F32),
            jax.ShapeDtypeStruct((n_seq, 4, S5_LANES), F32),
        ],
        scratch_shapes=[
            pltpu.VMEM((tile, 2 * S5_LANES), F32),
            pltpu.VMEM((seq_len, S5_WIDTH), F32),
            pltpu.VMEM((2, seg, 2 * S5_LANES), F32),
            pltpu.VMEM((S5_SEGS, 2 * S5_LANES), F32),
            pltpu.VMEM((S5_SEGS, 2 * S5_LANES), F32),
            pltpu.VMEM((1, 2 * S5_LANES), F32),
            pltpu.VMEM((tile, S5_WIDTH), F32),
            pltpu.VMEM((tile, LANE), F32),
            pltpu.VMEM((tile, LANE), F32),
        ],
        compiler_params=_cparams("arbitrary", "arbitrary"),
        name="s5_scan",
    )(zm, zm, h0, lam, bd, cd, skip, gw, gb)


def _s5_params(lam_re, lam_im, log_step, b_re, b_im, c_re, c_im):
    step = jnp.exp(log_step)[..., None]
    mag = jnp.exp(lam_re * step)
    bar_re, bar_im = mag * jnp.cos(lam_im * step), mag * jnp.sin(lam_im * step)
    den = lam_re * lam_re + lam_im * lam_im
    q_re = ((bar_re - 1.0) * lam_re + bar_im * lam_im) / den
    q_im = (bar_im * lam_re - (bar_re - 1.0) * lam_im) / den
    bb_re = q_re[..., None] * b_re - q_im[..., None] * b_im
    bb_im = q_re[..., None] * b_im + q_im[..., None] * b_re
    eye = jnp.eye(S5_GROUPS, dtype=F32)

    def in_mat(x):
        xt = jnp.swapaxes(x, 2, 3)
        return (xt[:, :, :, None, :] * eye[None, :, None, :, None]).reshape(2, S5_WIDTH, S5_LANES)

    def out_mat(x):
        xt = jnp.swapaxes(x, 2, 3)
        return (xt[:, :, :, None, :] * eye[None, :, None, :, None]).reshape(2, S5_LANES, S5_WIDTH)

    bd = jnp.concatenate([in_mat(bb_re), in_mat(bb_im)], axis=2).astype(BF16)
    cd = jnp.concatenate([out_mat(c_re), out_mat(-c_im)], axis=1).astype(BF16)
    lam_rows = jnp.stack([bar_re[0], bar_im[0], bar_re[1], bar_im[1]])
    return lam_rows.reshape(4, S5_LANES), bd, cd


ROPE_HALF = HEAD_DIM // 4


def _rope_tables(seq_len):
    t = jnp.arange(seq_len)
    row = (t // GRID_W).astype(F32)
    col = (t % GRID_W).astype(F32)
    inv = ROPE_BASE ** (-jnp.arange(ROPE_HALF, dtype=F32) / ROPE_HALF)
    ang_r = row[:, None] * inv
    ang_c = col[:, None] * inv
    cos = jnp.concatenate([jnp.cos(ang_r), jnp.cos(ang_r), jnp.cos(ang_c), jnp.cos(ang_c)], axis=-1)
    sin = jnp.concatenate([-jnp.sin(ang_r), jnp.sin(ang_r), -jnp.sin(ang_c), jnp.sin(ang_c)], axis=-1)
    return jnp.tile(cos, (1, LANE // HEAD_DIM)), jnp.tile(sin, (1, LANE // HEAD_DIM))


def _rope(x, cos, sin):
    lane = lax.broadcasted_iota(jnp.int32, x.shape, 1)
    first = (lane % (2 * ROPE_HALF)) < ROPE_HALF
    partner = jnp.where(first, pltpu.roll(x, LANE - ROPE_HALF, axis=1), pltpu.roll(x, ROPE_HALF, axis=1))
    return x * cos + partner * sin


def _group_attention(qg, ks, vs, masks, sink_col):
    qb = qg.astype(BF16)
    ss = []
    m = sink_col
    for k, mk in zip(ks, masks):
        s = lax.dot_general(qb, k.astype(BF16), (((1,), (1,)), ((), ())), preferred_element_type=F32)
        s = s * (HEAD_DIM ** -0.5)
        if mk is not None:
            s = jnp.where(mk, s, NEG_INF)
        ss.append(s)
        m = jnp.maximum(m, jnp.max(s, axis=-1, keepdims=True))
    den = jnp.exp(sink_col - m)
    acc = jnp.zeros((qg.shape[0], HEAD_DIM), F32)
    for s, v in zip(ss, vs):
        p = jnp.exp(s - m)
        den = den + jnp.sum(p, axis=-1, keepdims=True)
        acc = acc + jnp.dot(p.astype(BF16), v.astype(BF16), preferred_element_type=F32)
    return acc / den


def _heads_to_rows(q, g):
    return jnp.concatenate([q[:, (Q_PER_KV * g + r) * HEAD_DIM:(Q_PER_KV * g + r + 1) * HEAD_DIM]
                            for r in range(Q_PER_KV)], axis=0)


def _sink_col(sink_ref, g, t):
    return jnp.concatenate([jnp.full((t, 1), sink_ref[Q_PER_KV * g + r], F32) for r in range(Q_PER_KV)], axis=0)


def _rows_to_heads(outs, t):
    return jnp.concatenate([outs[g][r * t:(r + 1) * t] for g in range(N_KV_HEADS) for r in range(Q_PER_KV)], axis=1)


def _ctx_attn_kernel(q_ref, k_ref, v_ref, sink_ref, o_ref):
    q, k, v = q_ref[...], k_ref[...], v_ref[...]
    t = q.shape[0]
    outs = []
    for g in range(N_KV_HEADS):
        sl = slice(g * HEAD_DIM, (g + 1) * HEAD_DIM)
        outs.append(_group_attention(_heads_to_rows(q, g), [k[:, sl]], [v[:, sl]], [None], _sink_col(sink_ref, g, t)))
    o_ref[...] = _rows_to_heads(outs, t)


def _context_attention(zm, sink):
    return pl.pallas_call(
        _ctx_attn_kernel,
        grid=(BATCH,),
        in_specs=[
            pl.BlockSpec((SEQ, ATTN_WIDTH), lambda b: (b, SPLIT_Q // ATTN_WIDTH)),
            pl.BlockSpec((SEQ, KV_WIDTH), lambda b: (b, SPLIT_K // KV_WIDTH)),
            pl.BlockSpec((SEQ, KV_WIDTH), lambda b: (b, SPLIT_V // KV_WIDTH)),
            pl.BlockSpec(memory_space=pltpu.SMEM),
        ],
        out_specs=pl.BlockSpec((SEQ, ATTN_WIDTH), lambda b: (b, 0)),
        out_shape=jax.ShapeDtypeStruct((N_PROMPT_TOK, ATTN_WIDTH), F32),
        compiler_params=_cparams("arbitrary"),
        name="ctx_attention",
    )(zm, zm, zm, sink)


def _latent_attn_kernel(q_ref, kl_ref, kc_ref, kr_ref, vl_ref, vc_ref, vr_ref, ck_ref, cv_ref, cos_ref, sin_ref,
                        sink_ref, o_ref):
    i = pl.program_id(1)
    nb = DEC_SEQ // BLOCK

    def table(ref, blk):
        return ref[pl.ds(pl.multiple_of(blk * BLOCK, BLOCK), BLOCK), :]

    left, right = jnp.maximum(i - 1, 0), jnp.minimum(i + 1, nb - 1)
    cos_q, sin_q = table(cos_ref, i), table(sin_ref, i)
    q = jnp.concatenate([_rope(q_ref[:, c:c + LANE], cos_q, sin_q) for c in range(0, ATTN_WIDTH, LANE)], axis=1)
    kw = jnp.concatenate([
        _rope(kl_ref[...], table(cos_ref, left), table(sin_ref, left)),
        _rope(kc_ref[...], cos_q, sin_q),
        _rope(kr_ref[...], table(cos_ref, right), table(sin_ref, right)),
    ], axis=0)
    vw = jnp.concatenate([vl_ref[...], vc_ref[...], vr_ref[...]], axis=0)

    m_rows = Q_PER_KV * BLOCK
    r = lax.broadcasted_iota(jnp.int32, (m_rows, 3 * BLOCK), 0) % BLOCK
    c = lax.broadcasted_iota(jnp.int32, (m_rows, 3 * BLOCK), 1)
    kpos = c + (i - 1) * BLOCK
    win = (kpos >= 0) & (kpos < DEC_SEQ) & (jnp.abs(r + BLOCK - c) <= WINDOW)

    ck, cv = ck_ref[...], cv_ref[...]
    outs = []
    for g in range(N_KV_HEADS):
        sl = slice(g * HEAD_DIM, (g + 1) * HEAD_DIM)
        outs.append(_group_attention(_heads_to_rows(q, g), [ck[:, sl], kw[:, sl]], [cv[:, sl], vw[:, sl]],
                                     [None, win], _sink_col(sink_ref, g, BLOCK)))
    o_ref[...] = _rows_to_heads(outs, BLOCK)


def _latent_attention(zm, cache_k, cache_v, layer, cos, sin, sink):
    nb = DEC_SEQ // BLOCK
    blk0 = N_PROMPT_TOK // BLOCK
    kcol, vcol = SPLIT_K // KV_WIDTH, SPLIT_V // KV_WIDTH

    def row(b, i):
        return blk0 + b * nb + i

    def win_specs(col):
        return [
            pl.BlockSpec((BLOCK, KV_WIDTH), lambda b, i: (row(b, jnp.maximum(i - 1, 0)), col)),
            pl.BlockSpec((BLOCK, KV_WIDTH), lambda b, i: (row(b, i), col)),
            pl.BlockSpec((BLOCK, KV_WIDTH), lambda b, i: (row(b, jnp.minimum(i + 1, nb - 1)), col)),
        ]

    ctx_spec = pl.BlockSpec((None, None, PAST_LEN, KV_WIDTH), lambda b, i: (b, layer, 0, 0))
    tab_spec = pl.BlockSpec((DEC_SEQ, LANE), lambda b, i: (0, 0))
    return pl.pallas_call(
        _latent_attn_kernel,
        grid=(DEC_BATCH, nb),
        in_specs=[pl.BlockSpec((BLOCK, ATTN_WIDTH), lambda b, i: (row(b, i), SPLIT_Q // ATTN_WIDTH))]
        + win_specs(kcol) + win_specs(vcol)
        + [ctx_spec, ctx_spec, tab_spec, tab_spec, pl.BlockSpec(memory_space=pltpu.SMEM)],
        out_specs=pl.BlockSpec((BLOCK, ATTN_WIDTH), lambda b, i: (b * nb + i, 0)),
        out_shape=jax.ShapeDtypeStruct((N_SAMPLE_TOK, ATTN_WIDTH), F32),
        compiler_params=_cparams("arbitrary", "arbitrary"),
        name="latent_attention",
    )(zm, zm, zm, zm, zm, zm, zm, cache_k, cache_v, cos, sin, sink)


def kernel(x_prompt, x_sample, cache_k, cache_v, state_s5_re, state_s5_im, c, c_ctx, ada_w, ada_b, norm1_g, w_in, hy_conv_w, hy_conv_b, hy_pos_w1, hy_pos_b1, hy_pos_w2, hy_pos_b2, hy_pos_w3, hy_sin_freq, hy_decay, hy_skip, s5_lam_re, s5_lam_im, s5_log_step, s5_b_re, s5_b_im, s5_c_re, s5_c_im, s5_skip, s5_glu_w, s5_glu_b, attn_sink, proj_a, proj_b, proj_c, w_out, norm2_g, ffn_w_gate, ffn_w_up, ffn_w_down, final_norm_g):
    cvec = jnp.concatenate([c_ctx[None], c, jnp.zeros((MOD_ROWS - 1 - DEC_BATCH, D_MODEL), F32)], axis=0)
    mod = _modulation(cvec, ada_w, ada_b).reshape(DEPTH * MOD_ROWS * 6, 1, D_MODEL)

    x = jnp.concatenate([x_prompt.reshape(N_PROMPT_TOK, D_MODEL), x_sample.reshape(N_SAMPLE_TOK, D_MODEL)], axis=0)
    w_main = w_in[:, :, :MAIN_IN].astype(BF16)
    w_gate = w_in[:, :, MAIN_IN:].astype(BF16)
    pa, pb, pc, wo = proj_a.astype(BF16), proj_b.astype(BF16), proj_c.astype(BF16), w_out.astype(BF16)
    wg, wu, wd = ffn_w_gate.astype(BF16), ffn_w_up.astype(BF16), ffn_w_down.astype(BF16)

    ctx_k = cache_k.reshape(DEC_BATCH, DEPTH, PAST_LEN, KV_WIDTH)
    ctx_v = cache_v.reshape(DEC_BATCH, DEPTH, PAST_LEN, KV_WIDTH)
    rope_cos, rope_sin = _rope_tables(DEC_SEQ)
    mats_p, mats_s = _dft_mats(min(SEQ, HY_BLOCK)), _dft_mats(min(DEC_SEQ, HY_BLOCK))

    ks_out, vs_out, sre_out, sim_out = [], [], [], []
    for l in range(DEPTH):
        p = {
            'hy_pos_w1': hy_pos_w1[l], 'hy_pos_b1': hy_pos_b1[l],
            'hy_pos_w2': hy_pos_w2[l], 'hy_pos_b2': hy_pos_b2[l], 'hy_pos_w3': hy_pos_w3[l],
            'hy_sin_freq': hy_sin_freq[l], 'hy_decay': hy_decay[l],
        }
        g1 = norm1_g[l].reshape(1, D_MODEL)
        zm = _in_proj(x, mod, l, g1, w_main[l])
        u, vb = _short_conv(zm, hy_conv_w[l], hy_conv_b[l].reshape(1, HY_IN))
        ya_p = _hyena_path(u, vb, 0, BATCH, SEQ, mats_p, _hyena_tables(SEQ, mats_p, p), hy_skip[l])
        ya_s = _hyena_path(u, vb, N_PROMPT_TOK, DEC_BATCH, DEC_SEQ, mats_s, _hyena_tables(DEC_SEQ, mats_s, p),
                           hy_skip[l])
        k_l = zm[:N_PROMPT_TOK, SPLIT_K:SPLIT_V].reshape(BATCH, SEQ, N_KV_HEADS, HEAD_DIM)
        v_l = zm[:N_PROMPT_TOK, SPLIT_V:].reshape(BATCH, SEQ, N_KV_HEADS, HEAD_DIM)
        yc_p = _context_attention(zm, attn_sink[l])
        yc_s = _latent_attention(zm, ctx_k, ctx_v, l, rope_cos, rope_sin, attn_sink[l])

        lam, bd, cd = _s5_params(s5_lam_re[l], s5_lam_im[l], s5_log_step[l], s5_b_re[l], s5_b_im[l],
                                 s5_c_re[l], s5_c_im[l])
        s5_w = (lam, bd, cd, s5_skip[l].reshape(1, S5_WIDTH), s5_glu_w[l].astype(BF16),
                s5_glu_b[l].reshape(1, 2 * S5_WIDTH))
        h0_s = jnp.stack([state_s5_re[:, l, 0], state_s5_im[:, l, 0], state_s5_re[:, l, 1], state_s5_im[:, l, 1]],
                         axis=1).reshape(DEC_BATCH, 4, S5_LANES)
        yb_p, fin_p = _s5_branch(zm, 0, BATCH, SEQ, SEQ, jnp.zeros((BATCH, 4, S5_LANES), F32), *s5_w)
        yb_s, _ = _s5_branch(zm, N_PROMPT_TOK, DEC_BATCH, DEC_SEQ, 512, h0_s, *s5_w)
        fin_p = fin_p.reshape(BATCH, 2, 2, S5_GROUPS, S5_STATE)

        ks_out.append(k_l)
        vs_out.append(v_l)
        sre_out.append(fin_p[:, :, 0])
        sim_out.append(fin_p[:, :, 1])
        x = _merge((ya_p, ya_s), (yb_p, yb_s), (yc_p, yc_s), x, mod, l, g1, w_gate[l], pa[l], pb[l], pc[l], wo[l])
        x = _ffn(x, mod, l, norm2_g[l].reshape(1, D_MODEL), wg[l], wu[l], wd[l])

    y_prompt, y_sample = _final_norm(x, final_norm_g.reshape(1, D_MODEL))
    return (y_prompt.reshape(BATCH, SEQ, D_MODEL), y_sample.reshape(DEC_BATCH, DEC_SEQ, D_MODEL),
            jnp.stack(ks_out, axis=1), jnp.stack(vs_out, axis=1), jnp.stack(sre_out, axis=1), jnp.stack(sim_out, axis=1))
```

```python
import functools
import math

import jax
import jax.numpy as jnp
import numpy as np
from jax import lax
from jax.experimental import pallas as pl
from jax.experimental.pallas import tpu as pltpu

D_MODEL = 1024
BATCH = 16
SEQ = 256
DEPTH = 4
DEC_BATCH = 2
DEC_SEQ = 4096
PAST_LEN = 512
GRID_W = 64
N_BRANCH = 3
HY_WIDTH = 256
HY_ORDER = 2
HY_SHORT_K = 3
HY_POS_EMB = 33
HY_POS_BANDS = (HY_POS_EMB - 1) // 2
HY_FILTER_HIDDEN = 64
S5_WIDTH = 256
S5_GROUP = 16
S5_GROUPS = S5_WIDTH // S5_GROUP
S5_STATE = 64
N_HEADS = 8
N_KV_HEADS = 2
Q_PER_KV = N_HEADS // N_KV_HEADS
HEAD_DIM = 64
ATTN_WIDTH = N_HEADS * HEAD_DIM
KV_WIDTH = N_KV_HEADS * HEAD_DIM
WINDOW = 128
BLOCK = 128
ROPE_BASE = 10000.0
FFN_HIDDEN = ((8 * D_MODEL // 3 + 255) // 256) * 256
HY_IN = (HY_ORDER + 1) * HY_WIDTH
GATE_IN = N_BRANCH * D_MODEL
MAIN_IN = HY_IN + S5_WIDTH + ATTN_WIDTH + 2 * KV_WIDTH
SPLIT_S5 = HY_IN
SPLIT_Q = SPLIT_S5 + S5_WIDTH
SPLIT_K = SPLIT_Q + ATTN_WIDTH
SPLIT_V = SPLIT_K + KV_WIDTH

F32 = jnp.float32
BF16 = jnp.bfloat16
EPS = 1e-6
NEG_INF = -1e30

N_PROMPT_TOK = BATCH * SEQ
N_SAMPLE_TOK = DEC_BATCH * DEC_SEQ
N_TOK = N_PROMPT_TOK + N_SAMPLE_TOK
LANE = 128
MOD_ROWS = 8
TM = 256
VMEM_LIMIT = 56 * 1024 * 1024


def _cparams(*sem):
    return pltpu.CompilerParams(dimension_semantics=sem, vmem_limit_bytes=VMEM_LIMIT)


def _mod_row(i):
    n_p = N_PROMPT_TOK // TM
    per_b = DEC_SEQ // TM
    return jnp.where(i < n_p, 0, 1 + (i - n_p) // per_b)


def _mod_spec(layer, k):
    return pl.BlockSpec((None, 1, D_MODEL), lambda i: ((layer * MOD_ROWS + _mod_row(i)) * 6 + k, 0, 0))


def _resident(shape, layer=None):
    if layer is None:
        return pl.BlockSpec(shape, lambda i: (0,) * len(shape), pipeline_mode=pl.Buffered(1))
    return pl.BlockSpec((None,) + shape, lambda i: (layer,) + (0,) * len(shape), pipeline_mode=pl.Buffered(1))


def _rms(x, g):
    return x * lax.rsqrt(jnp.mean(x * x, axis=-1, keepdims=True) + EPS) * g


def _mod_kernel(c_ref, w_ref, b_ref, o_ref):
    c = c_ref[...]
    s = (c * jax.nn.sigmoid(c)).astype(BF16)
    o_ref[...] = jnp.dot(s, w_ref[...].astype(BF16), preferred_element_type=F32) + b_ref[...]


def _modulation(cvec, ada_w, ada_b):
    tn = 1536
    return pl.pallas_call(
        _mod_kernel,
        grid=(DEPTH, 6 * D_MODEL // tn),
        in_specs=[
            pl.BlockSpec((MOD_ROWS, D_MODEL), lambda l, j: (0, 0)),
            pl.BlockSpec((None, D_MODEL, tn), lambda l, j: (l, 0, j)),
            pl.BlockSpec((None, 1, tn), lambda l, j: (l, 0, j)),
        ],
        out_specs=pl.BlockSpec((None, MOD_ROWS, tn), lambda l, j: (l, 0, j)),
        out_shape=jax.ShapeDtypeStruct((DEPTH, MOD_ROWS, 6 * D_MODEL), F32),
        compiler_params=_cparams("arbitrary", "arbitrary"),
        name="adaln_mod",
    )(cvec, ada_w, ada_b.reshape(DEPTH, 1, 6 * D_MODEL))


def _inproj_kernel(x_ref, sh_ref, sc_ref, g_ref, wm_ref, zm_ref):
    h = (_rms(x_ref[...], g_ref[...]) * (1.0 + sc_ref[...]) + sh_ref[...]).astype(BF16)
    zm_ref[...] = jnp.dot(h, wm_ref[...], preferred_element_type=F32)


def _in_proj(x, mod, layer, g, w_main):
    return pl.pallas_call(
        _inproj_kernel,
        grid=(N_TOK // TM,),
        in_specs=[
            pl.BlockSpec((TM, D_MODEL), lambda i: (i, 0)),
            _mod_spec(layer, 0),
            _mod_spec(layer, 1),
            _resident((1, D_MODEL), layer),
            _resident((D_MODEL, MAIN_IN), layer),
        ],
        out_specs=pl.BlockSpec((TM, MAIN_IN), lambda i: (i, 0)),
        out_shape=jax.ShapeDtypeStruct((N_TOK, MAIN_IN), F32),
        compiler_params=_cparams("arbitrary"),
        name="in_proj",
    )(x, mod, mod, g, w_main)


N_PROMPT_TILES = N_PROMPT_TOK // TM


def _path_specs(width):
    return [pl.BlockSpec((TM, width), lambda i: (jnp.minimum(i, N_PROMPT_TILES - 1), 0)),
            pl.BlockSpec((TM, width), lambda i: (jnp.maximum(i - N_PROMPT_TILES, 0), 0))]


def _merge_kernel(yap_ref, yas_ref, ybp_ref, ybs_ref, ycp_ref, ycs_ref, x_ref, sh_ref, sc_ref, g1_ref, g_ref,
                  wg_ref, pa_ref, pb_ref, pc_ref, wo_ref, o_ref):
    is_prompt = pl.program_id(0) < N_PROMPT_TILES
    x = x_ref[...]
    h = (_rms(x, g_ref[...]) * (1.0 + sc_ref[...]) + sh_ref[...]).astype(BF16)
    branches = ((yap_ref, yas_ref, pa_ref), (ybp_ref, ybs_ref, pb_ref), (ycp_ref, ycs_ref, pc_ref))
    m = jnp.zeros((TM, D_MODEL), F32)
    for k, (p_ref, s_ref, w_ref) in enumerate(branches):
        y = jnp.where(is_prompt, p_ref[...], s_ref[...]).astype(BF16)
        gate = jax.nn.sigmoid(jnp.dot(h, wg_ref[:, k * D_MODEL:(k + 1) * D_MODEL], preferred_element_type=F32))
        m = m + gate * jnp.dot(y, w_ref[...], preferred_element_type=F32)
    o_ref[...] = x + g1_ref[...] * jnp.dot(m.astype(BF16), wo_ref[...], preferred_element_type=F32)


def _merge(ya, yb, yc, x, mod, layer, g, w_gate, pa, pb, pc, wo):
    return pl.pallas_call(
        _merge_kernel,
        grid=(N_TOK // TM,),
        in_specs=_path_specs(HY_WIDTH) + _path_specs(S5_WIDTH) + _path_specs(ATTN_WIDTH) + [
            pl.BlockSpec((TM, D_MODEL), lambda i: (i, 0)),
            _mod_spec(layer, 0),
            _mod_spec(layer, 1),
            _mod_spec(layer, 2),
            _resident((1, D_MODEL), layer),
            _resident((D_MODEL, GATE_IN), layer),
            _resident((HY_WIDTH, D_MODEL), layer),
            _resident((S5_WIDTH, D_MODEL), layer),
            _resident((ATTN_WIDTH, D_MODEL), layer),
            _resident((D_MODEL, D_MODEL), layer),
        ],
        out_specs=pl.BlockSpec((TM, D_MODEL), lambda i: (i, 0)),
        out_shape=jax.ShapeDtypeStruct((N_TOK, D_MODEL), F32),
        compiler_params=_cparams("arbitrary"),
        name="merge_out",
    )(*ya, *yb, *yc, x, mod, mod, mod, g, w_gate, pa, pb, pc, wo)


def _ffn_kernel(x_ref, sh_ref, sc_ref, gt_ref, g_ref, wg_ref, wu_ref, wd_ref, o_ref):
    x = x_ref[...]
    h = (_rms(x, g_ref[...]) * (1.0 + sc_ref[...]) + sh_ref[...]).astype(BF16)
    a = jnp.dot(h, wg_ref[...], preferred_element_type=F32)
    b = jnp.dot(h, wu_ref[...], preferred_element_type=F32)
    f = (a * jax.nn.sigmoid(a) * b).astype(BF16)
    o_ref[...] = x + gt_ref[...] * jnp.dot(f, wd_ref[...], preferred_element_type=F32)


def _ffn(x, mod, layer, g, wg, wu, wd):
    return pl.pallas_call(
        _ffn_kernel,
        grid=(N_TOK // TM,),
        in_specs=[
            pl.BlockSpec((TM, D_MODEL), lambda i: (i, 0)),
            _mod_spec(layer, 3),
            _mod_spec(layer, 4),
            _mod_spec(layer, 5),
            _resident((1, D_MODEL), layer),
            _resident((D_MODEL, FFN_HIDDEN), layer),
            _resident((D_MODEL, FFN_HIDDEN), layer),
            _resident((FFN_HIDDEN, D_MODEL), layer),
        ],
        out_specs=pl.BlockSpec((TM, D_MODEL), lambda i: (i, 0)),
        out_shape=jax.ShapeDtypeStruct((N_TOK, D_MODEL), F32),
        compiler_params=_cparams("arbitrary"),
        name="ffn",
    )(x, mod, mod, mod, g, wg, wu, wd)


def _final_norm_kernel(x_ref, g_ref, op_ref, os_ref):
    y = _rms(x_ref[...], g_ref[...])
    is_prompt = pl.program_id(0) < N_PROMPT_TILES

    @pl.when(is_prompt)
    def _():
        op_ref[...] = y

    @pl.when(jnp.logical_not(is_prompt))
    def _():
        os_ref[...] = y


def _final_norm(x, g):
    return pl.pallas_call(
        _final_norm_kernel,
        grid=(N_TOK // TM,),
        in_specs=[pl.BlockSpec((TM, D_MODEL), lambda i: (i, 0)), _resident((1, D_MODEL))],
        out_specs=_path_specs(D_MODEL),
        out_shape=[jax.ShapeDtypeStruct((N_PROMPT_TOK, D_MODEL), F32),
                   jax.ShapeDtypeStruct((N_SAMPLE_TOK, D_MODEL), F32)],
        compiler_params=_cparams("arbitrary"),
        name="final_norm",
    )(x, g)


SUBLANE = 8
HY_TILE = 256
HY_BLOCK = 512
HY_ROWS = 64


def _dft_mats(blk):
    n = 2 * blk
    idx = jnp.arange(blk, dtype=jnp.int32)
    ang = (2.0 * math.pi / n) * ((idx[:, None] * idx[None, :]) % n).astype(F32)
    alt = jnp.where(idx % 2 == 0, 1.0, -1.0).astype(F32)
    return jnp.stack([jnp.cos(ang), jnp.where((idx == 0)[:, None], alt[None, :], -jnp.sin(ang))]).astype(BF16)


def _short_conv_kernel(x_ref, xp_ref, xn_ref, w_ref, b_ref, u_ref, vb_ref):
    i = pl.program_id(0)
    n_p = N_PROMPT_TOK // HY_TILE
    per_p, per_s = SEQ // HY_TILE, DEC_SEQ // HY_TILE
    idx = jnp.where(i < n_p, i % per_p, (i - n_p) % per_s)
    per = jnp.where(i < n_p, per_p, per_s)
    has_prev = (idx > 0).astype(F32)
    has_next = (idx < per - 1).astype(F32)
    x = x_ref[...]
    row = lax.broadcasted_iota(jnp.int32, (HY_TILE, 1), 0)
    x_prev = jnp.where(row == 0, xp_ref[SUBLANE - 1:SUBLANE, :] * has_prev, pltpu.roll(x, 1, axis=0))
    x_next = jnp.where(row == HY_TILE - 1, xn_ref[0:1, :] * has_next, pltpu.roll(x, HY_TILE - 1, axis=0))
    u = b_ref[...] + x_prev * w_ref[0:1, :] + x * w_ref[1:2, :] + x_next * w_ref[2:3, :]
    u_ref[...] = u
    vb_ref[...] = u[:, :HY_WIDTH].astype(BF16)


def _short_conv(zm, w, b):
    per8 = HY_TILE // SUBLANE
    last8 = N_TOK // SUBLANE - 1
    return pl.pallas_call(
        _short_conv_kernel,
        grid=(N_TOK // HY_TILE,),
        in_specs=[
            pl.BlockSpec((HY_TILE, HY_IN), lambda i: (i, 0)),
            pl.BlockSpec((SUBLANE, HY_IN), lambda i: (jnp.maximum(i * per8 - 1, 0), 0)),
            pl.BlockSpec((SUBLANE, HY_IN), lambda i: (jnp.minimum((i + 1) * per8, last8), 0)),
            pl.BlockSpec((HY_SHORT_K, HY_IN), lambda i: (0, 0)),
            pl.BlockSpec((1, HY_IN), lambda i: (0, 0)),
        ],
        out_specs=[
            pl.BlockSpec((HY_TILE, HY_IN), lambda i: (i, 0)),
            pl.BlockSpec((HY_TILE, HY_WIDTH), lambda i: (i, 0)),
        ],
        out_shape=[
            jax.ShapeDtypeStruct((N_TOK, HY_IN), F32),
            jax.ShapeDtypeStruct((N_TOK, HY_WIDTH), BF16),
        ],
        compiler_params=_cparams("arbitrary"),
        name="hyena_short_conv",
    )(zm, zm, zm, w, b)


def _dft_fwd_kernel(group, z_ref, f_ref, o_ref):
    blk = f_ref.shape[-1]
    z = jnp.concatenate([z_ref[g * blk:(g + 1) * blk, :] for g in range(group)], axis=1)
    for c in range(2):
        x = jnp.dot(f_ref[c], z, preferred_element_type=F32)
        for g in range(group):
            o_ref[g, c] = x[:, g * HY_WIDTH:(g + 1) * HY_WIDTH]


def _dft_fwd(z, row0, n_blk, mats):
    blk = mats.shape[-1]
    group = min(n_blk, 8)
    return pl.pallas_call(
        functools.partial(_dft_fwd_kernel, group),
        grid=(n_blk // group,),
        in_specs=[
            pl.BlockSpec((group * blk, HY_WIDTH), lambda b: (row0 // (group * blk) + b, 0)),
            pl.BlockSpec((2, blk, blk), lambda b: (0, 0, 0)),
        ],
        out_specs=pl.BlockSpec((group, 2, blk, HY_WIDTH), lambda b: (b, 0, 0, 0)),
        out_shape=jax.ShapeDtypeStruct((n_blk, 2, blk, HY_WIDTH), F32),
        compiler_params=_cparams("arbitrary"),
        name="hyena_dft_fwd",
    )(z, mats)


def _block_conv_kernel(n_blk, last, z_ref, k_ref, f_ref, g_ref, zp_ref, skip_ref, o_ref, *rest):
    s_ref = rest[-1]
    n = pl.program_id(1)
    blk = f_ref.shape[-1]

    def product(r, first_row_packed):
        acc_r = acc_i = jnp.zeros((r.size, HY_WIDTH), F32)
        fix_r = fix_i = acc_r
        for i in range(n_blk):
            d = n - i + n_blk - 1
            zr, zi = z_ref[i, 0, r, :], z_ref[i, 1, r, :]
            kr, ki = k_ref[d, 0, r, :], k_ref[d, 1, r, :]
            acc_r = acc_r + (zr * kr - zi * ki)
            acc_i = acc_i + (zr * ki + zi * kr)
            if first_row_packed:
                fix_r = fix_r + zr * kr
                fix_i = fix_i + zi * ki
        if first_row_packed:
            row0 = lax.broadcasted_iota(jnp.int32, (r.size, 1), 0) == 0
            acc_r = jnp.where(row0, fix_r, acc_r)
            acc_i = jnp.where(row0, fix_i, acc_i)
        s_ref[0, r, :] = acc_r.astype(BF16)
        s_ref[1, r, :] = acc_i.astype(BF16)

    def chunk(c, carry):
        product(pl.ds(pl.multiple_of(c * HY_ROWS, HY_ROWS), HY_ROWS), False)
        return carry

    lax.fori_loop(0, blk // HY_ROWS, chunk, 0)
    product(pl.ds(0, 2 * SUBLANE), True)

    conv_c = jnp.dot(f_ref[0], s_ref[0], preferred_element_type=F32)
    conv_s = jnp.dot(f_ref[1], s_ref[1], preferred_element_type=F32)
    t = lax.broadcasted_iota(jnp.int32, (blk, 1), 0)
    alt = jnp.where(t % 2 == 0, 1.0, -1.0).astype(F32)
    conv = conv_c + jnp.where(t == 0, 0.0, conv_s) + alt * s_ref[1, 0:1, :].astype(F32)
    z = g_ref[...] * (conv + skip_ref[...] * zp_ref[...])
    o_ref[...] = z
    if not last:
        rest[0][...] = z.astype(BF16)


def _block_conv(zspec, tables, mats, u, gate_col, z_prev, z_prev_row0, skip, row0, n_seq, last):
    blk = mats.shape[-1]
    n_blk = zspec.shape[0] // n_seq

    def rows(base):
        return lambda b, n: (base // blk + b * n_blk + n, 0)

    out_spec = pl.BlockSpec((blk, HY_WIDTH), rows(0))
    out_shape = [jax.ShapeDtypeStruct((n_seq * n_blk * blk, HY_WIDTH), F32)]
    if not last:
        out_shape.append(jax.ShapeDtypeStruct((n_seq * n_blk * blk, HY_WIDTH), BF16))
    return pl.pallas_call(
        functools.partial(_block_conv_kernel, n_blk, last),
        grid=(n_seq, n_blk),
        in_specs=[
            pl.BlockSpec((n_blk, 2, blk, HY_WIDTH), lambda b, n: (b, 0, 0, 0)),
            pl.BlockSpec(tables.shape, lambda b, n: (0, 0, 0, 0), pipeline_mode=pl.Buffered(1)),
            pl.BlockSpec((2, blk, blk), lambda b, n: (0, 0, 0)),
            pl.BlockSpec((blk, HY_WIDTH), lambda b, n: (row0 // blk + b * n_blk + n, gate_col)),
            pl.BlockSpec((blk, HY_WIDTH), rows(z_prev_row0)),
            pl.BlockSpec((1, HY_WIDTH), lambda b, n: (0, 0)),
        ],
        out_specs=[out_spec] * len(out_shape),
        out_shape=out_shape,
        scratch_shapes=[pltpu.VMEM((2, blk, HY_WIDTH), BF16)],
        compiler_params=_cparams("arbitrary", "arbitrary"),
        name="hyena_block_conv",
    )(zspec, tables, mats, u, z_prev, skip)


def _taps_kernel(seq_len, rows, w1_ref, b1_ref, w2_ref, b2_ref, w3_ref, freq_ref, decay_ref, k_ref, norm_ref):
    i = pl.program_id(0)
    lag = i * rows + lax.broadcasted_iota(jnp.int32, (rows, 1), 0) - seq_len
    pos = jnp.abs(lag).astype(F32)
    t = pos * (1.0 / (seq_len - 1))
    band = lax.broadcasted_iota(jnp.int32, (1, HY_POS_BANDS), 1).astype(F32)
    band = 1e-4 + band * ((HY_POS_BANDS - 1 - 1e-4) / (HY_POS_BANDS - 1))
    ang = (pos * (2.0 * math.pi / seq_len)) * band
    dot = functools.partial(jnp.dot, precision=lax.Precision.HIGHEST, preferred_element_type=F32)
    pre = t * w1_ref[0:1, :]
    pre = pre + dot(jnp.cos(ang), w1_ref[1:1 + HY_POS_BANDS, :])
    pre = pre - dot(jnp.sin(ang), w1_ref[1 + HY_POS_BANDS:, :])
    freq = freq_ref[...]
    h = jnp.sin(freq * (pre + b1_ref[...]))
    h = jnp.sin(freq * (dot(h, w2_ref[...]) + b2_ref[...]))
    h = jnp.dot(h.astype(BF16), w3_ref[...].astype(BF16), preferred_element_type=F32)
    h = h * jnp.exp(-t * jnp.abs(decay_ref[...]))

    @pl.when(i == 0)
    def _():
        norm_ref[...] = jnp.zeros_like(norm_ref)

    for o in range(HY_ORDER):
        c0 = o * 2 * HY_WIDTH
        k = jnp.where(lag >= 0, h[:, c0:c0 + HY_WIDTH], h[:, c0 + HY_WIDTH:c0 + 2 * HY_WIDTH])
        k = jnp.where(lag == -seq_len, 0.0, k)
        k_ref[o] = k.astype(BF16)
        norm_ref[:, o * HY_WIDTH:(o + 1) * HY_WIDTH] += jnp.sum(jnp.abs(k), axis=0, keepdims=True)


def _hyena_taps(seq_len, w1, b1, w2, b2, w3, freq, decay):
    rows = min(2 * seq_len, 512)
    full = lambda a: pl.BlockSpec(a.shape, lambda i: (0,) * a.ndim)
    args = (w1, b1.reshape(1, -1), w2, b2.reshape(1, -1), w3, freq.reshape(1, -1), decay.reshape(1, -1))
    return pl.pallas_call(
        functools.partial(_taps_kernel, seq_len, rows),
        grid=(2 * seq_len // rows,),
        in_specs=[full(a) for a in args],
        out_specs=[pl.BlockSpec((HY_ORDER, rows, HY_WIDTH), lambda i: (0, i, 0)),
                   pl.BlockSpec((1, HY_ORDER * HY_WIDTH), lambda i: (0, 0))],
        out_shape=[jax.ShapeDtypeStruct((HY_ORDER, 2 * seq_len, HY_WIDTH), BF16),
                   jax.ShapeDtypeStruct((1, HY_ORDER * HY_WIDTH), F32)],
        compiler_params=_cparams("arbitrary"),
        name="hyena_taps",
    )(*args)


def _hyena_tables(seq_len, mats, filt):
    blk = mats.shape[-1]
    n_lag = 2 * seq_len // blk
    taps, norm = _hyena_taps(seq_len, *filt)
    spec = _dft_fwd(taps.reshape(HY_ORDER * 2 * seq_len, HY_WIDTH), 0, HY_ORDER * n_lag, mats)
    spec = spec.reshape(HY_ORDER, n_lag, 2, blk, HY_WIDTH)
    f = jnp.arange(blk)[:, None]
    alt = jnp.where(f % 2 == 0, 1.0, -1.0)
    scale = jnp.where(f == 0, 1.0, 2.0) / (2 * blk) / norm.reshape(HY_ORDER, 1, 1, 1, HY_WIDTH)
    return (spec[:, 1:] + alt * spec[:, :-1]) * scale


def _hyena_path(u, vb, row0, n_seq, seq_len, mats, tables, skip):
    n_blk = n_seq * seq_len // mats.shape[-1]
    z1, z1b = _block_conv(_dft_fwd(vb, row0, n_blk, mats), tables[0], mats, u, 1, u, row0, skip[0:1], row0, n_seq,
                          False)
    (ya,) = _block_conv(_dft_fwd(z1b, 0, n_blk, mats), tables[1], mats, u, 2, z1, 0, skip[1:2], row0, n_seq, True)
    return ya


S5_LANES = S5_GROUPS * S5_STATE
S5_SEGS = 8


def _s5_kernel(n_tiles, tile, u0_ref, u1_ref, h0_ref, lam_ref, bd_ref, cd_ref, skip_ref, gw_ref, gb_ref,
               y_ref, fin_ref, bu_ref, ybwd_ref, p_ref, p8_ref, cm_ref, fl_ref, carry_ref, up_ref, st0_ref, st1_ref):
    seg = tile // S5_SEGS
    n = S5_LANES
    b = pl.program_id(0)
    j = pl.program_id(1)

    @pl.when((b == 0) & (j == 0))
    def _powers():
        for d in range(2):
            lr = lam_ref[2 * d:2 * d + 1, :]
            li = lam_ref[2 * d + 1:2 * d + 2, :]

            def put(k, pr, pi, d=d):
                p_ref[d, pl.ds(k, 1), 0:n] = pr
                p_ref[d, pl.ds(k, 1), n:2 * n] = pi
                r = pl.ds(pl.multiple_of(k * S5_SEGS, S5_SEGS), S5_SEGS)
                p8_ref[d, r, 0:n] = jnp.broadcast_to(pr, (S5_SEGS, n))
                p8_ref[d, r, n:2 * n] = jnp.broadcast_to(pi, (S5_SEGS, n))

            put(0, lr, li)

            def body(k, c, lr=lr, li=li, put=put):
                pr, pi = c
                nr = pr * lr - pi * li
                ni = pr * li + pi * lr
                put(k, nr, ni)
                return nr, ni

            lax.fori_loop(1, seg, body, (lr, li))

    def sweep(d, t):
        first, last = (0, n_tiles - 1) if d == 0 else (n_tiles - 1, 0)

        @pl.when(t == first)
        def _init():
            carry_ref[0:1, 0:n] = h0_ref[2 * d:2 * d + 1, :]
            carry_ref[0:1, n:2 * n] = h0_ref[2 * d + 1:2 * d + 2, :]

        def gather(i, c):
            r = pl.ds(pl.multiple_of(i * S5_SEGS, S5_SEGS), S5_SEGS)
            up_ref[r, 0:LANE] = u0_ref[pl.ds(i, S5_SEGS, stride=seg), :]
            up_ref[r, LANE:2 * LANE] = u1_ref[pl.ds(i, S5_SEGS, stride=seg), :]
            return c

        lax.fori_loop(0, seg, gather, 0, unroll=2)
        up = up_ref[...]
        ub = up.astype(BF16)
        for c0 in range(0, 2 * n, 512):
            bu_ref[:, c0:c0 + 512] = jnp.dot(ub, bd_ref[d, :, c0:c0 + 512], preferred_element_type=F32)

        lr8 = jnp.broadcast_to(lam_ref[2 * d:2 * d + 1, :], (S5_SEGS, n))
        li8 = jnp.broadcast_to(lam_ref[2 * d + 1:2 * d + 2, :], (S5_SEGS, n))

        def rows(i):
            k = i if d == 0 else seg - 1 - i
            return pl.ds(pl.multiple_of(k * S5_SEGS, S5_SEGS), S5_SEGS)

        def local(i, c):
            hr, hi = c
            r = rows(i)
            nr = lr8 * hr - li8 * hi + bu_ref[r, 0:n]
            ni = lr8 * hi + li8 * hr + bu_ref[r, n:2 * n]
            bu_ref[r, 0:n] = nr
            bu_ref[r, n:2 * n] = ni
            return nr, ni

        zero = jnp.zeros((S5_SEGS, n), F32)
        fr, fi = lax.fori_loop(0, seg, local, (zero, zero))
        fl_ref[:, 0:n] = fr
        fl_ref[:, n:2 * n] = fi

        plr = p_ref[d, seg - 1:seg, 0:n]
        pli = p_ref[d, seg - 1:seg, n:2 * n]
        cr = carry_ref[0:1, 0:n]
        ci = carry_ref[0:1, n:2 * n]
        for s in (range(S5_SEGS) if d == 0 else reversed(range(S5_SEGS))):
            cm_ref[s:s + 1, 0:n] = cr
            cm_ref[s:s + 1, n:2 * n] = ci
            flr = fl_ref[s:s + 1, 0:n]
            fli = fl_ref[s:s + 1, n:2 * n]
            cr, ci = plr * cr - pli * ci + flr, plr * ci + pli * cr + fli
        carry_ref[0:1, 0:n] = cr
        carry_ref[0:1, n:2 * n] = ci

        @pl.when(t == last)
        def _final():
            fin_ref[2 * d:2 * d + 1, :] = cr
            fin_ref[2 * d + 1:2 * d + 2, :] = ci

        cmr = cm_ref[:, 0:n]
        cmi = cm_ref[:, n:2 * n]

        def fix(i, c):
            r = rows(i)
            pw = pl.ds(pl.multiple_of(i * S5_SEGS, S5_SEGS), S5_SEGS)
            pr = p8_ref[d, pw, 0:n]
            pi = p8_ref[d, pw, n:2 * n]
            bu_ref[r, 0:n] = bu_ref[r, 0:n] + (pr * cmr - pi * cmi)
            bu_ref[r, n:2 * n] = bu_ref[r, n:2 * n] + (pr * cmi + pi * cmr)
            return c

        lax.fori_loop(0, seg, fix, 0, unroll=2)

        y = jnp.dot(bu_ref[...].astype(BF16), cd_ref[d], preferred_element_type=F32)
        trow = pl.ds(pl.multiple_of(t * tile, tile), tile)
        if d == 1:
            ybwd_ref[trow, :] = y
        else:
            tot = up * skip_ref[...] + y + ybwd_ref[trow, :]
            g = jnp.dot(tot.astype(BF16), gw_ref[...], preferred_element_type=F32) + gb_ref[...]
            res = g[:, :S5_WIDTH] * jax.nn.sigmoid(g[:, S5_WIDTH:])
            st0_ref[...] = res[:, 0:LANE]
            st1_ref[...] = res[:, LANE:2 * LANE]
            per_seg = seg // S5_SEGS

            def unpermute(m, c):
                s = m // per_seg
                i0 = (m % per_seg) * S5_SEGS
                src = pl.ds(i0 * S5_SEGS + s, S5_SEGS, stride=S5_SEGS)
                dst = pl.ds(pl.multiple_of(m * S5_SEGS, S5_SEGS), S5_SEGS)
                y_ref[dst, 0:LANE] = st0_ref[src, :]
                y_ref[dst, LANE:2 * LANE] = st1_ref[src, :]
                return c

            lax.fori_loop(0, tile // S5_SEGS, unpermute, 0, unroll=2)

    @pl.when(j < n_tiles)
    def _bwd():
        sweep(1, n_tiles - 1 - j)

    @pl.when(j >= n_tiles)
    def _fwd():
        sweep(0, j - n_tiles)


def _s5_branch(zm, row0, n_seq, seq_len, tile, h0, lam, bd, cd, skip, gw, gb):
    n_tiles = seq_len // tile
    blk0 = row0 // tile
    col = SPLIT_S5 // LANE
    seg = tile // S5_SEGS

    def tile_of(j):
        return jnp.where(j < n_tiles, n_tiles - 1 - j, j - n_tiles)

    return pl.pallas_call(
        functools.partial(_s5_kernel, n_tiles, tile),
        grid=(n_seq, 2 * n_tiles),
        in_specs=[
            pl.BlockSpec((tile, LANE), lambda b, j: (blk0 + b * n_tiles + tile_of(j), col)),
            pl.BlockSpec((tile, LANE), lambda b, j: (blk0 + b * n_tiles + tile_of(j), col + 1)),
            pl.BlockSpec((None, 4, S5_LANES), lambda b, j: (b, 0, 0)),
            pl.BlockSpec((4, S5_LANES), lambda b, j: (0, 0)),
            pl.BlockSpec((2, S5_WIDTH, 2 * S5_LANES), lambda b, j: (0, 0, 0)),
            pl.BlockSpec((2, 2 * S5_LANES, S5_WIDTH), lambda b, j: (0, 0, 0)),
            pl.BlockSpec((1, S5_WIDTH), lambda b, j: (0, 0)),
            pl.BlockSpec((S5_WIDTH, 2 * S5_WIDTH), lambda b, j: (0, 0)),
            pl.BlockSpec((1, 2 * S5_WIDTH), lambda b, j: (0, 0)),
        ],
        out_specs=[
            pl.BlockSpec((tile, S5_WIDTH), lambda b, j: (b * n_tiles + jnp.maximum(j - n_tiles, 0), 0)),
            pl.BlockSpec((None, 4, S5_LANES), lambda b, j: (b, 0, 0)),
        ],
        out_shape=[
            jax.ShapeDtypeStruct((n_seq * seq_len, S5_WIDTH), F32),
            jax.ShapeDtypeStruct((n_seq, 4, S5_LANES), F32),
        ],
        scratch_shapes=[
            pltpu.VMEM((tile, 2 * S5_LANES), F32),
            pltpu.VMEM((seq_len, S5_WIDTH), F32),
            pltpu.VMEM((2, seg, 2 * S5_LANES), F32),
            pltpu.VMEM((2, tile, 2 * S5_LANES), F32),
            pltpu.VMEM((S5_SEGS, 2 * S5_LANES), F32),
            pltpu.VMEM((S5_SEGS, 2 * S5_LANES), F32),
            pltpu.VMEM((1, 2 * S5_LANES), F32),
            pltpu.VMEM((tile, S5_WIDTH), F32),
            pltpu.VMEM((tile, LANE), F32),
            pltpu.VMEM((tile, LANE), F32),
        ],
        compiler_params=_cparams("arbitrary", "arbitrary"),
        name="s5_scan",
    )(zm, zm, h0, lam, bd, cd, skip, gw, gb)


def _s5_params(lam_re, lam_im, log_step, b_re, b_im, c_re, c_im):
    step = jnp.exp(log_step)[..., None]
    mag = jnp.exp(lam_re * step)
    bar_re, bar_im = mag * jnp.cos(lam_im * step), mag * jnp.sin(lam_im * step)
    den = lam_re * lam_re + lam_im * lam_im
    q_re = ((bar_re - 1.0) * lam_re + bar_im * lam_im) / den
    q_im = (bar_im * lam_re - (bar_re - 1.0) * lam_im) / den
    bb_re = q_re[..., None] * b_re - q_im[..., None] * b_im
    bb_im = q_re[..., None] * b_im + q_im[..., None] * b_re
    eye = jnp.eye(S5_GROUPS, dtype=F32)

    def in_mat(x):
        xt = jnp.swapaxes(x, 2, 3)
        return (xt[:, :, :, None, :] * eye[None, :, None, :, None]).reshape(2, S5_WIDTH, S5_LANES)

    def out_mat(x):
        xt = jnp.swapaxes(x, 2, 3)
        return (xt[:, :, :, None, :] * eye[None, :, None, :, None]).reshape(2, S5_LANES, S5_WIDTH)

    bd = jnp.concatenate([in_mat(bb_re), in_mat(bb_im)], axis=2).astype(BF16)
    cd = jnp.concatenate([out_mat(c_re), out_mat(-c_im)], axis=1).astype(BF16)
    lam_rows = jnp.stack([bar_re[0], bar_im[0], bar_re[1], bar_im[1]])
    return lam_rows.reshape(4, S5_LANES), bd, cd


ROPE_HALF = HEAD_DIM // 4


def _rope_tables(seq_len):
    t = jnp.arange(seq_len)
    row = (t // GRID_W).astype(F32)
    col = (t % GRID_W).astype(F32)
    inv = ROPE_BASE ** (-jnp.arange(ROPE_HALF, dtype=F32) / ROPE_HALF)
    ang_r = row[:, None] * inv
    ang_c = col[:, None] * inv
    cos = jnp.concatenate([jnp.cos(ang_r), jnp.cos(ang_r), jnp.cos(ang_c), jnp.cos(ang_c)], axis=-1)
    sin = jnp.concatenate([-jnp.sin(ang_r), jnp.sin(ang_r), -jnp.sin(ang_c), jnp.sin(ang_c)], axis=-1)
    return jnp.tile(cos, (1, LANE // HEAD_DIM)), jnp.tile(sin, (1, LANE // HEAD_DIM))


def _rope(x, cos, sin):
    lane = lax.broadcasted_iota(jnp.int32, x.shape, 1)
    first = (lane % (2 * ROPE_HALF)) < ROPE_HALF
    partner = jnp.where(first, pltpu.roll(x, LANE - ROPE_HALF, axis=1), pltpu.roll(x, ROPE_HALF, axis=1))
    return x * cos + partner * sin


def _gqa(q, ks, vs, masks, sink_ref):
    t = q.shape[0]
    low = lax.broadcasted_iota(jnp.int32, (1, LANE), 1) < HEAD_DIM
    assert N_KV_HEADS * HEAD_DIM == LANE
    k_nat = [k.astype(BF16) for k in ks]
    k_swp = [pltpu.roll(k, HEAD_DIM, axis=1).astype(BF16) for k in ks]
    v_aug = [jnp.concatenate([v, jnp.ones_like(v)], axis=1).astype(BF16) for v in vs]
    outs = {}
    for keys, heads in ((k_nat, [h for h in range(N_HEADS) if h % 2 == h // Q_PER_KV]),
                        (k_swp, [h for h in range(N_HEADS) if h % 2 != h // Q_PER_KV])):
        rows =[jnp.where(low if h % 2 == 0 else jnp.logical_not(low), q[:, (h // 2) * LANE:(h // 2 + 1) * LANE], 0.0)
                for h in heads]
        qs = jnp.concatenate(rows, axis=0).astype(BF16)
        sink_col = jnp.concatenate([jnp.full((t, 1), sink_ref[h], F32) for h in heads], axis=0)
        ss = []
        m = sink_col
        for k, mk in zip(keys, masks):
            s = lax.dot_general(qs, k, (((1,), (1,)), ((), ())), preferred_element_type=F32) * (HEAD_DIM ** -0.5)
            if mk is not None:
                s = jnp.where(mk, s, NEG_INF)
            ss.append(s)
            m = jnp.maximum(m, jnp.max(s, axis=-1, keepdims=True))
        acc = jnp.zeros((len(heads) * t, 2 * LANE), F32)
        for s, v in zip(ss, v_aug):
            acc = acc + jnp.dot(jnp.exp(s - m).astype(BF16), v, preferred_element_type=F32)
        o = acc[:, :LANE] / (acc[:, LANE:LANE + 1] + jnp.exp(sink_col - m))
        for j, h in enumerate(heads):
            outs[h] = o[j * t:(j + 1) * t]
    chunks = []
    for c in range(N_HEADS // 2):
        halves = []
        for h in (2 * c, 2 * c + 1):
            halves.append(outs[h] if h // Q_PER_KV == h % 2 else pltpu.roll(outs[h], HEAD_DIM, axis=1))
        chunks.append(jnp.where(low, halves[0], halves[1]))
    return jnp.concatenate(chunks, axis=1)


def _ctx_attn_kernel(q_ref, k_ref, v_ref, sink_ref, o_ref):
    o_ref[...] = _gqa(q_ref[...], [k_ref[...]], [v_ref[...]], [None], sink_ref)


def _context_attention(zm, sink):
    return pl.pallas_call(
        _ctx_attn_kernel,
        grid=(BATCH,),
        in_specs=[
            pl.BlockSpec((SEQ, ATTN_WIDTH), lambda b: (b, SPLIT_Q // ATTN_WIDTH)),
            pl.BlockSpec((SEQ, KV_WIDTH), lambda b: (b, SPLIT_K // KV_WIDTH)),
            pl.BlockSpec((SEQ, KV_WIDTH), lambda b: (b, SPLIT_V // KV_WIDTH)),
            pl.BlockSpec(memory_space=pltpu.SMEM),
        ],
        out_specs=pl.BlockSpec((SEQ, ATTN_WIDTH), lambda b: (b, 0)),
        out_shape=jax.ShapeDtypeStruct((N_PROMPT_TOK, ATTN_WIDTH), F32),
        compiler_params=_cparams("arbitrary"),
        name="ctx_attention",
    )(zm, zm, zm, sink)


def _latent_attn_kernel(q_ref, kl_ref, kc_ref, kr_ref, vl_ref, vc_ref, vr_ref, ck_ref, cv_ref, cos_ref, sin_ref,
                        sink_ref, o_ref):
    i = pl.program_id(1)
    nb = DEC_SEQ // BLOCK

    def table(ref, blk):
        return ref[pl.ds(pl.multiple_of(blk * BLOCK, BLOCK), BLOCK), :]

    left, right = jnp.maximum(i - 1, 0), jnp.minimum(i + 1, nb - 1)
    cos_q, sin_q = table(cos_ref, i), table(sin_ref, i)
    q = jnp.concatenate([_rope(q_ref[:, c:c + LANE], cos_q, sin_q) for c in range(0, ATTN_WIDTH, LANE)], axis=1)
    kw = jnp.concatenate([
        _rope(kl_ref[...], table(cos_ref, left), table(sin_ref, left)),
        _rope(kc_ref[...], cos_q, sin_q),
        _rope(kr_ref[...], table(cos_ref, right), table(sin_ref, right)),
    ], axis=0)
    vw = jnp.concatenate([vl_ref[...], vc_ref[...], vr_ref[...]], axis=0)

    m_rows = Q_PER_KV * BLOCK
    r = lax.broadcasted_iota(jnp.int32, (m_rows, 3 * BLOCK), 0) % BLOCK
    c = lax.broadcasted_iota(jnp.int32, (m_rows, 3 * BLOCK), 1)
    kpos = c + (i - 1) * BLOCK
    win = (kpos >= 0) & (kpos < DEC_SEQ) & (jnp.abs(r + BLOCK - c) <= WINDOW)

    o_ref[...] = _gqa(q, [ck_ref[...], kw], [cv_ref[...], vw], [None, win], sink_ref)


def _latent_attention(zm, cache_k, cache_v, layer, cos, sin, sink):
    nb = DEC_SEQ // BLOCK
    blk0 = N_PROMPT_TOK // BLOCK
    kcol, vcol = SPLIT_K // KV_WIDTH, SPLIT_V // KV_WIDTH

    def row(b, i):
        return blk0 + b * nb + i

    def win_specs(col):
        return [
            pl.BlockSpec((BLOCK, KV_WIDTH), lambda b, i: (row(b, jnp.maximum(i - 1, 0)), col)),
            pl.BlockSpec((BLOCK, KV_WIDTH), lambda b, i: (row(b, i), col)),
            pl.BlockSpec((BLOCK, KV_WIDTH), lambda b, i: (row(b, jnp.minimum(i + 1, nb - 1)), col)),
        ]

    ctx_spec = pl.BlockSpec((None, None, PAST_LEN, KV_WIDTH), lambda b, i: (b, layer, 0, 0))
    tab_spec = pl.BlockSpec((DEC_SEQ, LANE), lambda b, i: (0, 0))
    return pl.pallas_call(
        _latent_attn_kernel,
        grid=(DEC_BATCH, nb),
        in_specs=[pl.BlockSpec((BLOCK, ATTN_WIDTH), lambda b, i: (row(b, i), SPLIT_Q // ATTN_WIDTH))]
        + win_specs(kcol) + win_specs(vcol)
        + [ctx_spec, ctx_spec, tab_spec, tab_spec, pl.BlockSpec(memory_space=pltpu.SMEM)],
        out_specs=pl.BlockSpec((BLOCK, ATTN_WIDTH), lambda b, i: (b * nb + i, 0)),
        out_shape=jax.ShapeDtypeStruct((N_SAMPLE_TOK, ATTN_WIDTH), F32),
        compiler_params=_cparams("arbitrary", "arbitrary"),
        name="latent_attention",
    )(zm, zm, zm, zm, zm, zm, zm, cache_k, cache_v, cos, sin, sink)


def kernel(x_prompt, x_sample, cache_k, cache_v, state_s5_re, state_s5_im, c, c_ctx, ada_w, ada_b, norm1_g, w_in, hy_conv_w, hy_conv_b, hy_pos_w1, hy_pos_b1, hy_pos_w2, hy_pos_b2, hy_pos_w3, hy_sin_freq, hy_decay, hy_skip, s5_lam_re, s5_lam_im, s5_log_step, s5_b_re, s5_b_im, s5_c_re, s5_c_im, s5_skip, s5_glu_w, s5_glu_b, attn_sink, proj_a, proj_b, proj_c, w_out, norm2_g, ffn_w_gate, ffn_w_up, ffn_w_down, final_norm_g):
    cvec = jnp.concatenate([c_ctx[None], c, jnp.zeros((MOD_ROWS - 1 - DEC_BATCH, D_MODEL), F32)], axis=0)
    mod = _modulation(cvec, ada_w, ada_b).reshape(DEPTH * MOD_ROWS * 6, 1, D_MODEL)

    x = jnp.concatenate([x_prompt.reshape(N_PROMPT_TOK, D_MODEL), x_sample.reshape(N_SAMPLE_TOK, D_MODEL)], axis=0)
    w_main = w_in[:, :, :MAIN_IN].astype(BF16)
    w_gate = w_in[:, :, MAIN_IN:].astype(BF16)
    pa, pb, pc, wo = proj_a.astype(BF16), proj_b.astype(BF16), proj_c.astype(BF16), w_out.astype(BF16)
    wg, wu, wd = ffn_w_gate.astype(BF16), ffn_w_up.astype(BF16), ffn_w_down.astype(BF16)
    g1, g2 = norm1_g.reshape(DEPTH, 1, D_MODEL), norm2_g.reshape(DEPTH, 1, D_MODEL)
    glu_w = s5_glu_w.astype(BF16)

    ctx_k = cache_k.reshape(DEC_BATCH, DEPTH, PAST_LEN, KV_WIDTH)
    ctx_v = cache_v.reshape(DEC_BATCH, DEPTH, PAST_LEN, KV_WIDTH)
    rope_cos, rope_sin = _rope_tables(DEC_SEQ)
    mats_p, mats_s = _dft_mats(min(SEQ, HY_BLOCK)), _dft_mats(min(DEC_SEQ, HY_BLOCK))

    ks_out, vs_out, sre_out, sim_out = [], [], [], []
    for l in range(DEPTH):
        filt = (hy_pos_w1[l], hy_pos_b1[l], hy_pos_w2[l], hy_pos_b2[l], hy_pos_w3[l], hy_sin_freq[l], hy_decay[l])
        zm = _in_proj(x, mod, l, g1, w_main)
        u, vb = _short_conv(zm, hy_conv_w[l], hy_conv_b[l].reshape(1, HY_IN))
        ya_p = _hyena_path(u, vb, 0, BATCH, SEQ, mats_p, _hyena_tables(SEQ, mats_p, filt), hy_skip[l])
        ya_s = _hyena_path(u, vb, N_PROMPT_TOK, DEC_BATCH, DEC_SEQ, mats_s, _hyena_tables(DEC_SEQ, mats_s, filt),
                           hy_skip[l])
        k_l = zm[:N_PROMPT_TOK, SPLIT_K:SPLIT_V].reshape(BATCH, SEQ, N_KV_HEADS, HEAD_DIM)
        v_l = zm[:N_PROMPT_TOK, SPLIT_V:].reshape(BATCH, SEQ, N_KV_HEADS, HEAD_DIM)
        yc_p = _context_attention(zm, attn_sink[l])
        yc_s = _latent_attention(zm, ctx_k, ctx_v, l, rope_cos, rope_sin, attn_sink[l])

        lam, bd, cd = _s5_params(s5_lam_re[l], s5_lam_im[l], s5_log_step[l], s5_b_re[l], s5_b_im[l],
                                 s5_c_re[l], s5_c_im[l])
        s5_w = (lam, bd, cd, s5_skip[l].reshape(1, S5_WIDTH), glu_w[l], s5_glu_b[l].reshape(1, 2 * S5_WIDTH))
        h0_s = jnp.stack([state_s5_re[:, l, 0], state_s5_im[:, l, 0], state_s5_re[:, l, 1], state_s5_im[:, l, 1]],
                         axis=1).reshape(DEC_BATCH, 4, S5_LANES)
        yb_p, fin_p = _s5_branch(zm, 0, BATCH, SEQ, SEQ, jnp.zeros((BATCH, 4, S5_LANES), F32), *s5_w)
        yb_s, _ = _s5_branch(zm, N_PROMPT_TOK, DEC_BATCH, DEC_SEQ, 512, h0_s, *s5_w)
        fin_p = fin_p.reshape(BATCH, 2, 2, S5_GROUPS, S5_STATE)

        ks_out.append(k_l)
        vs_out.append(v_l)
        sre_out.append(fin_p[:, :, 0])
        sim_out.append(fin_p[:, :, 1])
        x = _merge((ya_p, ya_s), (yb_p, yb_s), (yc_p, yc_s), x, mod, l, g1, w_gate, pa, pb, pc, wo)
        x = _ffn(x, mod, l, g2, wg, wu, wd)

    y_prompt, y_sample = _final_norm(x, final_norm_g.reshape(1, D_MODEL))
    return (y_prompt.reshape(BATCH, SEQ, D_MODEL), y_sample.reshape(DEC_BATCH, DEC_SEQ, D_MODEL),
            jnp.stack(ks_out, axis=1), jnp.stack(vs_out, axis=1), jnp.stack(sre_out, axis=1), jnp.stack(sim_out, axis=1))
```

```python
import functools
import math

import jax
import jax.numpy as jnp
import numpy as np
from jax import lax
from jax.experimental import pallas as pl
from jax.experimental.pallas import tpu as pltpu

D_MODEL = 1024
BATCH = 16
SEQ = 256
DEPTH = 4
DEC_BATCH = 2
DEC_SEQ = 4096
PAST_LEN = 512
GRID_W = 64
N_BRANCH = 3
HY_WIDTH = 256
HY_ORDER = 2
HY_SHORT_K = 3
HY_POS_EMB = 33
HY_POS_BANDS = (HY_POS_EMB - 1) // 2
HY_FILTER_HIDDEN = 64
S5_WIDTH = 256
S5_GROUP = 16
S5_GROUPS = S5_WIDTH // S5_GROUP
S5_STATE = 64
N_HEADS = 8
N_KV_HEADS = 2
Q_PER_KV = N_HEADS // N_KV_HEADS
HEAD_DIM = 64
ATTN_WIDTH = N_HEADS * HEAD_DIM
KV_WIDTH = N_KV_HEADS * HEAD_DIM
WINDOW = 128
BLOCK = 128
ROPE_BASE = 10000.0
FFN_HIDDEN = ((8 * D_MODEL // 3 + 255) // 256) * 256
HY_IN = (HY_ORDER + 1) * HY_WIDTH
GATE_IN = N_BRANCH * D_MODEL
MAIN_IN = HY_IN + S5_WIDTH + ATTN_WIDTH + 2 * KV_WIDTH
SPLIT_S5 = HY_IN
SPLIT_Q = SPLIT_S5 + S5_WIDTH
SPLIT_K = SPLIT_Q + ATTN_WIDTH
SPLIT_V = SPLIT_K + KV_WIDTH

F32 = jnp.float32
BF16 = jnp.bfloat16
EPS = 1e-6
NEG_INF = -1e30

N_PROMPT_TOK = BATCH * SEQ
N_SAMPLE_TOK = DEC_BATCH * DEC_SEQ
N_TOK = N_PROMPT_TOK + N_SAMPLE_TOK
LANE = 128
MOD_ROWS = 8
TM = 512
VMEM_LIMIT = 56 * 1024 * 1024


def _cparams(*sem):
    return pltpu.CompilerParams(dimension_semantics=sem, vmem_limit_bytes=VMEM_LIMIT)


def _mod_row(i):
    n_p = N_PROMPT_TOK // TM
    per_b = DEC_SEQ // TM
    return jnp.where(i < n_p, 0, 1 + (i - n_p) // per_b)


def _mod_spec(layer, k):
    return pl.BlockSpec((None, 1, D_MODEL), lambda i: ((layer * MOD_ROWS + _mod_row(i)) * 6 + k, 0, 0))


def _resident(shape, layer=None):
    if layer is None:
        return pl.BlockSpec(shape, lambda i: (0,) * len(shape), pipeline_mode=pl.Buffered(1))
    return pl.BlockSpec((None,) + shape, lambda i: (layer,) + (0,) * len(shape), pipeline_mode=pl.Buffered(1))


def _rms(x, g):
    return x * lax.rsqrt(jnp.mean(x * x, axis=-1, keepdims=True) + EPS) * g


def _mod_kernel(c_ref, w_ref, b_ref, o_ref):
    c = c_ref[...]
    s = (c * jax.nn.sigmoid(c)).astype(BF16)
    o_ref[...] = jnp.dot(s, w_ref[...].astype(BF16), preferred_element_type=F32) + b_ref[...]


def _modulation(cvec, ada_w, ada_b):
    tn = 1536
    return pl.pallas_call(
        _mod_kernel,
        grid=(DEPTH, 6 * D_MODEL // tn),
        in_specs=[
            pl.BlockSpec((MOD_ROWS, D_MODEL), lambda l, j: (0, 0)),
            pl.BlockSpec((None, D_MODEL, tn), lambda l, j: (l, 0, j)),
            pl.BlockSpec((None, 1, tn), lambda l, j: (l, 0, j)),
        ],
        out_specs=pl.BlockSpec((None, MOD_ROWS, tn), lambda l, j: (l, 0, j)),
        out_shape=jax.ShapeDtypeStruct((DEPTH, MOD_ROWS, 6 * D_MODEL), F32),
        compiler_params=_cparams("arbitrary", "arbitrary"),
        name="adaln_mod",
    )(cvec, ada_w, ada_b.reshape(DEPTH, 1, 6 * D_MODEL))


def _inproj_kernel(x_ref, sh_ref, sc_ref, g_ref, wm_ref, zm_ref):
    h = (_rms(x_ref[...], g_ref[...]) * (1.0 + sc_ref[...]) + sh_ref[...]).astype(BF16)
    zm_ref[...] = jnp.dot(h, wm_ref[...], preferred_element_type=F32)


def _in_proj(x, mod, layer, g, w_main):
    return pl.pallas_call(
        _inproj_kernel,
        grid=(N_TOK // TM,),
        in_specs=[
            pl.BlockSpec((TM, D_MODEL), lambda i: (i, 0)),
            _mod_spec(layer, 0),
            _mod_spec(layer, 1),
            _resident((1, D_MODEL), layer),
            _resident((D_MODEL, MAIN_IN), layer),
        ],
        out_specs=pl.BlockSpec((TM, MAIN_IN), lambda i: (i, 0)),
        out_shape=jax.ShapeDtypeStruct((N_TOK, MAIN_IN), F32),
        compiler_params=_cparams("arbitrary"),
        name="in_proj",
    )(x, mod, mod, g, w_main)


N_PROMPT_TILES = N_PROMPT_TOK // TM


def _path_specs(width):
    return [pl.BlockSpec((TM, width), lambda i: (jnp.minimum(i, N_PROMPT_TILES - 1), 0)),
            pl.BlockSpec((TM, width), lambda i: (jnp.maximum(i - N_PROMPT_TILES, 0), 0))]


def _merge_kernel(yap_ref, yas_ref, ybp_ref, ybs_ref, ycp_ref, ycs_ref, x_ref, sh_ref, sc_ref, g1_ref, g_ref,
                  wg_ref, pa_ref, pb_ref, pc_ref, wo_ref, o_ref):
    is_prompt = pl.program_id(0) < N_PROMPT_TILES
    x = x_ref[...]
    h = (_rms(x, g_ref[...]) * (1.0 + sc_ref[...]) + sh_ref[...]).astype(BF16)
    branches = ((yap_ref, yas_ref, pa_ref), (ybp_ref, ybs_ref, pb_ref), (ycp_ref, ycs_ref, pc_ref))
    m = jnp.zeros((TM, D_MODEL), F32)
    for k, (p_ref, s_ref, w_ref) in enumerate(branches):
        y = jnp.where(is_prompt, p_ref[...], s_ref[...]).astype(BF16)
        gate = jax.nn.sigmoid(jnp.dot(h, wg_ref[:, k * D_MODEL:(k + 1) * D_MODEL], preferred_element_type=F32))
        m = m + gate * jnp.dot(y, w_ref[...], preferred_element_type=F32)
    o_ref[...] = x + g1_ref[...] * jnp.dot(m.astype(BF16), wo_ref[...], preferred_element_type=F32)


def _merge(ya, yb, yc, x, mod, layer, g, w_gate, pa, pb, pc, wo):
    return pl.pallas_call(
        _merge_kernel,
        grid=(N_TOK // TM,),
        in_specs=_path_specs(HY_WIDTH) + _path_specs(S5_WIDTH) + _path_specs(ATTN_WIDTH) + [
            pl.BlockSpec((TM, D_MODEL), lambda i: (i, 0)),
            _mod_spec(layer, 0),
            _mod_spec(layer, 1),
            _mod_spec(layer, 2),
            _resident((1, D_MODEL), layer),
            _resident((D_MODEL, GATE_IN), layer),
            _resident((HY_WIDTH, D_MODEL), layer),
            _resident((S5_WIDTH, D_MODEL), layer),
            _resident((ATTN_WIDTH, D_MODEL), layer),
            _resident((D_MODEL, D_MODEL), layer),
        ],
        out_specs=pl.BlockSpec((TM, D_MODEL), lambda i: (i, 0)),
        out_shape=jax.ShapeDtypeStruct((N_TOK, D_MODEL), F32),
        compiler_params=_cparams("arbitrary"),
        name="merge_out",
    )(*ya, *yb, *yc, x, mod, mod, mod, g, w_gate, pa, pb, pc, wo)


def _ffn_kernel(x_ref, sh_ref, sc_ref, gt_ref, g_ref, wg_ref, wu_ref, wd_ref, o_ref):
    x = x_ref[...]
    h = (_rms(x, g_ref[...]) * (1.0 + sc_ref[...]) + sh_ref[...]).astype(BF16)
    a = jnp.dot(h, wg_ref[...], preferred_element_type=F32)
    b = jnp.dot(h, wu_ref[...], preferred_element_type=F32)
    f = (a * jax.nn.sigmoid(a) * b).astype(BF16)
    o_ref[...] = x + gt_ref[...] * jnp.dot(f, wd_ref[...], preferred_element_type=F32)


def _ffn(x, mod, layer, g, wg, wu, wd):
    return pl.pallas_call(
        _ffn_kernel,
        grid=(N_TOK // TM,),
        in_specs=[
            pl.BlockSpec((TM, D_MODEL), lambda i: (i, 0)),
            _mod_spec(layer, 3),
            _mod_spec(layer, 4),
            _mod_spec(layer, 5),
            _resident((1, D_MODEL), layer),
            _resident((D_MODEL, FFN_HIDDEN), layer),
            _resident((D_MODEL, FFN_HIDDEN), layer),
            _resident((FFN_HIDDEN, D_MODEL), layer),
        ],
        out_specs=pl.BlockSpec((TM, D_MODEL), lambda i: (i, 0)),
        out_shape=jax.ShapeDtypeStruct((N_TOK, D_MODEL), F32),
        compiler_params=_cparams("arbitrary"),
        name="ffn",
    )(x, mod, mod, mod, g, wg, wu, wd)


def _final_norm_kernel(x_ref, g_ref, op_ref, os_ref):
    y = _rms(x_ref[...], g_ref[...])
    is_prompt = pl.program_id(0) < N_PROMPT_TILES

    @pl.when(is_prompt)
    def _():
        op_ref[...] = y

    @pl.when(jnp.logical_not(is_prompt))
    def _():
        os_ref[...] = y


def _final_norm(x, g):
    return pl.pallas_call(
        _final_norm_kernel,
        grid=(N_TOK // TM,),
        in_specs=[pl.BlockSpec((TM, D_MODEL), lambda i: (i, 0)), _resident((1, D_MODEL))],
        out_specs=_path_specs(D_MODEL),
        out_shape=[jax.ShapeDtypeStruct((N_PROMPT_TOK, D_MODEL), F32),
                   jax.ShapeDtypeStruct((N_SAMPLE_TOK, D_MODEL), F32)],
        compiler_params=_cparams("arbitrary"),
        name="final_norm",
    )(x, g)


SUBLANE = 8
HY_TILE = 256
HY_BLOCK = 512
HY_ROWS = 64


def _dft_mats(blk):
    n = 2 * blk
    idx = jnp.arange(blk, dtype=jnp.int32)
    ang = (2.0 * math.pi / n) * ((idx[:, None] * idx[None, :]) % n).astype(F32)
    alt = jnp.where(idx % 2 == 0, 1.0, -1.0).astype(F32)
    return jnp.stack([jnp.cos(ang), jnp.where((idx == 0)[:, None], alt[None, :], -jnp.sin(ang))]).astype(BF16)


def _short_conv_kernel(x_ref, xp_ref, xn_ref, w_ref, b_ref, u_ref, vb_ref):
    i = pl.program_id(0)
    n_p = N_PROMPT_TOK // HY_TILE
    per_p, per_s = SEQ // HY_TILE, DEC_SEQ // HY_TILE
    idx = jnp.where(i < n_p, i % per_p, (i - n_p) % per_s)
    per = jnp.where(i < n_p, per_p, per_s)
    has_prev = (idx > 0).astype(F32)
    has_next = (idx < per - 1).astype(F32)
    x = x_ref[...]
    row = lax.broadcasted_iota(jnp.int32, (HY_TILE, 1), 0)
    x_prev = jnp.where(row == 0, xp_ref[SUBLANE - 1:SUBLANE, :] * has_prev, pltpu.roll(x, 1, axis=0))
    x_next = jnp.where(row == HY_TILE - 1, xn_ref[0:1, :] * has_next, pltpu.roll(x, HY_TILE - 1, axis=0))
    u = b_ref[...] + x_prev * w_ref[0:1, :] + x * w_ref[1:2, :] + x_next * w_ref[2:3, :]
    u_ref[...] = u
    vb_ref[...] = u[:, :HY_WIDTH].astype(BF16)


def _short_conv(zm, w, b):
    per8 = HY_TILE // SUBLANE
    last8 = N_TOK // SUBLANE - 1
    return pl.pallas_call(
        _short_conv_kernel,
        grid=(N_TOK // HY_TILE,),
        in_specs=[
            pl.BlockSpec((HY_TILE, HY_IN), lambda i: (i, 0)),
            pl.BlockSpec((SUBLANE, HY_IN), lambda i: (jnp.maximum(i * per8 - 1, 0), 0)),
            pl.BlockSpec((SUBLANE, HY_IN), lambda i: (jnp.minimum((i + 1) * per8, last8), 0)),
            pl.BlockSpec((HY_SHORT_K, HY_IN), lambda i: (0, 0)),
            pl.BlockSpec((1, HY_IN), lambda i: (0, 0)),
        ],
        out_specs=[
            pl.BlockSpec((HY_TILE, HY_IN), lambda i: (i, 0)),
            pl.BlockSpec((HY_TILE, HY_WIDTH), lambda i: (i, 0)),
        ],
        out_shape=[
            jax.ShapeDtypeStruct((N_TOK, HY_IN), F32),
            jax.ShapeDtypeStruct((N_TOK, HY_WIDTH), BF16),
        ],
        compiler_params=_cparams("arbitrary"),
        name="hyena_short_conv",
    )(zm, zm, zm, w, b)


def _block_conv_kernel(n_blk, last, zin_ref, k_ref, f_ref, g_ref, zp_ref, skip_ref, o_ref, *rest):
    z_ref, s_ref = rest[-2:]
    n = pl.program_id(1)
    blk = f_ref.shape[-1]

    @pl.when(n == 0)
    def _spectra():
        group = min(n_blk, 8)
        for g0 in range(0, n_blk, group):
            z = jnp.concatenate([zin_ref[(g0 + g) * blk:(g0 + g + 1) * blk, :] for g in range(group)], axis=1)
            for c in range(2):
                x = jnp.dot(f_ref[c], z, preferred_element_type=F32)
                for g in range(group):
                    z_ref[g0 + g, c] = x[:, g * HY_WIDTH:(g + 1) * HY_WIDTH]

    def product(r, first_row_packed):
        acc_r = acc_i = jnp.zeros((r.size, HY_WIDTH), F32)
        fix_r = fix_i = acc_r
        for i in range(n_blk):
            d = n - i + n_blk - 1
            zr, zi = z_ref[i, 0, r, :], z_ref[i, 1, r, :]
            kr, ki = k_ref[d, 0, r, :], k_ref[d, 1, r, :]
            acc_r = acc_r + (zr * kr - zi * ki)
            acc_i = acc_i + (zr * ki + zi * kr)
            if first_row_packed:
                fix_r = fix_r + zr * kr
                fix_i = fix_i + zi * ki
        if first_row_packed:
            row0 = lax.broadcasted_iota(jnp.int32, (r.size, 1), 0) == 0
            acc_r = jnp.where(row0, fix_r, acc_r)
            acc_i = jnp.where(row0, fix_i, acc_i)
        s_ref[0, r, :] = acc_r.astype(BF16)
        s_ref[1, r, :] = acc_i.astype(BF16)

    def chunk(c, carry):
        product(pl.ds(pl.multiple_of(c * HY_ROWS, HY_ROWS), HY_ROWS), False)
        return carry

    lax.fori_loop(0, blk // HY_ROWS, chunk, 0)
    product(pl.ds(0, 2 * SUBLANE), True)

    conv_c = jnp.dot(f_ref[0], s_ref[0], preferred_element_type=F32)
    conv_s = jnp.dot(f_ref[1], s_ref[1], preferred_element_type=F32)
    t = lax.broadcasted_iota(jnp.int32, (blk, 1), 0)
    alt = jnp.where(t % 2 == 0, 1.0, -1.0).astype(F32)
    conv = conv_c + jnp.where(t == 0, 0.0, conv_s) + alt * s_ref[1, 0:1, :].astype(F32)
    z = g_ref[...] * (conv + skip_ref[...] * zp_ref[...])
    o_ref[...] = z
    if not last:
        rest[0][...] = z.astype(BF16)


def _block_conv(zb, zb_row0, tables, mats, u, gate_col, z_prev, z_prev_row0, skip, row0, n_seq, seq_len, last):
    blk = mats.shape[-1]
    n_blk = seq_len // blk

    def rows(base):
        return lambda b, n: (base // blk + b * n_blk + n, 0)

    out_spec = pl.BlockSpec((blk, HY_WIDTH), rows(0))
    out_shape = [jax.ShapeDtypeStruct((n_seq * n_blk * blk, HY_WIDTH), F32)]
    if not last:
        out_shape.append(jax.ShapeDtypeStruct((n_seq * n_blk * blk, HY_WIDTH), BF16))
    return pl.pallas_call(
        functools.partial(_block_conv_kernel, n_blk, last),
        grid=(n_seq, n_blk),
        in_specs=[
            pl.BlockSpec((seq_len, HY_WIDTH), lambda b, n: (zb_row0 // seq_len + b, 0)),
            pl.BlockSpec(tables.shape, lambda b, n: (0, 0, 0, 0), pipeline_mode=pl.Buffered(1)),
            pl.BlockSpec((2, blk, blk), lambda b, n: (0, 0, 0)),
            pl.BlockSpec((blk, HY_WIDTH), lambda b, n: (row0 // blk + b * n_blk + n, gate_col)),
            pl.BlockSpec((blk, HY_WIDTH), rows(z_prev_row0)),
            pl.BlockSpec((1, HY_WIDTH), lambda b, n: (0, 0)),
        ],
        out_specs=[out_spec] * len(out_shape),
        out_shape=out_shape,
        scratch_shapes=[pltpu.VMEM((n_blk, 2, blk, HY_WIDTH), F32), pltpu.VMEM((2, blk, HY_WIDTH), BF16)],
        compiler_params=_cparams("arbitrary", "arbitrary"),
        name="hyena_block_conv",
    )(zb, tables, mats, u, z_prev, skip)


def _taps_kernel(seq_len, rows, w1_ref, b1_ref, w2_ref, b2_ref, w3_ref, freq_ref, decay_ref, k_ref, norm_ref):
    i = pl.program_id(0)
    lag = i * rows + lax.broadcasted_iota(jnp.int32, (rows, 1), 0) - seq_len
    t = jnp.abs(lag).astype(F32) * (1.0 / (seq_len - 1))
    pos = jnp.abs(i * rows + lax.broadcasted_iota(jnp.int32, (1, rows), 1) - seq_len).astype(F32)
    band = lax.broadcasted_iota(jnp.int32, (HY_POS_BANDS, 1), 0).astype(F32)
    band = 1e-4 + band * ((HY_POS_BANDS - 1 - 1e-4) / (HY_POS_BANDS - 1))
    ang = band * (pos * (2.0 * math.pi / seq_len))
    dot = functools.partial(jnp.dot, precision=lax.Precision.HIGHEST, preferred_element_type=F32)
    pre = w1_ref[:, 0:1] * (pos * (1.0 / (seq_len - 1)))
    pre = pre + dot(w1_ref[:, 1:1 + HY_POS_BANDS], jnp.cos(ang))
    pre = pre - dot(w1_ref[:, 1 + HY_POS_BANDS:], jnp.sin(ang))
    freq = freq_ref[...]
    h = jnp.sin(freq * (pre + b1_ref[...]))
    h = jnp.sin(freq * (dot(w2_ref[...], h) + b2_ref[...]))
    h = lax.dot_general(h.astype(BF16), w3_ref[...].astype(BF16), (((0,), (0,)), ((), ())),
                        preferred_element_type=F32)
    h = h * jnp.exp(-t * jnp.abs(decay_ref[...]))

    @pl.when(i == 0)
    def _():
        norm_ref[...] = jnp.zeros_like(norm_ref)

    for o in range(HY_ORDER):
        c0 = o * 2 * HY_WIDTH
        k = jnp.where(lag >= 0, h[:, c0:c0 + HY_WIDTH], h[:, c0 + HY_WIDTH:c0 + 2 * HY_WIDTH])
        k = jnp.where(lag == -seq_len, 0.0, k)
        k_ref[o] = k.astype(BF16)
        norm_ref[:, o * HY_WIDTH:(o + 1) * HY_WIDTH] += jnp.sum(jnp.abs(k), axis=0, keepdims=True)


def _hyena_taps(seq_len, w1, b1, w2, b2, w3, freq, decay):
    rows = min(2 * seq_len, 512)
    full = lambda a: pl.BlockSpec(a.shape, lambda i: (0,) * a.ndim)
    args = (w1.T, b1.reshape(-1, 1), w2.T, b2.reshape(-1, 1), w3, freq.reshape(-1, 1), decay.reshape(1, -1))
    return pl.pallas_call(
        functools.partial(_taps_kernel, seq_len, rows),
        grid=(2 * seq_len // rows,),
        in_specs=[full(a) for a in args],
        out_specs=[pl.BlockSpec((HY_ORDER, rows, HY_WIDTH), lambda i: (0, i, 0)),
                   pl.BlockSpec((1, HY_ORDER * HY_WIDTH), lambda i: (0, 0))],
        out_shape=[jax.ShapeDtypeStruct((HY_ORDER, 2 * seq_len, HY_WIDTH), BF16),
                   jax.ShapeDtypeStruct((1, HY_ORDER * HY_WIDTH), F32)],
        compiler_params=_cparams("arbitrary"),
        name="hyena_taps",
    )(*args)


def _tables_kernel(x_ref, norm_ref, f_ref, o_ref, prev_ref):
    q = pl.program_id(1)
    blk = f_ref.shape[-1]
    f = lax.broadcasted_iota(jnp.int32, (blk, 1), 0)
    alt = jnp.where(f % 2 == 0, 1.0, -1.0).astype(F32)
    scale = jnp.where(f == 0, 1.0, 2.0).astype(F32) * (1.0 / (2 * blk)) / norm_ref[...]
    x = x_ref[...]
    for c in range(2):
        xc = jnp.dot(f_ref[c], x, preferred_element_type=F32)

        @pl.when(q > 0)
        def _(c=c, xc=xc):
            o_ref[c] = (xc + alt * prev_ref[c]) * scale

        prev_ref[c] = xc


def _hyena_tables(seq_len, mats, filt):
    blk = mats.shape[-1]
    n_lag = 2 * seq_len // blk
    taps, norm = _hyena_taps(seq_len, *filt)
    return pl.pallas_call(
        _tables_kernel,
        grid=(HY_ORDER, n_lag),
        in_specs=[
            pl.BlockSpec((None, blk, HY_WIDTH), lambda o, q: (o, q, 0)),
            pl.BlockSpec((1, HY_WIDTH), lambda o, q: (0, o)),
            pl.BlockSpec((2, blk, blk), lambda o, q: (0, 0, 0)),
        ],
        out_specs=pl.BlockSpec((None, None, 2, blk, HY_WIDTH), lambda o, q: (o, jnp.maximum(q - 1, 0), 0, 0, 0)),
        out_shape=jax.ShapeDtypeStruct((HY_ORDER, n_lag - 1, 2, blk, HY_WIDTH), F32),
        scratch_shapes=[pltpu.VMEM((2, blk, HY_WIDTH), F32)],
        compiler_params=_cparams("arbitrary", "arbitrary"),
        name="hyena_tables",
    )(taps, norm, mats)


def _hyena_path(u, vb, row0, n_seq, seq_len, mats, tables, skip):
    z1, z1b = _block_conv(vb, row0, tables[0], mats, u, 1, u, row0, skip[0:1], row0, n_seq, seq_len, False)
    (ya,) = _block_conv(z1b, 0, tables[1], mats, u, 2, z1, 0, skip[1:2], row0, n_seq, seq_len, True)
    return ya


S5_LANES = S5_GROUPS * S5_STATE
S5_SEGS = 8


def _s5_kernel(n_tiles, tile, u0_ref, u1_ref, h0_ref, lam_ref, bd_ref, cd_ref, skip_ref, gw_ref, gb_ref,
               y_ref, fin_ref, bu_ref, ybwd_ref, p_ref, p8_ref, cm_ref, fl_ref, carry_ref, up_ref, st0_ref, st1_ref):
    seg = tile // S5_SEGS
    n = S5_LANES
    b = pl.program_id(0)
    j = pl.program_id(1)

    @pl.when((b == 0) & (j == 0))
    def _powers():
        for d in range(2):
            lr = lam_ref[2 * d:2 * d + 1, :]
            li = lam_ref[2 * d + 1:2 * d + 2, :]

            def put(k, pr, pi, d=d):
                p_ref[d, pl.ds(k, 1), 0:n] = pr
                p_ref[d, pl.ds(k, 1), n:2 * n] = pi
                r = pl.ds(pl.multiple_of(k * S5_SEGS, S5_SEGS), S5_SEGS)
                p8_ref[d, r, 0:n] = jnp.broadcast_to(pr, (S5_SEGS, n))
                p8_ref[d, r, n:2 * n] = jnp.broadcast_to(pi, (S5_SEGS, n))

            put(0, lr, li)

            def body(k, c, lr=lr, li=li, put=put):
                pr, pi = c
                nr = pr * lr - pi * li
                ni = pr * li + pi * lr
                put(k, nr, ni)
                return nr, ni

            lax.fori_loop(1, seg, body, (lr, li))

    def sweep(d, t):
        first, last = (0, n_tiles - 1) if d == 0 else (n_tiles - 1, 0)

        @pl.when(t == first)
        def _init():
            carry_ref[0:1, 0:n] = h0_ref[2 * d:2 * d + 1, :]
            carry_ref[0:1, n:2 * n] = h0_ref[2 * d + 1:2 * d + 2, :]

        def gather(i, c):
            r = pl.ds(pl.multiple_of(i * S5_SEGS, S5_SEGS), S5_SEGS)
            up_ref[r, 0:LANE] = u0_ref[pl.ds(i, S5_SEGS, stride=seg), :]
            up_ref[r, LANE:2 * LANE] = u1_ref[pl.ds(i, S5_SEGS, stride=seg), :]
            return c

        lax.fori_loop(0, seg, gather, 0, unroll=2)
        up = up_ref[...]
        ub = up.astype(BF16)
        for c0 in range(0, 2 * n, 512):
            bu_ref[:, c0:c0 + 512] = jnp.dot(ub, bd_ref[d, :, c0:c0 + 512], preferred_element_type=F32)

        lr8 = jnp.broadcast_to(lam_ref[2 * d:2 * d + 1, :], (S5_SEGS, n))
        li8 = jnp.broadcast_to(lam_ref[2 * d + 1:2 * d + 2, :], (S5_SEGS, n))

        def rows(i):
            k = i if d == 0 else seg - 1 - i
            return pl.ds(pl.multiple_of(k * S5_SEGS, S5_SEGS), S5_SEGS)

        def local(i, c):
            hr, hi = c
            r = rows(i)
            nr = lr8 * hr - li8 * hi + bu_ref[r, 0:n]
            ni = lr8 * hi + li8 * hr + bu_ref[r, n:2 * n]
            bu_ref[r, 0:n] = nr
            bu_ref[r, n:2 * n] = ni
            return nr, ni

        zero = jnp.zeros((S5_SEGS, n), F32)
        fr, fi = lax.fori_loop(0, seg, local, (zero, zero))
        fl_ref[:, 0:n] = fr
        fl_ref[:, n:2 * n] = fi

        plr = p_ref[d, seg - 1:seg, 0:n]
        pli = p_ref[d, seg - 1:seg, n:2 * n]
        cr = carry_ref[0:1, 0:n]
        ci = carry_ref[0:1, n:2 * n]
        for s in (range(S5_SEGS) if d == 0 else reversed(range(S5_SEGS))):
            cm_ref[s:s + 1, 0:n] = cr
            cm_ref[s:s + 1, n:2 * n] = ci
            flr = fl_ref[s:s + 1, 0:n]
            fli = fl_ref[s:s + 1, n:2 * n]
            cr, ci = plr * cr - pli * ci + flr, plr * ci + pli * cr + fli
        carry_ref[0:1, 0:n] = cr
        carry_ref[0:1, n:2 * n] = ci

        @pl.when(t == last)
        def _final():
            fin_ref[2 * d:2 * d + 1, :] = cr
            fin_ref[2 * d + 1:2 * d + 2, :] = ci

        cmr = cm_ref[:, 0:n]
        cmi = cm_ref[:, n:2 * n]

        def fix(i, c):
            r = rows(i)
            pw = pl.ds(pl.multiple_of(i * S5_SEGS, S5_SEGS), S5_SEGS)
            pr = p8_ref[d, pw, 0:n]
            pi = p8_ref[d, pw, n:2 * n]
            bu_ref[r, 0:n] = bu_ref[r, 0:n] + (pr * cmr - pi * cmi)
            bu_ref[r, n:2 * n] = bu_ref[r, n:2 * n] + (pr * cmi + pi * cmr)
            return c

        lax.fori_loop(0, seg, fix, 0, unroll=2)

        y = jnp.dot(bu_ref[...].astype(BF16), cd_ref[d], preferred_element_type=F32)
        trow = pl.ds(pl.multiple_of(t * tile, tile), tile)
        if d == 1:
            ybwd_ref[trow, :] = y
        else:
            tot = up * skip_ref[...] + y + ybwd_ref[trow, :]
            g = jnp.dot(tot.astype(BF16), gw_ref[...], preferred_element_type=F32) + gb_ref[...]
            res = g[:, :S5_WIDTH] * jax.nn.sigmoid(g[:, S5_WIDTH:])
            st0_ref[...] = res[:, 0:LANE]
            st1_ref[...] = res[:, LANE:2 * LANE]
            per_seg = seg // S5_SEGS

            def unpermute(m, c):
                s = m // per_seg
                i0 = (m % per_seg) * S5_SEGS
                src = pl.ds(i0 * S5_SEGS + s, S5_SEGS, stride=S5_SEGS)
                dst = pl.ds(pl.multiple_of(m * S5_SEGS, S5_SEGS), S5_SEGS)
                y_ref[dst, 0:LANE] = st0_ref[src, :]
                y_ref[dst, LANE:2 * LANE] = st1_ref[src, :]
                return c

            lax.fori_loop(0, tile // S5_SEGS, unpermute, 0, unroll=2)

    @pl.when(j < n_tiles)
    def _bwd():
        sweep(1, n_tiles - 1 - j)

    @pl.when(j >= n_tiles)
    def _fwd():
        sweep(0, j - n_tiles)


def _s5_branch(zm, row0, n_seq, seq_len, tile, h0, lam, bd, cd, skip, gw, gb):
    n_tiles = seq_len // tile
    blk0 = row0 // tile
    col = SPLIT_S5 // LANE
    seg = tile // S5_SEGS

    def tile_of(j):
        return jnp.where(j < n_tiles, n_tiles - 1 - j, j - n_tiles)

    return pl.pallas_call(
        functools.partial(_s5_kernel, n_tiles, tile),
        grid=(n_seq, 2 * n_tiles),
        in_specs=[
            pl.BlockSpec((tile, LANE), lambda b, j: (blk0 + b * n_tiles + tile_of(j), col)),
            pl.BlockSpec((tile, LANE), lambda b, j: (blk0 + b * n_tiles + tile_of(j), col + 1)),
            pl.BlockSpec((None, 4, S5_LANES), lambda b, j: (b, 0, 0)),
            pl.BlockSpec((4, S5_LANES), lambda b, j: (0, 0)),
            pl.BlockSpec((2, S5_WIDTH, 2 * S5_LANES), lambda b, j: (0, 0, 0)),
            pl.BlockSpec((2, 2 * S5_LANES, S5_WIDTH), lambda b, j: (0, 0, 0)),
            pl.BlockSpec((1, S5_WIDTH), lambda b, j: (0, 0)),
            pl.BlockSpec((S5_WIDTH, 2 * S5_WIDTH), lambda b, j: (0, 0)),
            pl.BlockSpec((1, 2 * S5_WIDTH), lambda b, j: (0, 0)),
        ],
        out_specs=[
            pl.BlockSpec((tile, S5_WIDTH), lambda b, j: (b * n_tiles + jnp.maximum(j - n_tiles, 0), 0)),
            pl.BlockSpec((None, 4, S5_LANES), lambda b, j: (b, 0, 0)),
        ],
        out_shape=[
            jax.ShapeDtypeStruct((n_seq * seq_len, S5_WIDTH), F32),
            jax.ShapeDtypeStruct((n_seq, 4, S5_LANES), F32),
        ],
        scratch_shapes=[
            pltpu.VMEM((tile, 2 * S5_LANES), F32),
            pltpu.VMEM((seq_len, S5_WIDTH), F32),
            pltpu.VMEM((2, seg, 2 * S5_LANES), F32),
            pltpu.VMEM((2, tile, 2 * S5_LANES), F32),
            pltpu.VMEM((S5_SEGS, 2 * S5_LANES), F32),
            pltpu.VMEM((S5_SEGS, 2 * S5_LANES), F32),
            pltpu.VMEM((1, 2 * S5_LANES), F32),
            pltpu.VMEM((tile, S5_WIDTH), F32),
            pltpu.VMEM((tile, LANE), F32),
            pltpu.VMEM((tile, LANE), F32),
        ],
        compiler_params=_cparams("arbitrary", "arbitrary"),
        name="s5_scan",
    )(zm, zm, h0, lam, bd, cd, skip, gw, gb)


def _s5_params(lam_re, lam_im, log_step, b_re, b_im, c_re, c_im):
    step = jnp.exp(log_step)[..., None]
    mag = jnp.exp(lam_re * step)
    bar_re, bar_im = mag * jnp.cos(lam_im * step), mag * jnp.sin(lam_im * step)
    den = lam_re * lam_re + lam_im * lam_im
    q_re = ((bar_re - 1.0) * lam_re + bar_im * lam_im) / den
    q_im = (bar_im * lam_re - (bar_re - 1.0) * lam_im) / den
    bb_re = q_re[..., None] * b_re - q_im[..., None] * b_im
    bb_im = q_re[..., None] * b_im + q_im[..., None] * b_re
    eye = jnp.eye(S5_GROUPS, dtype=F32)

    def in_mat(x):
        xt = jnp.swapaxes(x, 2, 3)
        return (xt[:, :, :, None, :] * eye[None, :, None, :, None]).reshape(2, S5_WIDTH, S5_LANES)

    def out_mat(x):
        xt = jnp.swapaxes(x, 2, 3)
        return (xt[:, :, :, None, :] * eye[None, :, None, :, None]).reshape(2, S5_LANES, S5_WIDTH)

    bd = jnp.concatenate([in_mat(bb_re), in_mat(bb_im)], axis=2).astype(BF16)
    cd = jnp.concatenate([out_mat(c_re), out_mat(-c_im)], axis=1).astype(BF16)
    lam_rows = jnp.stack([bar_re[0], bar_im[0], bar_re[1], bar_im[1]])
    return lam_rows.reshape(4, S5_LANES), bd, cd


ROPE_HALF = HEAD_DIM // 4


def _rope_tables(seq_len):
    t = jnp.arange(seq_len)
    row = (t // GRID_W).astype(F32)
    col = (t % GRID_W).astype(F32)
    inv = ROPE_BASE ** (-jnp.arange(ROPE_HALF, dtype=F32) / ROPE_HALF)
    ang_r = row[:, None] * inv
    ang_c = col[:, None] * inv
    cos = jnp.concatenate([jnp.cos(ang_r), jnp.cos(ang_r), jnp.cos(ang_c), jnp.cos(ang_c)], axis=-1)
    sin = jnp.concatenate([-jnp.sin(ang_r), jnp.sin(ang_r), -jnp.sin(ang_c), jnp.sin(ang_c)], axis=-1)
    return jnp.tile(cos, (1, LANE // HEAD_DIM)), jnp.tile(sin, (1, LANE // HEAD_DIM))


def _rope(x, cos, sin):
    lane = lax.broadcasted_iota(jnp.int32, x.shape, 1)
    first = (lane % (2 * ROPE_HALF)) < ROPE_HALF
    partner = jnp.where(first, pltpu.roll(x, LANE - ROPE_HALF, axis=1), pltpu.roll(x, ROPE_HALF, axis=1))
    return x * cos + partner * sin


def _gqa(q, ks, vs, masks, sink_ref):
    t = q.shape[0]
    low = lax.broadcasted_iota(jnp.int32, (1, LANE), 1) < HEAD_DIM
    assert N_KV_HEADS * HEAD_DIM == LANE
    k_nat = [k.astype(BF16) for k in ks]
    k_swp = [pltpu.roll(k, HEAD_DIM, axis=1).astype(BF16) for k in ks]
    v_aug = [jnp.concatenate([v, jnp.ones_like(v)], axis=1).astype(BF16) for v in vs]
    outs = {}
    for keys, heads in ((k_nat, [h for h in range(N_HEADS) if h % 2 == h // Q_PER_KV]),
                        (k_swp, [h for h in range(N_HEADS) if h % 2 != h // Q_PER_KV])):
        rows =[jnp.where(low if h % 2 == 0 else jnp.logical_not(low), q[:, (h // 2) * LANE:(h // 2 + 1) * LANE], 0.0)
                for h in heads]
        qs = jnp.concatenate(rows, axis=0).astype(BF16)
        sink_col = jnp.concatenate([jnp.full((t, 1), sink_ref[h], F32) for h in heads], axis=0)
        ss = []
        m = sink_col
        for k, mk in zip(keys, masks):
            s = lax.dot_general(qs, k, (((1,), (1,)), ((), ())), preferred_element_type=F32) * (HEAD_DIM ** -0.5)
            if mk is not None:
                s = jnp.where(mk, s, NEG_INF)
            ss.append(s)
            m = jnp.maximum(m, jnp.max(s, axis=-1, keepdims=True))
        acc = jnp.zeros((len(heads) * t, 2 * LANE), F32)
        for s, v in zip(ss, v_aug):
            acc = acc + jnp.dot(jnp.exp(s - m).astype(BF16), v, preferred_element_type=F32)
        o = acc[:, :LANE] / (acc[:, LANE:LANE + 1] + jnp.exp(sink_col - m))
        for j, h in enumerate(heads):
            outs[h] = o[j * t:(j + 1) * t]
    chunks = []
    for c in range(N_HEADS // 2):
        halves = []
        for h in (2 * c, 2 * c + 1):
            halves.append(outs[h] if h // Q_PER_KV == h % 2 else pltpu.roll(outs[h], HEAD_DIM, axis=1))
        chunks.append(jnp.where(low, halves[0], halves[1]))
    return jnp.concatenate(chunks, axis=1)


def _ctx_attn_kernel(q_ref, k_ref, v_ref, sink_ref, o_ref):
    o_ref[...] = _gqa(q_ref[...], [k_ref[...]], [v_ref[...]], [None], sink_ref)


def _context_attention(zm, sink):
    return pl.pallas_call(
        _ctx_attn_kernel,
        grid=(BATCH,),
        in_specs=[
            pl.BlockSpec((SEQ, ATTN_WIDTH), lambda b: (b, SPLIT_Q // ATTN_WIDTH)),
            pl.BlockSpec((SEQ, KV_WIDTH), lambda b: (b, SPLIT_K // KV_WIDTH)),
            pl.BlockSpec((SEQ, KV_WIDTH), lambda b: (b, SPLIT_V // KV_WIDTH)),
            pl.BlockSpec(memory_space=pltpu.SMEM),
        ],
        out_specs=pl.BlockSpec((SEQ, ATTN_WIDTH), lambda b: (b, 0)),
        out_shape=jax.ShapeDtypeStruct((N_PROMPT_TOK, ATTN_WIDTH), F32),
        compiler_params=_cparams("arbitrary"),
        name="ctx_attention",
    )(zm, zm, zm, sink)


def _latent_attn_kernel(q_ref, kl_ref, kc_ref, kr_ref, vl_ref, vc_ref, vr_ref, ck_ref, cv_ref, cos_ref, sin_ref,
                        sink_ref, o_ref):
    i = pl.program_id(1)
    nb = DEC_SEQ // BLOCK

    def table(ref, blk):
        return ref[pl.ds(pl.multiple_of(blk * BLOCK, BLOCK), BLOCK), :]

    left, right = jnp.maximum(i - 1, 0), jnp.minimum(i + 1, nb - 1)
    cos_q, sin_q = table(cos_ref, i), table(sin_ref, i)
    q = jnp.concatenate([_rope(q_ref[:, c:c + LANE], cos_q, sin_q) for c in range(0, ATTN_WIDTH, LANE)], axis=1)
    kw = jnp.concatenate([
        _rope(kl_ref[...], table(cos_ref, left), table(sin_ref, left)),
        _rope(kc_ref[...], cos_q, sin_q),
        _rope(kr_ref[...], table(cos_ref, right), table(sin_ref, right)),
    ], axis=0)
    vw = jnp.concatenate([vl_ref[...], vc_ref[...], vr_ref[...]], axis=0)

    m_rows = Q_PER_KV * BLOCK
    r = lax.broadcasted_iota(jnp.int32, (m_rows, 3 * BLOCK), 0) % BLOCK
    c = lax.broadcasted_iota(jnp.int32, (m_rows, 3 * BLOCK), 1)
    kpos = c + (i - 1) * BLOCK
    win = (kpos >= 0) & (kpos < DEC_SEQ) & (jnp.abs(r + BLOCK - c) <= WINDOW)

    o_ref[...] = _gqa(q, [ck_ref[...], kw], [cv_ref[...], vw], [None, win], sink_ref)


def _latent_attention(zm, cache_k, cache_v, layer, cos, sin, sink):
    nb = DEC_SEQ // BLOCK
    blk0 = N_PROMPT_TOK // BLOCK
    kcol, vcol = SPLIT_K // KV_WIDTH, SPLIT_V // KV_WIDTH

    def row(b, i):
        return blk0 + b * nb + i

    def win_specs(col):
        return [
            pl.BlockSpec((BLOCK, KV_WIDTH), lambda b, i: (row(b, jnp.maximum(i - 1, 0)), col)),
            pl.BlockSpec((BLOCK, KV_WIDTH), lambda b, i: (row(b, i), col)),
            pl.BlockSpec((BLOCK, KV_WIDTH), lambda b, i: (row(b, jnp.minimum(i + 1, nb - 1)), col)),
        ]

    ctx_spec = pl.BlockSpec((None, None, PAST_LEN, KV_WIDTH), lambda b, i: (b, layer, 0, 0))
    tab_spec = pl.BlockSpec((DEC_SEQ, LANE), lambda b, i: (0, 0))
    return pl.pallas_call(
        _latent_attn_kernel,
        grid=(DEC_BATCH, nb),
        in_specs=[pl.BlockSpec((BLOCK, ATTN_WIDTH), lambda b, i: (row(b, i), SPLIT_Q // ATTN_WIDTH))]
        + win_specs(kcol) + win_specs(vcol)
        + [ctx_spec, ctx_spec, tab_spec, tab_spec, pl.BlockSpec(memory_space=pltpu.SMEM)],
        out_specs=pl.BlockSpec((BLOCK, ATTN_WIDTH), lambda b, i: (b * nb + i, 0)),
        out_shape=jax.ShapeDtypeStruct((N_SAMPLE_TOK, ATTN_WIDTH), F32),
        compiler_params=_cparams("arbitrary", "arbitrary"),
        name="latent_attention",
    )(zm, zm, zm, zm, zm, zm, zm, cache_k, cache_v, cos, sin, sink)


def kernel(x_prompt, x_sample, cache_k, cache_v, state_s5_re, state_s5_im, c, c_ctx, ada_w, ada_b, norm1_g, w_in, hy_conv_w, hy_conv_b, hy_pos_w1, hy_pos_b1, hy_pos_w2, hy_pos_b2, hy_pos_w3, hy_sin_freq, hy_decay, hy_skip, s5_lam_re, s5_lam_im, s5_log_step, s5_b_re, s5_b_im, s5_c_re, s5_c_im, s5_skip, s5_glu_w, s5_glu_b, attn_sink, proj_a, proj_b, proj_c, w_out, norm2_g, ffn_w_gate, ffn_w_up, ffn_w_down, final_norm_g):
    cvec = jnp.concatenate([c_ctx[None], c, jnp.zeros((MOD_ROWS - 1 - DEC_BATCH, D_MODEL), F32)], axis=0)
    mod = _modulation(cvec, ada_w, ada_b).reshape(DEPTH * MOD_ROWS * 6, 1, D_MODEL)

    x = jnp.concatenate([x_prompt.reshape(N_PROMPT_TOK, D_MODEL), x_sample.reshape(N_SAMPLE_TOK, D_MODEL)], axis=0)
    w_main = w_in[:, :, :MAIN_IN].astype(BF16)
    w_gate = w_in[:, :, MAIN_IN:].astype(BF16)
    pa, pb, pc, wo = proj_a.astype(BF16), proj_b.astype(BF16), proj_c.astype(BF16), w_out.astype(BF16)
    wg, wu, wd = ffn_w_gate.astype(BF16), ffn_w_up.astype(BF16), ffn_w_down.astype(BF16)
    g1, g2 = norm1_g.reshape(DEPTH, 1, D_MODEL), norm2_g.reshape(DEPTH, 1, D_MODEL)
    glu_w = s5_glu_w.astype(BF16)

    ctx_k = cache_k.reshape(DEC_BATCH, DEPTH, PAST_LEN, KV_WIDTH)
    ctx_v = cache_v.reshape(DEC_BATCH, DEPTH, PAST_LEN, KV_WIDTH)
    rope_cos, rope_sin = _rope_tables(DEC_SEQ)
    mats_p, mats_s = _dft_mats(min(SEQ, HY_BLOCK)), _dft_mats(min(DEC_SEQ, HY_BLOCK))

    ks_out, vs_out, sre_out, sim_out = [], [], [], []
    for l in range(DEPTH):
        filt = (hy_pos_w1[l], hy_pos_b1[l], hy_pos_w2[l], hy_pos_b2[l], hy_pos_w3[l], hy_sin_freq[l], hy_decay[l])
        zm = _in_proj(x, mod, l, g1, w_main)
        u, vb = _short_conv(zm, hy_conv_w[l], hy_conv_b[l].reshape(1, HY_IN))
        ya_p = _hyena_path(u, vb, 0, BATCH, SEQ, mats_p, _hyena_tables(SEQ, mats_p, filt), hy_skip[l])
        ya_s = _hyena_path(u, vb, N_PROMPT_TOK, DEC_BATCH, DEC_SEQ, mats_s, _hyena_tables(DEC_SEQ, mats_s, filt),
                           hy_skip[l])
        k_l = zm[:N_PROMPT_TOK, SPLIT_K:SPLIT_V].reshape(BATCH, SEQ, N_KV_HEADS, HEAD_DIM)
        v_l = zm[:N_PROMPT_TOK, SPLIT_V:].reshape(BATCH, SEQ, N_KV_HEADS, HEAD_DIM)
        yc_p = _context_attention(zm, attn_sink[l])
        yc_s = _latent_attention(zm, ctx_k, ctx_v, l, rope_cos, rope_sin, attn_sink[l])

        lam, bd, cd = _s5_params(s5_lam_re[l], s5_lam_im[l], s5_log_step[l], s5_b_re[l], s5_b_im[l],
                                 s5_c_re[l], s5_c_im[l])
        s5_w = (lam, bd, cd, s5_skip[l].reshape(1, S5_WIDTH), glu_w[l], s5_glu_b[l].reshape(1, 2 * S5_WIDTH))
        h0_s = jnp.stack([state_s5_re[:, l, 0], state_s5_im[:, l, 0], state_s5_re[:, l, 1], state_s5_im[:, l, 1]],
                         axis=1).reshape(DEC_BATCH, 4, S5_LANES)
        yb_p, fin_p = _s5_branch(zm, 0, BATCH, SEQ, SEQ, jnp.zeros((BATCH, 4, S5_LANES), F32), *s5_w)
        yb_s, _ = _s5_branch(zm, N_PROMPT_TOK, DEC_BATCH, DEC_SEQ, 512, h0_s, *s5_w)
        fin_p = fin_p.reshape(BATCH, 2, 2, S5_GROUPS, S5_STATE)

        ks_out.append(k_l)
        vs_out.append(v_l)
        sre_out.append(fin_p[:, :, 0])
        sim_out.append(fin_p[:, :, 1])
        x = _merge((ya_p, ya_s), (yb_p, yb_s), (yc_p, yc_s), x, mod, l, g1, w_gate, pa, pb, pc, wo)
        x = _ffn(x, mod, l, g2, wg, wu, wd)

    y_prompt, y_sample = _final_norm(x, final_norm_g.reshape(1, D_MODEL))
    return (y_prompt.reshape(BATCH, SEQ, D_MODEL), y_sample.reshape(DEC_BATCH, DEC_SEQ, D_MODEL),
            jnp.stack(ks_out, axis=1), jnp.stack(vs_out, axis=1), jnp.stack(sre_out, axis=1), jnp.stack(sim_out, axis=1))
```

```python
import functools
import math

import jax
import jax.numpy as jnp
import numpy as np
from jax import lax
from jax.experimental import pallas as pl
from jax.experimental.pallas import tpu as pltpu

D_MODEL = 1024
BATCH = 16
SEQ = 256
DEPTH = 4
DEC_BATCH = 2
DEC_SEQ = 4096
PAST_LEN = 512
GRID_W = 64
N_BRANCH = 3
HY_WIDTH = 256
HY_ORDER = 2
HY_SHORT_K = 3
HY_POS_EMB = 33
HY_POS_BANDS = (HY_POS_EMB - 1) // 2
HY_FILTER_HIDDEN = 64
S5_WIDTH = 256
S5_GROUP = 16
S5_GROUPS = S5_WIDTH // S5_GROUP
S5_STATE = 64
N_HEADS = 8
N_KV_HEADS = 2
Q_PER_KV = N_HEADS // N_KV_HEADS
HEAD_DIM = 64
ATTN_WIDTH = N_HEADS * HEAD_DIM
KV_WIDTH = N_KV_HEADS * HEAD_DIM
WINDOW = 128
BLOCK = 128
ROPE_BASE = 10000.0
FFN_HIDDEN = ((8 * D_MODEL // 3 + 255) // 256) * 256
HY_IN = (HY_ORDER + 1) * HY_WIDTH
GATE_IN = N_BRANCH * D_MODEL
MAIN_IN = HY_IN + S5_WIDTH + ATTN_WIDTH + 2 * KV_WIDTH
SPLIT_S5 = HY_IN
SPLIT_Q = SPLIT_S5 + S5_WIDTH
SPLIT_K = SPLIT_Q + ATTN_WIDTH
SPLIT_V = SPLIT_K + KV_WIDTH

F32 = jnp.float32
BF16 = jnp.bfloat16
EPS = 1e-6
NEG_INF = -1e30

N_PROMPT_TOK = BATCH * SEQ
N_SAMPLE_TOK = DEC_BATCH * DEC_SEQ
N_TOK = N_PROMPT_TOK + N_SAMPLE_TOK
LANE = 128
MOD_ROWS = 8
TM = 512
VMEM_LIMIT = 56 * 1024 * 1024


def _cparams(*sem):
    return pltpu.CompilerParams(dimension_semantics=sem, vmem_limit_bytes=VMEM_LIMIT)


def _mod_row(i):
    n_p = N_PROMPT_TOK // TM
    per_b = DEC_SEQ // TM
    return jnp.where(i < n_p, 0, 1 + (i - n_p) // per_b)


def _mod_spec(layer, k):
    return pl.BlockSpec((None, 1, D_MODEL), lambda i: ((layer * MOD_ROWS + _mod_row(i)) * 6 + k, 0, 0))


def _resident(shape, layer=None):
    if layer is None:
        return pl.BlockSpec(shape, lambda i: (0,) * len(shape), pipeline_mode=pl.Buffered(1))
    return pl.BlockSpec((None,) + shape, lambda i: (layer,) + (0,) * len(shape), pipeline_mode=pl.Buffered(1))


def _rms(x, g):
    return x * lax.rsqrt(jnp.mean(x * x, axis=-1, keepdims=True) + EPS) * g


def _mod_kernel(c_ref, w_ref, b_ref, o_ref):
    c = c_ref[...]
    s = (c * jax.nn.sigmoid(c)).astype(BF16)
    o_ref[...] = jnp.dot(s, w_ref[...].astype(BF16), preferred_element_type=F32) + b_ref[...]


def _modulation(cvec, ada_w, ada_b):
    tn = 1536
    return pl.pallas_call(
        _mod_kernel,
        grid=(DEPTH, 6 * D_MODEL // tn),
        in_specs=[
            pl.BlockSpec((MOD_ROWS, D_MODEL), lambda l, j: (0, 0)),
            pl.BlockSpec((None, D_MODEL, tn), lambda l, j: (l, 0, j)),
            pl.BlockSpec((None, 1, tn), lambda l, j: (l, 0, j)),
        ],
        out_specs=pl.BlockSpec((None, MOD_ROWS, tn), lambda l, j: (l, 0, j)),
        out_shape=jax.ShapeDtypeStruct((DEPTH, MOD_ROWS, 6 * D_MODEL), F32),
        compiler_params=_cparams("arbitrary", "arbitrary"),
        name="adaln_mod",
    )(cvec, ada_w, ada_b.reshape(DEPTH, 1, 6 * D_MODEL))


def _inproj_kernel(x_ref, sh_ref, sc_ref, g_ref, wm_ref, zm_ref):
    h = (_rms(x_ref[...], g_ref[...]) * (1.0 + sc_ref[...]) + sh_ref[...]).astype(BF16)
    zm_ref[...] = jnp.dot(h, wm_ref[...], preferred_element_type=F32)


def _in_proj(x, mod, layer, g, w_main):
    return pl.pallas_call(
        _inproj_kernel,
        grid=(N_TOK // TM,),
        in_specs=[
            pl.BlockSpec((TM, D_MODEL), lambda i: (i, 0)),
            _mod_spec(layer, 0),
            _mod_spec(layer, 1),
            _resident((1, D_MODEL), layer),
            _resident((D_MODEL, MAIN_IN), layer),
        ],
        out_specs=pl.BlockSpec((TM, MAIN_IN), lambda i: (i, 0)),
        out_shape=jax.ShapeDtypeStruct((N_TOK, MAIN_IN), F32),
        compiler_params=_cparams("arbitrary"),
        name="in_proj",
    )(x, mod, mod, g, w_main)


N_PROMPT_TILES = N_PROMPT_TOK // TM


def _path_specs(width):
    return [pl.BlockSpec((TM, width), lambda i: (jnp.minimum(i, N_PROMPT_TILES - 1), 0)),
            pl.BlockSpec((TM, width), lambda i: (jnp.maximum(i - N_PROMPT_TILES, 0), 0))]


def _merge_kernel(yap_ref, yas_ref, ybp_ref, ybs_ref, ycp_ref, ycs_ref, x_ref, sh_ref, sc_ref, g1_ref, g_ref,
                  wg_ref, pa_ref, pb_ref, pc_ref, wo_ref, o_ref):
    is_prompt = pl.program_id(0) < N_PROMPT_TILES
    x = x_ref[...]
    h = (_rms(x, g_ref[...]) * (1.0 + sc_ref[...]) + sh_ref[...]).astype(BF16)
    branches = ((yap_ref, yas_ref, pa_ref), (ybp_ref, ybs_ref, pb_ref), (ycp_ref, ycs_ref, pc_ref))
    m = jnp.zeros((TM, D_MODEL), F32)
    for k, (p_ref, s_ref, w_ref) in enumerate(branches):
        y = jnp.where(is_prompt, p_ref[...], s_ref[...]).astype(BF16)
        gate = jax.nn.sigmoid(jnp.dot(h, wg_ref[:, k * D_MODEL:(k + 1) * D_MODEL], preferred_element_type=F32))
        m = m + gate * jnp.dot(y, w_ref[...], preferred_element_type=F32)
    o_ref[...] = x + g1_ref[...] * jnp.dot(m.astype(BF16), wo_ref[...], preferred_element_type=F32)


def _merge(ya, yb, yc, x, mod, layer, g, w_gate, pa, pb, pc, wo):
    return pl.pallas_call(
        _merge_kernel,
        grid=(N_TOK // TM,),
        in_specs=_path_specs(HY_WIDTH) + _path_specs(S5_WIDTH) + _path_specs(ATTN_WIDTH) + [
            pl.BlockSpec((TM, D_MODEL), lambda i: (i, 0)),
            _mod_spec(layer, 0),
            _mod_spec(layer, 1),
            _mod_spec(layer, 2),
            _resident((1, D_MODEL), layer),
            _resident((D_MODEL, GATE_IN), layer),
            _resident((HY_WIDTH, D_MODEL), layer),
            _resident((S5_WIDTH, D_MODEL), layer),
            _resident((ATTN_WIDTH, D_MODEL), layer),
            _resident((D_MODEL, D_MODEL), layer),
        ],
        out_specs=pl.BlockSpec((TM, D_MODEL), lambda i: (i, 0)),
        out_shape=jax.ShapeDtypeStruct((N_TOK, D_MODEL), F32),
        compiler_params=_cparams("arbitrary"),
        name="merge_out",
    )(*ya, *yb, *yc, x, mod, mod, mod, g, w_gate, pa, pb, pc, wo)


def _ffn_kernel(x_ref, sh_ref, sc_ref, gt_ref, g_ref, wg_ref, wu_ref, wd_ref, o_ref):
    x = x_ref[...]
    h = (_rms(x, g_ref[...]) * (1.0 + sc_ref[...]) + sh_ref[...]).astype(BF16)
    a = jnp.dot(h, wg_ref[...], preferred_element_type=F32)
    b = jnp.dot(h, wu_ref[...], preferred_element_type=F32)
    f = (a * jax.nn.sigmoid(a) * b).astype(BF16)
    o_ref[...] = x + gt_ref[...] * jnp.dot(f, wd_ref[...], preferred_element_type=F32)


def _ffn(x, mod, layer, g, wg, wu, wd):
    return pl.pallas_call(
        _ffn_kernel,
        grid=(N_TOK // TM,),
        in_specs=[
            pl.BlockSpec((TM, D_MODEL), lambda i: (i, 0)),
            _mod_spec(layer, 3),
            _mod_spec(layer, 4),
            _mod_spec(layer, 5),
            _resident((1, D_MODEL), layer),
            _resident((D_MODEL, FFN_HIDDEN), layer),
            _resident((D_MODEL, FFN_HIDDEN), layer),
            _resident((FFN_HIDDEN, D_MODEL), layer),
        ],
        out_specs=pl.BlockSpec((TM, D_MODEL), lambda i: (i, 0)),
        out_shape=jax.ShapeDtypeStruct((N_TOK, D_MODEL), F32),
        compiler_params=_cparams("arbitrary"),
        name="ffn",
    )(x, mod, mod, mod, g, wg, wu, wd)


def _final_norm_kernel(x_ref, g_ref, op_ref, os_ref):
    y = _rms(x_ref[...], g_ref[...])
    is_prompt = pl.program_id(0) < N_PROMPT_TILES

    @pl.when(is_prompt)
    def _():
        op_ref[...] = y

    @pl.when(jnp.logical_not(is_prompt))
    def _():
        os_ref[...] = y


def _final_norm(x, g):
    return pl.pallas_call(
        _final_norm_kernel,
        grid=(N_TOK // TM,),
        in_specs=[pl.BlockSpec((TM, D_MODEL), lambda i: (i, 0)), _resident((1, D_MODEL))],
        out_specs=_path_specs(D_MODEL),
        out_shape=[jax.ShapeDtypeStruct((N_PROMPT_TOK, D_MODEL), F32),
                   jax.ShapeDtypeStruct((N_SAMPLE_TOK, D_MODEL), F32)],
        compiler_params=_cparams("arbitrary"),
        name="final_norm",
    )(x, g)


SUBLANE = 8
HY_TILE = 256
HY_BLOCK = 512
HY_ROWS = 64


def _dft_mats(blk):
    n = 2 * blk
    idx = jnp.arange(blk, dtype=jnp.int32)
    ang = (2.0 * math.pi / n) * ((idx[:, None] * idx[None, :]) % n).astype(F32)
    alt = jnp.where(idx % 2 == 0, 1.0, -1.0).astype(F32)
    return jnp.stack([jnp.cos(ang), jnp.where((idx == 0)[:, None], alt[None, :], -jnp.sin(ang))]).astype(BF16)


def _short_conv_kernel(x_ref, xp_ref, xn_ref, w_ref, b_ref, u_ref, vb_ref):
    i = pl.program_id(0)
    n_p = N_PROMPT_TOK // HY_TILE
    per_p, per_s = SEQ // HY_TILE, DEC_SEQ // HY_TILE
    idx = jnp.where(i < n_p, i % per_p, (i - n_p) % per_s)
    per = jnp.where(i < n_p, per_p, per_s)
    has_prev = (idx > 0).astype(F32)
    has_next = (idx < per - 1).astype(F32)
    x = x_ref[...]
    row = lax.broadcasted_iota(jnp.int32, (HY_TILE, 1), 0)
    x_prev = jnp.where(row == 0, xp_ref[SUBLANE - 1:SUBLANE, :] * has_prev, pltpu.roll(x, 1, axis=0))
    x_next = jnp.where(row == HY_TILE - 1, xn_ref[0:1, :] * has_next, pltpu.roll(x, HY_TILE - 1, axis=0))
    u = b_ref[...] + x_prev * w_ref[0:1, :] + x * w_ref[1:2, :] + x_next * w_ref[2:3, :]
    u_ref[...] = u
    vb_ref[...] = u[:, :HY_WIDTH].astype(BF16)


def _short_conv(zm, w, b):
    per8 = HY_TILE // SUBLANE
    last8 = N_TOK // SUBLANE - 1
    return pl.pallas_call(
        _short_conv_kernel,
        grid=(N_TOK // HY_TILE,),
        in_specs=[
            pl.BlockSpec((HY_TILE, HY_IN), lambda i: (i, 0)),
            pl.BlockSpec((SUBLANE, HY_IN), lambda i: (jnp.maximum(i * per8 - 1, 0), 0)),
            pl.BlockSpec((SUBLANE, HY_IN), lambda i: (jnp.minimum((i + 1) * per8, last8), 0)),
            pl.BlockSpec((HY_SHORT_K, HY_IN), lambda i: (0, 0)),
            pl.BlockSpec((1, HY_IN), lambda i: (0, 0)),
        ],
        out_specs=[
            pl.BlockSpec((HY_TILE, HY_IN), lambda i: (i, 0)),
            pl.BlockSpec((HY_TILE, HY_WIDTH), lambda i: (i, 0)),
        ],
        out_shape=[
            jax.ShapeDtypeStruct((N_TOK, HY_IN), F32),
            jax.ShapeDtypeStruct((N_TOK, HY_WIDTH), BF16),
        ],
        compiler_params=_cparams("arbitrary"),
        name="hyena_short_conv",
    )(zm, zm, zm, w, b)


def _block_conv_kernel(n_blk, last, zin_ref, k_ref, f_ref, g_ref, zp_ref, skip_ref, o_ref, *rest):
    z_ref, s_ref = rest[-2:]
    n = pl.program_id(1)
    blk = f_ref.shape[-1]

    @pl.when(n == 0)
    def _spectra():
        group = min(n_blk, 8)
        for g0 in range(0, n_blk, group):
            z = jnp.concatenate([zin_ref[(g0 + g) * blk:(g0 + g + 1) * blk, :] for g in range(group)], axis=1)
            for c in range(2):
                x = jnp.dot(f_ref[c], z, preferred_element_type=F32)
                for g in range(group):
                    z_ref[g0 + g, c] = x[:, g * HY_WIDTH:(g + 1) * HY_WIDTH]

    def product(r, first_row_packed):
        acc_r = acc_i = jnp.zeros((r.size, HY_WIDTH), F32)
        fix_r = fix_i = acc_r
        for i in range(n_blk):
            d = n - i + n_blk - 1
            zr, zi = z_ref[i, 0, r, :], z_ref[i, 1, r, :]
            kr, ki = k_ref[d, 0, r, :], k_ref[d, 1, r, :]
            acc_r = acc_r + (zr * kr - zi * ki)
            acc_i = acc_i + (zr * ki + zi * kr)
            if first_row_packed:
                fix_r = fix_r + zr * kr
                fix_i = fix_i + zi * ki
        if first_row_packed:
            row0 = lax.broadcasted_iota(jnp.int32, (r.size, 1), 0) == 0
            acc_r = jnp.where(row0, fix_r, acc_r)
            acc_i = jnp.where(row0, fix_i, acc_i)
        s_ref[0, r, :] = acc_r.astype(BF16)
        s_ref[1, r, :] = acc_i.astype(BF16)

    def chunk(c, carry):
        product(pl.ds(pl.multiple_of(c * HY_ROWS, HY_ROWS), HY_ROWS), False)
        return carry

    lax.fori_loop(0, blk // HY_ROWS, chunk, 0)
    product(pl.ds(0, 2 * SUBLANE), True)

    conv_c = jnp.dot(f_ref[0], s_ref[0], preferred_element_type=F32)
    conv_s = jnp.dot(f_ref[1], s_ref[1], preferred_element_type=F32)
    t = lax.broadcasted_iota(jnp.int32, (blk, 1), 0)
    alt = jnp.where(t % 2 == 0, 1.0, -1.0).astype(F32)
    conv = conv_c + jnp.where(t == 0, 0.0, conv_s) + alt * s_ref[1, 0:1, :].astype(F32)
    z = g_ref[...] * (conv + skip_ref[...] * zp_ref[...])
    o_ref[...] = z
    if not last:
        rest[0][...] = z.astype(BF16)


def _block_conv(zb, zb_row0, tables, mats, u, gate_col, z_prev, z_prev_row0, skip, row0, n_seq, seq_len, last):
    blk = mats.shape[-1]
    n_blk = seq_len // blk

    def rows(base):
        return lambda b, n: (base // blk + b * n_blk + n, 0)

    out_spec = pl.BlockSpec((blk, HY_WIDTH), rows(0))
    out_shape = [jax.ShapeDtypeStruct((n_seq * n_blk * blk, HY_WIDTH), F32)]
    if not last:
        out_shape.append(jax.ShapeDtypeStruct((n_seq * n_blk * blk, HY_WIDTH), BF16))
    return pl.pallas_call(
        functools.partial(_block_conv_kernel, n_blk, last),
        grid=(n_seq, n_blk),
        in_specs=[
            pl.BlockSpec((seq_len, HY_WIDTH), lambda b, n: (zb_row0 // seq_len + b, 0)),
            pl.BlockSpec(tables.shape, lambda b, n: (0, 0, 0, 0), pipeline_mode=pl.Buffered(1)),
            pl.BlockSpec((2, blk, blk), lambda b, n: (0, 0, 0)),
            pl.BlockSpec((blk, HY_WIDTH), lambda b, n: (row0 // blk + b * n_blk + n, gate_col)),
            pl.BlockSpec((blk, HY_WIDTH), rows(z_prev_row0)),
            pl.BlockSpec((1, HY_WIDTH), lambda b, n: (0, 0)),
        ],
        out_specs=[out_spec] * len(out_shape),
        out_shape=out_shape,
        scratch_shapes=[pltpu.VMEM((n_blk, 2, blk, HY_WIDTH), F32), pltpu.VMEM((2, blk, HY_WIDTH), BF16)],
        compiler_params=_cparams("arbitrary", "arbitrary"),
        name="hyena_block_conv",
    )(zb, tables, mats, u, z_prev, skip)


def _taps_kernel(seq_len, rows, w1_ref, b1_ref, w2_ref, b2_ref, w3_ref, freq_ref, decay_ref, k_ref, norm_ref):
    i = pl.program_id(0)
    lag = i * rows + lax.broadcasted_iota(jnp.int32, (rows, 1), 0) - seq_len
    t = jnp.abs(lag).astype(F32) * (1.0 / (seq_len - 1))
    pos = jnp.abs(i * rows + lax.broadcasted_iota(jnp.int32, (1, rows), 1) - seq_len).astype(F32)
    band = lax.broadcasted_iota(jnp.int32, (HY_POS_BANDS, 1), 0).astype(F32)
    band = 1e-4 + band * ((HY_POS_BANDS - 1 - 1e-4) / (HY_POS_BANDS - 1))
    ang = band * (pos * (2.0 * math.pi / seq_len))
    dot = functools.partial(jnp.dot, precision=lax.Precision.HIGHEST, preferred_element_type=F32)
    pre = w1_ref[:, 0:1] * (pos * (1.0 / (seq_len - 1)))
    pre = pre + dot(w1_ref[:, 1:1 + HY_POS_BANDS], jnp.cos(ang))
    pre = pre - dot(w1_ref[:, 1 + HY_POS_BANDS:], jnp.sin(ang))
    freq = freq_ref[...]
    h = jnp.sin(freq * (pre + b1_ref[...]))
    h = jnp.sin(freq * (dot(w2_ref[...], h) + b2_ref[...]))
    h = lax.dot_general(h.astype(BF16), w3_ref[...].astype(BF16), (((0,), (0,)), ((), ())),
                        preferred_element_type=F32)
    h = h * jnp.exp(-t * jnp.abs(decay_ref[...]))

    @pl.when(i == 0)
    def _():
        norm_ref[...] = jnp.zeros_like(norm_ref)

    for o in range(HY_ORDER):
        c0 = o * 2 * HY_WIDTH
        k = jnp.where(lag >= 0, h[:, c0:c0 + HY_WIDTH], h[:, c0 + HY_WIDTH:c0 + 2 * HY_WIDTH])
        k = jnp.where(lag == -seq_len, 0.0, k)
        k_ref[o] = k.astype(BF16)
        norm_ref[:, o * HY_WIDTH:(o + 1) * HY_WIDTH] += jnp.sum(jnp.abs(k), axis=0, keepdims=True)


def _hyena_taps(seq_len, w1, b1, w2, b2, w3, freq, decay):
    rows = min(2 * seq_len, 512)
    full = lambda a: pl.BlockSpec(a.shape, lambda i: (0,) * a.ndim)
    args = (w1.T, b1.reshape(-1, 1), w2.T, b2.reshape(-1, 1), w3, freq.reshape(-1, 1), decay.reshape(1, -1))
    return pl.pallas_call(
        functools.partial(_taps_kernel, seq_len, rows),
        grid=(2 * seq_len // rows,),
        in_specs=[full(a) for a in args],
        out_specs=[pl.BlockSpec((HY_ORDER, rows, HY_WIDTH), lambda i: (0, i, 0)),
                   pl.BlockSpec((1, HY_ORDER * HY_WIDTH), lambda i: (0, 0))],
        out_shape=[jax.ShapeDtypeStruct((HY_ORDER, 2 * seq_len, HY_WIDTH), BF16),
                   jax.ShapeDtypeStruct((1, HY_ORDER * HY_WIDTH), F32)],
        compiler_params=_cparams("arbitrary"),
        name="hyena_taps",
    )(*args)


def _tables_kernel(x_ref, norm_ref, f_ref, o_ref, prev_ref):
    q = pl.program_id(1)
    blk = f_ref.shape[-1]
    f = lax.broadcasted_iota(jnp.int32, (blk, 1), 0)
    alt = jnp.where(f % 2 == 0, 1.0, -1.0).astype(F32)
    scale = jnp.where(f == 0, 1.0, 2.0).astype(F32) * (1.0 / (2 * blk)) / norm_ref[...]
    x = x_ref[...]
    for c in range(2):
        xc = jnp.dot(f_ref[c], x, preferred_element_type=F32)

        @pl.when(q > 0)
        def _(c=c, xc=xc):
            o_ref[c] = (xc + alt * prev_ref[c]) * scale

        prev_ref[c] = xc


def _hyena_tables(seq_len, mats, filt):
    blk = mats.shape[-1]
    n_lag = 2 * seq_len // blk
    taps, norm = _hyena_taps(seq_len, *filt)
    return pl.pallas_call(
        _tables_kernel,
        grid=(HY_ORDER, n_lag),
        in_specs=[
            pl.BlockSpec((None, blk, HY_WIDTH), lambda o, q: (o, q, 0)),
            pl.BlockSpec((1, HY_WIDTH), lambda o, q: (0, o)),
            pl.BlockSpec((2, blk, blk), lambda o, q: (0, 0, 0)),
        ],
        out_specs=pl.BlockSpec((None, None, 2, blk, HY_WIDTH), lambda o, q: (o, jnp.maximum(q - 1, 0), 0, 0, 0)),
        out_shape=jax.ShapeDtypeStruct((HY_ORDER, n_lag - 1, 2, blk, HY_WIDTH), F32),
        scratch_shapes=[pltpu.VMEM((2, blk, HY_WIDTH), F32)],
        compiler_params=_cparams("arbitrary", "arbitrary"),
        name="hyena_tables",
    )(taps, norm, mats)


def _hyena_path(u, vb, row0, n_seq, seq_len, mats, tables, skip):
    z1, z1b = _block_conv(vb, row0, tables[0], mats, u, 1, u, row0, skip[0:1], row0, n_seq, seq_len, False)
    (ya,) = _block_conv(z1b, 0, tables[1], mats, u, 2, z1, 0, skip[1:2], row0, n_seq, seq_len, True)
    return ya


S5_LANES = S5_GROUPS * S5_STATE
S5_SEGS = 8
S5_CHUNKS = 4


def _s5_kernel(n_tiles, tile, u0_ref, u1_ref, h0_ref, lam_ref, bd_ref, cd_ref, skip_ref, gw_ref, gb_ref,
               y_ref, fin_ref, bu_ref, ybwd_ref, p_ref, p8_ref, cm_ref, fl_ref, carry_ref, up_ref, st0_ref, st1_ref,
               ub_ref, yacc_ref):
    seg = tile // S5_SEGS
    n = S5_LANES
    b = pl.program_id(0)
    j = pl.program_id(1)

    @pl.when((b == 0) & (j == 0))
    def _powers():
        for d in range(2):
            lr = lam_ref[2 * d:2 * d + 1, :]
            li = lam_ref[2 * d + 1:2 * d + 2, :]

            def put(k, pr, pi, d=d):
                p_ref[d, pl.ds(k, 1), 0:n] = pr
                p_ref[d, pl.ds(k, 1), n:2 * n] = pi
                r = pl.ds(pl.multiple_of(k * S5_SEGS, S5_SEGS), S5_SEGS)
                p8_ref[d, r, 0:n] = jnp.broadcast_to(pr, (S5_SEGS, n))
                p8_ref[d, r, n:2 * n] = jnp.broadcast_to(pi, (S5_SEGS, n))

            put(0, lr, li)

            def body(k, c, lr=lr, li=li, put=put):
                pr, pi = c
                nr = pr * lr - pi * li
                ni = pr * li + pi * lr
                put(k, nr, ni)
                return nr, ni

            lax.fori_loop(1, seg, body, (lr, li))

    def sweep(d, t):
        first, last = (0, n_tiles - 1) if d == 0 else (n_tiles - 1, 0)

        @pl.when(t == first)
        def _init():
            carry_ref[0:1, 0:n] = h0_ref[2 * d:2 * d + 1, :]
            carry_ref[0:1, n:2 * n] = h0_ref[2 * d + 1:2 * d + 2, :]

        def gather(i, c):
            r = pl.ds(pl.multiple_of(i * S5_SEGS, S5_SEGS), S5_SEGS)
            up_ref[r, 0:LANE] = u0_ref[pl.ds(i, S5_SEGS, stride=seg), :]
            up_ref[r, LANE:2 * LANE] = u1_ref[pl.ds(i, S5_SEGS, stride=seg), :]
            return c

        lax.fori_loop(0, seg, gather, 0, unroll=2)
        ub_ref[...] = up_ref[...].astype(BF16)
        nq = n // S5_CHUNKS

        def lanes(q):
            return slice(q * nq, (q + 1) * nq), slice(n + q * nq, n + (q + 1) * nq)

        def rows(i):
            k = i if d == 0 else seg - 1 - i
            return slice(k * S5_SEGS, (k + 1) * S5_SEGS)

        def project_in(q):
            re, im = lanes(q)
            bu_ref[:, re] = jnp.dot(ub_ref[...], bd_ref[d, :, re], preferred_element_type=F32)
            bu_ref[:, im] = jnp.dot(ub_ref[...], bd_ref[d, :, im], preferred_element_type=F32)

        def scan(q):
            re, im = lanes(q)
            lr8 = jnp.broadcast_to(lam_ref[2 * d:2 * d + 1, re], (S5_SEGS, nq))
            li8 = jnp.broadcast_to(lam_ref[2 * d + 1:2 * d + 2, re], (S5_SEGS, nq))
            hr = hi = jnp.zeros((S5_SEGS, nq), F32)
            for i in range(seg):
                r = rows(i)
                hr, hi = lr8 * hr - li8 * hi + bu_ref[r, re], lr8 * hi + li8 * hr + bu_ref[r, im]
                bu_ref[r, re] = hr
                bu_ref[r, im] = hi
            fl_ref[:, re] = hr
            fl_ref[:, im] = hi

            plr = p_ref[d, seg - 1:seg, re]
            pli = p_ref[d, seg - 1:seg, im]
            cr = carry_ref[0:1, re]
            ci = carry_ref[0:1, im]
            for s in (range(S5_SEGS) if d == 0 else reversed(range(S5_SEGS))):
                cm_ref[s:s + 1, re] = cr
                cm_ref[s:s + 1, im] = ci
                cr, ci = (plr * cr - pli * ci + fl_ref[s:s + 1, re], plr * ci + pli * cr + fl_ref[s:s + 1, im])
            carry_ref[0:1, re] = cr
            carry_ref[0:1, im] = ci

            cmr = cm_ref[:, re]
            cmi = cm_ref[:, im]
            for i in range(seg):
                r = rows(i)
                pw = slice(i * S5_SEGS, (i + 1) * S5_SEGS)
                pr = p8_ref[d, pw, re]
                pi = p8_ref[d, pw, im]
                bu_ref[r, re] = bu_ref[r, re] + (pr * cmr - pi * cmi)
                bu_ref[r, im] = bu_ref[r, im] + (pr * cmi + pi * cmr)

        def project_out(q):
            re, im = lanes(q)
            part = (jnp.dot(bu_ref[:, re].astype(BF16), cd_ref[d, re, :], preferred_element_type=F32)
                    + jnp.dot(bu_ref[:, im].astype(BF16), cd_ref[d, im, :], preferred_element_type=F32))
            yacc_ref[...] = part if q == 0 else yacc_ref[...] + part

        project_in(0)
        for q in range(S5_CHUNKS):
            if q + 1 < S5_CHUNKS:
                project_in(q + 1)
            if q > 0:
                project_out(q - 1)
            scan(q)
        project_out(S5_CHUNKS - 1)
        y = yacc_ref[...]

        @pl.when(t == last)
        def _final():
            fin_ref[2 * d:2 * d + 1, :] = carry_ref[0:1, 0:n]
            fin_ref[2 * d + 1:2 * d + 2, :] = carry_ref[0:1, n:2 * n]

        trow = pl.ds(pl.multiple_of(t * tile, tile), tile)
        if d == 1:
            ybwd_ref[trow, :] = y
        else:
            tot = up_ref[...] * skip_ref[...] + y + ybwd_ref[trow, :]
            g = jnp.dot(tot.astype(BF16), gw_ref[...], preferred_element_type=F32) + gb_ref[...]
            res = g[:, :S5_WIDTH] * jax.nn.sigmoid(g[:, S5_WIDTH:])
            st0_ref[...] = res[:, 0:LANE]
            st1_ref[...] = res[:, LANE:2 * LANE]
            per_seg = seg // S5_SEGS

            def unpermute(m, c):
                s = m // per_seg
                i0 = (m % per_seg) * S5_SEGS
                src = pl.ds(i0 * S5_SEGS + s, S5_SEGS, stride=S5_SEGS)
                dst = pl.ds(pl.multiple_of(m * S5_SEGS, S5_SEGS), S5_SEGS)
                y_ref[dst, 0:LANE] = st0_ref[src, :]
                y_ref[dst, LANE:2 * LANE] = st1_ref[src, :]
                return c

            lax.fori_loop(0, tile // S5_SEGS, unpermute, 0, unroll=2)

    @pl.when(j < n_tiles)
    def _bwd():
        sweep(1, n_tiles - 1 - j)

    @pl.when(j >= n_tiles)
    def _fwd():
        sweep(0, j - n_tiles)


def _s5_branch(zm, row0, n_seq, seq_len, tile, h0, lam, bd, cd, skip, gw, gb):
    n_tiles = seq_len // tile
    blk0 = row0 // tile
    col = SPLIT_S5 // LANE
    seg = tile // S5_SEGS

    def tile_of(j):
        return jnp.where(j < n_tiles, n_tiles - 1 - j, j - n_tiles)

    return pl.pallas_call(
        functools.partial(_s5_kernel, n_tiles, tile),
        grid=(n_seq, 2 * n_tiles),
        in_specs=[
            pl.BlockSpec((tile, LANE), lambda b, j: (blk0 + b * n_tiles + tile_of(j), col)),
            pl.BlockSpec((tile, LANE), lambda b, j: (blk0 + b * n_tiles + tile_of(j), col + 1)),
            pl.BlockSpec((None, 4, S5_LANES), lambda b, j: (b, 0, 0)),
            pl.BlockSpec((4, S5_LANES), lambda b, j: (0, 0)),
            pl.BlockSpec((2, S5_WIDTH, 2 * S5_LANES), lambda b, j: (0, 0, 0)),
            pl.BlockSpec((2, 2 * S5_LANES, S5_WIDTH), lambda b, j: (0, 0, 0)),
            pl.BlockSpec((1, S5_WIDTH), lambda b, j: (0, 0)),
            pl.BlockSpec((S5_WIDTH, 2 * S5_WIDTH), lambda b, j: (0, 0)),
            pl.BlockSpec((1, 2 * S5_WIDTH), lambda b, j: (0, 0)),
        ],
        out_specs=[
            pl.BlockSpec((tile, S5_WIDTH), lambda b, j: (b * n_tiles + jnp.maximum(j - n_tiles, 0), 0)),
            pl.BlockSpec((None, 4, S5_LANES), lambda b, j: (b, 0, 0)),
        ],
        out_shape=[
            jax.ShapeDtypeStruct((n_seq * seq_len, S5_WIDTH), F32),
            jax.ShapeDtypeStruct((n_seq, 4, S5_LANES), F32),
        ],
        scratch_shapes=[
            pltpu.VMEM((tile, 2 * S5_LANES), F32),
            pltpu.VMEM((seq_len, S5_WIDTH), F32),
            pltpu.VMEM((2, seg, 2 * S5_LANES), F32),
            pltpu.VMEM((2, tile, 2 * S5_LANES), F32),
            pltpu.VMEM((S5_SEGS, 2 * S5_LANES), F32),
            pltpu.VMEM((S5_SEGS, 2 * S5_LANES), F32),
            pltpu.VMEM((1, 2 * S5_LANES), F32),
            pltpu.VMEM((tile, S5_WIDTH), F32),
            pltpu.VMEM((tile, LANE), F32),
            pltpu.VMEM((tile, LANE), F32),
            pltpu.VMEM((tile, S5_WIDTH), BF16),
            pltpu.VMEM((tile, S5_WIDTH), F32),
        ],
        compiler_params=_cparams("arbitrary", "arbitrary"),
        name="s5_scan",
    )(zm, zm, h0, lam, bd, cd, skip, gw, gb)


def _s5_params(lam_re, lam_im, log_step, b_re, b_im, c_re, c_im):
    step = jnp.exp(log_step)[..., None]
    mag = jnp.exp(lam_re * step)
    bar_re, bar_im = mag * jnp.cos(lam_im * step), mag * jnp.sin(lam_im * step)
    den = lam_re * lam_re + lam_im * lam_im
    q_re = ((bar_re - 1.0) * lam_re + bar_im * lam_im) / den
    q_im = (bar_im * lam_re - (bar_re - 1.0) * lam_im) / den
    bb_re = q_re[..., None] * b_re - q_im[..., None] * b_im
    bb_im = q_re[..., None] * b_im + q_im[..., None] * b_re
    eye = jnp.eye(S5_GROUPS, dtype=F32)

    def in_mat(x):
        xt = jnp.swapaxes(x, 2, 3)
        return (xt[:, :, :, None, :] * eye[None, :, None, :, None]).reshape(2, S5_WIDTH, S5_LANES)

    def out_mat(x):
        xt = jnp.swapaxes(x, 2, 3)
        return (xt[:, :, :, None, :] * eye[None, :, None, :, None]).reshape(2, S5_LANES, S5_WIDTH)

    bd = jnp.concatenate([in_mat(bb_re), in_mat(bb_im)], axis=2).astype(BF16)
    cd = jnp.concatenate([out_mat(c_re), out_mat(-c_im)], axis=1).astype(BF16)
    lam_rows = jnp.stack([bar_re[0], bar_im[0], bar_re[1], bar_im[1]])
    return lam_rows.reshape(4, S5_LANES), bd, cd


ROPE_HALF = HEAD_DIM // 4


def _rope_tables(seq_len):
    t = jnp.arange(seq_len)
    row = (t // GRID_W).astype(F32)
    col = (t % GRID_W).astype(F32)
    inv = ROPE_BASE ** (-jnp.arange(ROPE_HALF, dtype=F32) / ROPE_HALF)
    ang_r = row[:, None] * inv
    ang_c = col[:, None] * inv
    cos = jnp.concatenate([jnp.cos(ang_r), jnp.cos(ang_r), jnp.cos(ang_c), jnp.cos(ang_c)], axis=-1)
    sin = jnp.concatenate([-jnp.sin(ang_r), jnp.sin(ang_r), -jnp.sin(ang_c), jnp.sin(ang_c)], axis=-1)
    return jnp.tile(cos, (1, LANE // HEAD_DIM)), jnp.tile(sin, (1, LANE // HEAD_DIM))


def _rope(x, cos, sin):
    lane = lax.broadcasted_iota(jnp.int32, x.shape, 1)
    first = (lane % (2 * ROPE_HALF)) < ROPE_HALF
    partner = jnp.where(first, pltpu.roll(x, LANE - ROPE_HALF, axis=1), pltpu.roll(x, ROPE_HALF, axis=1))
    return x * cos + partner * sin


def _gqa(q, ks, vs, masks, sink_ref):
    t = q.shape[0]
    low = lax.broadcasted_iota(jnp.int32, (1, LANE), 1) < HEAD_DIM
    assert N_KV_HEADS * HEAD_DIM == LANE
    k_nat = [k.astype(BF16) for k in ks]
    k_swp = [pltpu.roll(k, HEAD_DIM, axis=1).astype(BF16) for k in ks]
    v_aug = [jnp.concatenate([v, jnp.ones_like(v)], axis=1).astype(BF16) for v in vs]
    outs = {}
    for keys, heads in ((k_nat, [h for h in range(N_HEADS) if h % 2 == h // Q_PER_KV]),
                        (k_swp, [h for h in range(N_HEADS) if h % 2 != h // Q_PER_KV])):
        scale = HEAD_DIM ** -0.5
        assert math.frexp(scale)[0] == 0.5
        rows = [jnp.where(low if h % 2 == 0 else jnp.logical_not(low),
                          q[:, (h // 2) * LANE:(h // 2 + 1) * LANE] * scale, 0.0) for h in heads]
        qs = jnp.concatenate(rows, axis=0).astype(BF16)
        sink_col = jnp.concatenate([jnp.full((t, 1), sink_ref[h], F32) for h in heads], axis=0)
        ss = []
        m = sink_col
        for k, mk in zip(keys, masks):
            s = lax.dot_general(qs, k, (((1,), (1,)), ((), ())), preferred_element_type=F32)
            if mk is not None:
                s = jnp.where(mk, s, NEG_INF)
            ss.append(s)
            m = jnp.maximum(m, jnp.max(s, axis=-1, keepdims=True))
        acc = jnp.zeros((len(heads) * t, 2 * LANE), F32)
        for s, v in zip(ss, v_aug):
            acc = acc + jnp.dot(jnp.exp(s - m).astype(BF16), v, preferred_element_type=F32)
        o = acc[:, :LANE] / (acc[:, LANE:LANE + 1] + jnp.exp(sink_col - m))
        for j, h in enumerate(heads):
            outs[h] = o[j * t:(j + 1) * t]
    chunks = []
    for c in range(N_HEADS // 2):
        halves = []
        for h in (2 * c, 2 * c + 1):
            halves.append(outs[h] if h // Q_PER_KV == h % 2 else pltpu.roll(outs[h], HEAD_DIM, axis=1))
        chunks.append(jnp.where(low, halves[0], halves[1]))
    return jnp.concatenate(chunks, axis=1)


def _ctx_attn_kernel(q_ref, k_ref, v_ref, sink_ref, kin_ref, vin_ref, o_ref, kout_ref, vout_ref):
    del kin_ref, vin_ref
    k, v = k_ref[...], v_ref[...]
    kout_ref[...] = k
    vout_ref[...] = v
    o_ref[...] = _gqa(q_ref[...], [k], [v], [None], sink_ref)


def _context_attention(zm, sink, layer, new_k, new_v):
    slab = pl.BlockSpec((None, None, SEQ, KV_WIDTH), lambda b: (b, layer, 0, 0))
    return pl.pallas_call(
        _ctx_attn_kernel,
        grid=(BATCH,),
        in_specs=[
            pl.BlockSpec((SEQ, ATTN_WIDTH), lambda b: (b, SPLIT_Q // ATTN_WIDTH)),
            pl.BlockSpec((SEQ, KV_WIDTH), lambda b: (b, SPLIT_K // KV_WIDTH)),
            pl.BlockSpec((SEQ, KV_WIDTH), lambda b: (b, SPLIT_V // KV_WIDTH)),
            pl.BlockSpec(memory_space=pltpu.SMEM),
            pl.BlockSpec(memory_space=pl.ANY),
            pl.BlockSpec(memory_space=pl.ANY),
        ],
        out_specs=[pl.BlockSpec((SEQ, ATTN_WIDTH), lambda b: (b, 0)), slab, slab],
        out_shape=[jax.ShapeDtypeStruct((N_PROMPT_TOK, ATTN_WIDTH), F32),
                   jax.ShapeDtypeStruct(new_k.shape, F32), jax.ShapeDtypeStruct(new_v.shape, F32)],
        input_output_aliases={4: 1, 5: 2},
        compiler_params=_cparams("arbitrary"),
        name="ctx_attention",
    )(zm, zm, zm, sink, new_k, new_v)


def _latent_attn_kernel(q_ref, kl_ref, kc_ref, kr_ref, vl_ref, vc_ref, vr_ref, ck_ref, cv_ref, cos_ref, sin_ref,
                        sink_ref, o_ref):
    i = pl.program_id(1)
    nb = DEC_SEQ // BLOCK

    def table(ref, blk):
        return ref[pl.ds(pl.multiple_of(blk * BLOCK, BLOCK), BLOCK), :]

    left, right = jnp.maximum(i - 1, 0), jnp.minimum(i + 1, nb - 1)
    cos_q, sin_q = table(cos_ref, i), table(sin_ref, i)
    q = jnp.concatenate([_rope(q_ref[:, c:c + LANE], cos_q, sin_q) for c in range(0, ATTN_WIDTH, LANE)], axis=1)
    kw = jnp.concatenate([
        _rope(kl_ref[...], table(cos_ref, left), table(sin_ref, left)),
        _rope(kc_ref[...], cos_q, sin_q),
        _rope(kr_ref[...], table(cos_ref, right), table(sin_ref, right)),
    ], axis=0)
    vw = jnp.concatenate([vl_ref[...], vc_ref[...], vr_ref[...]], axis=0)

    m_rows = Q_PER_KV * BLOCK
    r = lax.broadcasted_iota(jnp.int32, (m_rows, 3 * BLOCK), 0) % BLOCK
    c = lax.broadcasted_iota(jnp.int32, (m_rows, 3 * BLOCK), 1)
    kpos = c + (i - 1) * BLOCK
    win = (kpos >= 0) & (kpos < DEC_SEQ) & (jnp.abs(r + BLOCK - c) <= WINDOW)

    o_ref[...] = _gqa(q, [ck_ref[...], kw], [cv_ref[...], vw], [None, win], sink_ref)


def _latent_attention(zm, cache_k, cache_v, layer, cos, sin, sink):
    nb = DEC_SEQ // BLOCK
    blk0 = N_PROMPT_TOK // BLOCK
    kcol, vcol = SPLIT_K // KV_WIDTH, SPLIT_V // KV_WIDTH

    def row(b, i):
        return blk0 + b * nb + i

    def win_specs(col):
        return [
            pl.BlockSpec((BLOCK, KV_WIDTH), lambda b, i: (row(b, jnp.maximum(i - 1, 0)), col)),
            pl.BlockSpec((BLOCK, KV_WIDTH), lambda b, i: (row(b, i), col)),
            pl.BlockSpec((BLOCK, KV_WIDTH), lambda b, i: (row(b, jnp.minimum(i + 1, nb - 1)), col)),
        ]

    ctx_spec = pl.BlockSpec((None, None, PAST_LEN, KV_WIDTH), lambda b, i: (b, layer, 0, 0))
    tab_spec = pl.BlockSpec((DEC_SEQ, LANE), lambda b, i: (0, 0))
    return pl.pallas_call(
        _latent_attn_kernel,
        grid=(DEC_BATCH, nb),
        in_specs=[pl.BlockSpec((BLOCK, ATTN_WIDTH), lambda b, i: (row(b, i), SPLIT_Q // ATTN_WIDTH))]
        + win_specs(kcol) + win_specs(vcol)
        + [ctx_spec, ctx_spec, tab_spec, tab_spec, pl.BlockSpec(memory_space=pltpu.SMEM)],
        out_specs=pl.BlockSpec((BLOCK, ATTN_WIDTH), lambda b, i: (b * nb + i, 0)),
        out_shape=jax.ShapeDtypeStruct((N_SAMPLE_TOK, ATTN_WIDTH), F32),
        compiler_params=_cparams("arbitrary", "arbitrary"),
        name="latent_attention",
    )(zm, zm, zm, zm, zm, zm, zm, cache_k, cache_v, cos, sin, sink)


def kernel(x_prompt, x_sample, cache_k, cache_v, state_s5_re, state_s5_im, c, c_ctx, ada_w, ada_b, norm1_g, w_in, hy_conv_w, hy_conv_b, hy_pos_w1, hy_pos_b1, hy_pos_w2, hy_pos_b2, hy_pos_w3, hy_sin_freq, hy_decay, hy_skip, s5_lam_re, s5_lam_im, s5_log_step, s5_b_re, s5_b_im, s5_c_re, s5_c_im, s5_skip, s5_glu_w, s5_glu_b, attn_sink, proj_a, proj_b, proj_c, w_out, norm2_g, ffn_w_gate, ffn_w_up, ffn_w_down, final_norm_g):
    cvec = jnp.concatenate([c_ctx[None], c, jnp.zeros((MOD_ROWS - 1 - DEC_BATCH, D_MODEL), F32)], axis=0)
    mod = _modulation(cvec, ada_w, ada_b).reshape(DEPTH * MOD_ROWS * 6, 1, D_MODEL)

    x = jnp.concatenate([x_prompt.reshape(N_PROMPT_TOK, D_MODEL), x_sample.reshape(N_SAMPLE_TOK, D_MODEL)], axis=0)
    w_main = w_in[:, :, :MAIN_IN].astype(BF16)
    w_gate = w_in[:, :, MAIN_IN:].astype(BF16)
    pa, pb, pc, wo = proj_a.astype(BF16), proj_b.astype(BF16), proj_c.astype(BF16), w_out.astype(BF16)
    wg, wu, wd = ffn_w_gate.astype(BF16), ffn_w_up.astype(BF16), ffn_w_down.astype(BF16)
    g1, g2 = norm1_g.reshape(DEPTH, 1, D_MODEL), norm2_g.reshape(DEPTH, 1, D_MODEL)
    glu_w = s5_glu_w.astype(BF16)

    ctx_k = cache_k.reshape(DEC_BATCH, DEPTH, PAST_LEN, KV_WIDTH)
    ctx_v = cache_v.reshape(DEC_BATCH, DEPTH, PAST_LEN, KV_WIDTH)
    rope_cos, rope_sin = _rope_tables(DEC_SEQ)
    mats_p, mats_s = _dft_mats(min(SEQ, HY_BLOCK)), _dft_mats(min(DEC_SEQ, HY_BLOCK))

    new_k = jnp.zeros((BATCH, DEPTH, SEQ, KV_WIDTH), F32)
    new_v = jnp.zeros((BATCH, DEPTH, SEQ, KV_WIDTH), F32)
    sre_out, sim_out = [], []
    for l in range(DEPTH):
        filt = (hy_pos_w1[l], hy_pos_b1[l], hy_pos_w2[l], hy_pos_b2[l], hy_pos_w3[l], hy_sin_freq[l], hy_decay[l])
        zm = _in_proj(x, mod, l, g1, w_main)
        u, vb = _short_conv(zm, hy_conv_w[l], hy_conv_b[l].reshape(1, HY_IN))
        ya_p = _hyena_path(u, vb, 0, BATCH, SEQ, mats_p, _hyena_tables(SEQ, mats_p, filt), hy_skip[l])
        ya_s = _hyena_path(u, vb, N_PROMPT_TOK, DEC_BATCH, DEC_SEQ, mats_s, _hyena_tables(DEC_SEQ, mats_s, filt),
                           hy_skip[l])
        yc_p, new_k, new_v = _context_attention(zm, attn_sink[l], l, new_k, new_v)
        yc_s = _latent_attention(zm, ctx_k, ctx_v, l, rope_cos, rope_sin, attn_sink[l])

        lam, bd, cd = _s5_params(s5_lam_re[l], s5_lam_im[l], s5_log_step[l], s5_b_re[l], s5_b_im[l],
                                 s5_c_re[l], s5_c_im[l])
        s5_w = (lam, bd, cd, s5_skip[l].reshape(1, S5_WIDTH), glu_w[l], s5_glu_b[l].reshape(1, 2 * S5_WIDTH))
        h0_s = jnp.stack([state_s5_re[:, l, 0], state_s5_im[:, l, 0], state_s5_re[:, l, 1], state_s5_im[:, l, 1]],
                         axis=1).reshape(DEC_BATCH, 4, S5_LANES)
        yb_p, fin_p = _s5_branch(zm, 0, BATCH, SEQ, SEQ, jnp.zeros((BATCH, 4, S5_LANES), F32), *s5_w)
        yb_s, _ = _s5_branch(zm, N_PROMPT_TOK, DEC_BATCH, DEC_SEQ, 512, h0_s, *s5_w)
        fin_p = fin_p.reshape(BATCH, 2, 2, S5_GROUPS, S5_STATE)

        sre_out.append(fin_p[:, :, 0])
        sim_out.append(fin_p[:, :, 1])
        x = _merge((ya_p, ya_s), (yb_p, yb_s), (yc_p, yc_s), x, mod, l, g1, w_gate, pa, pb, pc, wo)
        x = _ffn(x, mod, l, g2, wg, wu, wd)

    y_prompt, y_sample = _final_norm(x, final_norm_g.reshape(1, D_MODEL))
    return (y_prompt.reshape(BATCH, SEQ, D_MODEL), y_sample.reshape(DEC_BATCH, DEC_SEQ, D_MODEL),
            new_k.reshape(BATCH, DEPTH, SEQ, N_KV_HEADS, HEAD_DIM), new_v.reshape(BATCH, DEPTH, SEQ, N_KV_HEADS, HEAD_DIM),
            jnp.stack(sre_out, axis=1), jnp.stack(sim_out, axis=1))
```

```python
import functools
import math

import jax
import jax.numpy as jnp
import numpy as np
from jax import lax
from jax.experimental import pallas as pl
from jax.experimental.pallas import tpu as pltpu

D_MODEL = 1024
BATCH = 16
SEQ = 256
DEPTH = 4
DEC_BATCH = 2
DEC_SEQ = 4096
PAST_LEN = 512
GRID_W = 64
N_BRANCH = 3
HY_WIDTH = 256
HY_ORDER = 2
HY_SHORT_K = 3
HY_POS_EMB = 33
HY_POS_BANDS = (HY_POS_EMB - 1) // 2
HY_FILTER_HIDDEN = 64
S5_WIDTH = 256
S5_GROUP = 16
S5_GROUPS = S5_WIDTH // S5_GROUP
S5_STATE = 64
N_HEADS = 8
N_KV_HEADS = 2
Q_PER_KV = N_HEADS // N_KV_HEADS
HEAD_DIM = 64
ATTN_WIDTH = N_HEADS * HEAD_DIM
KV_WIDTH = N_KV_HEADS * HEAD_DIM
WINDOW = 128
BLOCK = 128
ROPE_BASE = 10000.0
FFN_HIDDEN = ((8 * D_MODEL // 3 + 255) // 256) * 256
HY_IN = (HY_ORDER + 1) * HY_WIDTH
GATE_IN = N_BRANCH * D_MODEL
MAIN_IN = HY_IN + S5_WIDTH + ATTN_WIDTH + 2 * KV_WIDTH
SPLIT_S5 = HY_IN
SPLIT_Q = SPLIT_S5 + S5_WIDTH
SPLIT_K = SPLIT_Q + ATTN_WIDTH
SPLIT_V = SPLIT_K + KV_WIDTH

F32 = jnp.float32
BF16 = jnp.bfloat16
EPS = 1e-6
NEG_INF = -1e30

N_PROMPT_TOK = BATCH * SEQ
N_SAMPLE_TOK = DEC_BATCH * DEC_SEQ
N_TOK = N_PROMPT_TOK + N_SAMPLE_TOK
LANE = 128
MOD_ROWS = 8
TM = 512
VMEM_LIMIT = 56 * 1024 * 1024


def _cparams(*sem):
    return pltpu.CompilerParams(dimension_semantics=sem, vmem_limit_bytes=VMEM_LIMIT)


def _mod_row(i):
    n_p = N_PROMPT_TOK // TM
    per_b = DEC_SEQ // TM
    return jnp.where(i < n_p, 0, 1 + (i - n_p) // per_b)


def _mod_spec(layer, k):
    return pl.BlockSpec((None, 1, D_MODEL), lambda i: ((layer * MOD_ROWS + _mod_row(i)) * 6 + k, 0, 0))


def _resident(shape, layer=None):
    if layer is None:
        return pl.BlockSpec(shape, lambda i: (0,) * len(shape), pipeline_mode=pl.Buffered(1))
    return pl.BlockSpec((None,) + shape, lambda i: (layer,) + (0,) * len(shape), pipeline_mode=pl.Buffered(1))


def _rms(x, g):
    return x * lax.rsqrt(jnp.mean(x * x, axis=-1, keepdims=True) + EPS) * g


def _mod_kernel(c_ref, w_ref, b_ref, o_ref):
    c = c_ref[...]
    s = (c * jax.nn.sigmoid(c)).astype(BF16)
    o_ref[...] = jnp.dot(s, w_ref[...].astype(BF16), preferred_element_type=F32) + b_ref[...]


def _modulation(cvec, ada_w, ada_b):
    tn = 1536
    return pl.pallas_call(
        _mod_kernel,
        grid=(DEPTH, 6 * D_MODEL // tn),
        in_specs=[
            pl.BlockSpec((MOD_ROWS, D_MODEL), lambda l, j: (0, 0)),
            pl.BlockSpec((None, D_MODEL, tn), lambda l, j: (l, 0, j)),
            pl.BlockSpec((None, 1, tn), lambda l, j: (l, 0, j)),
        ],
        out_specs=pl.BlockSpec((None, MOD_ROWS, tn), lambda l, j: (l, 0, j)),
        out_shape=jax.ShapeDtypeStruct((DEPTH, MOD_ROWS, 6 * D_MODEL), F32),
        compiler_params=_cparams("arbitrary", "arbitrary"),
        name="adaln_mod",
    )(cvec, ada_w, ada_b.reshape(DEPTH, 1, 6 * D_MODEL))


def _inproj_kernel(x_ref, sh_ref, sc_ref, g_ref, wm_ref, zm_ref):
    h = (_rms(x_ref[...], g_ref[...]) * (1.0 + sc_ref[...]) + sh_ref[...]).astype(BF16)
    zm_ref[...] = jnp.dot(h, wm_ref[...], preferred_element_type=F32)


def _in_proj(x, mod, layer, g, w_main):
    return pl.pallas_call(
        _inproj_kernel,
        grid=(N_TOK // TM,),
        in_specs=[
            pl.BlockSpec((TM, D_MODEL), lambda i: (i, 0)),
            _mod_spec(layer, 0),
            _mod_spec(layer, 1),
            _resident((1, D_MODEL), layer),
            _resident((D_MODEL, MAIN_IN), layer),
        ],
        out_specs=pl.BlockSpec((TM, MAIN_IN), lambda i: (i, 0)),
        out_shape=jax.ShapeDtypeStruct((N_TOK, MAIN_IN), F32),
        compiler_params=_cparams("arbitrary"),
        name="in_proj",
    )(x, mod, mod, g, w_main)


N_PROMPT_TILES = N_PROMPT_TOK // TM


def _path_specs(width):
    return [pl.BlockSpec((TM, width), lambda i: (jnp.minimum(i, N_PROMPT_TILES - 1), 0)),
            pl.BlockSpec((TM, width), lambda i: (jnp.maximum(i - N_PROMPT_TILES, 0), 0))]


def _merge_kernel(yap_ref, yas_ref, ybp_ref, ybs_ref, ycp_ref, ycs_ref, x_ref, sh_ref, sc_ref, g1_ref, g_ref,
                  wg_ref, pa_ref, pb_ref, pc_ref, wo_ref, o_ref):
    is_prompt = pl.program_id(0) < N_PROMPT_TILES
    x = x_ref[...]
    h = (_rms(x, g_ref[...]) * (1.0 + sc_ref[...]) + sh_ref[...]).astype(BF16)
    branches = ((yap_ref, yas_ref, pa_ref), (ybp_ref, ybs_ref, pb_ref), (ycp_ref, ycs_ref, pc_ref))
    m = jnp.zeros((TM, D_MODEL), F32)
    for k, (p_ref, s_ref, w_ref) in enumerate(branches):
        y = jnp.where(is_prompt, p_ref[...], s_ref[...]).astype(BF16)
        gate = jax.nn.sigmoid(jnp.dot(h, wg_ref[:, k * D_MODEL:(k + 1) * D_MODEL], preferred_element_type=F32))
        m = m + gate * jnp.dot(y, w_ref[...], preferred_element_type=F32)
    o_ref[...] = x + g1_ref[...] * jnp.dot(m.astype(BF16), wo_ref[...], preferred_element_type=F32)


def _merge(ya, yb, yc, x, mod, layer, g, w_gate, pa, pb, pc, wo):
    return pl.pallas_call(
        _merge_kernel,
        grid=(N_TOK // TM,),
        in_specs=_path_specs(HY_WIDTH) + _path_specs(S5_WIDTH) + _path_specs(ATTN_WIDTH) + [
            pl.BlockSpec((TM, D_MODEL), lambda i: (i, 0)),
            _mod_spec(layer, 0),
            _mod_spec(layer, 1),
            _mod_spec(layer, 2),
            _resident((1, D_MODEL), layer),
            _resident((D_MODEL, GATE_IN), layer),
            _resident((HY_WIDTH, D_MODEL), layer),
            _resident((S5_WIDTH, D_MODEL), layer),
            _resident((ATTN_WIDTH, D_MODEL), layer),
            _resident((D_MODEL, D_MODEL), layer),
        ],
        out_specs=pl.BlockSpec((TM, D_MODEL), lambda i: (i, 0)),
        out_shape=jax.ShapeDtypeStruct((N_TOK, D_MODEL), F32),
        compiler_params=_cparams("arbitrary"),
        name="merge_out",
    )(*ya, *yb, *yc, x, mod, mod, mod, g, w_gate, pa, pb, pc, wo)


def _ffn_kernel(x_ref, sh_ref, sc_ref, gt_ref, g_ref, wg_ref, wu_ref, wd_ref, o_ref):
    x = x_ref[...]
    h = (_rms(x, g_ref[...]) * (1.0 + sc_ref[...]) + sh_ref[...]).astype(BF16)
    a = jnp.dot(h, wg_ref[...], preferred_element_type=F32)
    b = jnp.dot(h, wu_ref[...], preferred_element_type=F32)
    f = (a * jax.nn.sigmoid(a) * b).astype(BF16)
    o_ref[...] = x + gt_ref[...] * jnp.dot(f, wd_ref[...], preferred_element_type=F32)


def _ffn(x, mod, layer, g, wg, wu, wd):
    return pl.pallas_call(
        _ffn_kernel,
        grid=(N_TOK // TM,),
        in_specs=[
            pl.BlockSpec((TM, D_MODEL), lambda i: (i, 0)),
            _mod_spec(layer, 3),
            _mod_spec(layer, 4),
            _mod_spec(layer, 5),
            _resident((1, D_MODEL), layer),
            _resident((D_MODEL, FFN_HIDDEN), layer),
            _resident((D_MODEL, FFN_HIDDEN), layer),
            _resident((FFN_HIDDEN, D_MODEL), layer),
        ],
        out_specs=pl.BlockSpec((TM, D_MODEL), lambda i: (i, 0)),
        out_shape=jax.ShapeDtypeStruct((N_TOK, D_MODEL), F32),
        compiler_params=_cparams("arbitrary"),
        name="ffn",
    )(x, mod, mod, mod, g, wg, wu, wd)


def _final_norm_kernel(x_ref, g_ref, op_ref, os_ref):
    y = _rms(x_ref[...], g_ref[...])
    is_prompt = pl.program_id(0) < N_PROMPT_TILES

    @pl.when(is_prompt)
    def _():
        op_ref[...] = y

    @pl.when(jnp.logical_not(is_prompt))
    def _():
        os_ref[...] = y


def _final_norm(x, g):
    return pl.pallas_call(
        _final_norm_kernel,
        grid=(N_TOK // TM,),
        in_specs=[pl.BlockSpec((TM, D_MODEL), lambda i: (i, 0)), _resident((1, D_MODEL))],
        out_specs=_path_specs(D_MODEL),
        out_shape=[jax.ShapeDtypeStruct((N_PROMPT_TOK, D_MODEL), F32),
                   jax.ShapeDtypeStruct((N_SAMPLE_TOK, D_MODEL), F32)],
        compiler_params=_cparams("arbitrary"),
        name="final_norm",
    )(x, g)


SUBLANE = 8
HY_TILE = 1024
assert N_PROMPT_TOK % HY_TILE == 0 and N_SAMPLE_TOK % HY_TILE == 0
assert SEQ & (SEQ - 1) == 0 and DEC_SEQ & (DEC_SEQ - 1) == 0
HY_BLOCK = 512
HY_ROWS = 64


def _dft_mats(blk):
    n = 2 * blk
    idx = jnp.arange(blk, dtype=jnp.int32)
    ang = (2.0 * math.pi / n) * ((idx[:, None] * idx[None, :]) % n).astype(F32)
    alt = jnp.where(idx % 2 == 0, 1.0, -1.0).astype(F32)
    return jnp.stack([jnp.cos(ang), jnp.where((idx == 0)[:, None], alt[None, :], -jnp.sin(ang))]).astype(BF16)


def _short_conv_kernel(x_ref, xp_ref, xn_ref, w_ref, b_ref, u_ref, vb_ref):
    i = pl.program_id(0)
    seq = jnp.where(i < N_PROMPT_TOK // HY_TILE, SEQ, DEC_SEQ)
    x = x_ref[...]
    row = lax.broadcasted_iota(jnp.int32, (HY_TILE, 1), 0)
    pos = (i * HY_TILE + row) & (seq - 1)
    x_prev = jnp.where(row == 0, xp_ref[SUBLANE - 1:SUBLANE, :], pltpu.roll(x, 1, axis=0))
    x_prev = jnp.where(pos == 0, 0.0, x_prev)
    x_next = jnp.where(row == HY_TILE - 1, xn_ref[0:1, :], pltpu.roll(x, HY_TILE - 1, axis=0))
    x_next = jnp.where(pos == seq - 1, 0.0, x_next)
    u = b_ref[...] + x_prev * w_ref[0:1, :] + x * w_ref[1:2, :] + x_next * w_ref[2:3, :]
    u_ref[...] = u
    vb_ref[...] = u[:, :HY_WIDTH].astype(BF16)


def _short_conv(zm, w, b):
    per8 = HY_TILE // SUBLANE
    last8 = N_TOK // SUBLANE - 1
    return pl.pallas_call(
        _short_conv_kernel,
        grid=(N_TOK // HY_TILE,),
        in_specs=[
            pl.BlockSpec((HY_TILE, HY_IN), lambda i: (i, 0)),
            pl.BlockSpec((SUBLANE, HY_IN), lambda i: (jnp.maximum(i * per8 - 1, 0), 0)),
            pl.BlockSpec((SUBLANE, HY_IN), lambda i: (jnp.minimum((i + 1) * per8, last8), 0)),
            pl.BlockSpec((HY_SHORT_K, HY_IN), lambda i: (0, 0)),
            pl.BlockSpec((1, HY_IN), lambda i: (0, 0)),
        ],
        out_specs=[
            pl.BlockSpec((HY_TILE, HY_IN), lambda i: (i, 0)),
            pl.BlockSpec((HY_TILE, HY_WIDTH), lambda i: (i, 0)),
        ],
        out_shape=[
            jax.ShapeDtypeStruct((N_TOK, HY_IN), F32),
            jax.ShapeDtypeStruct((N_TOK, HY_WIDTH), BF16),
        ],
        compiler_params=_cparams("arbitrary"),
        name="hyena_short_conv",
    )(zm, zm, zm, w, b)


def _block_conv_kernel(n_blk, last, zin_ref, k_ref, f_ref, g_ref, zp_ref, skip_ref, o_ref, *rest):
    z_ref, s_ref = rest[-2:]
    n = pl.program_id(1)
    blk = f_ref.shape[-1]

    @pl.when(n == 0)
    def _spectra():
        group = min(n_blk, 8)
        for g0 in range(0, n_blk, group):
            z = jnp.concatenate([zin_ref[(g0 + g) * blk:(g0 + g + 1) * blk, :] for g in range(group)], axis=1)
            for c in range(2):
                x = jnp.dot(f_ref[c], z, preferred_element_type=F32)
                for g in range(group):
                    z_ref[g0 + g, c] = x[:, g * HY_WIDTH:(g + 1) * HY_WIDTH]

    def product(r, first_row_packed):
        acc_r = acc_i = jnp.zeros((r.size, HY_WIDTH), F32)
        fix_r = fix_i = acc_r
        for i in range(n_blk):
            d = n - i + n_blk
            zr, zi = z_ref[i, 0, r, :], z_ref[i, 1, r, :]
            kr, ki = k_ref[d, 0, r, :], k_ref[d, 1, r, :]
            acc_r = acc_r + (zr * kr - zi * ki)
            acc_i = acc_i + (zr * ki + zi * kr)
            if first_row_packed:
                fix_r = fix_r + zr * kr
                fix_i = fix_i + zi * ki
        if first_row_packed:
            row0 = lax.broadcasted_iota(jnp.int32, (r.size, 1), 0) == 0
            acc_r = jnp.where(row0, fix_r, acc_r)
            acc_i = jnp.where(row0, fix_i, acc_i)
        s_ref[0, r, :] = acc_r.astype(BF16)
        s_ref[1, r, :] = acc_i.astype(BF16)

    def chunk(c, carry):
        product(pl.ds(pl.multiple_of(c * HY_ROWS, HY_ROWS), HY_ROWS), False)
        return carry

    lax.fori_loop(0, blk // HY_ROWS, chunk, 0)
    product(pl.ds(0, 2 * SUBLANE), True)

    conv_c = jnp.dot(f_ref[0], s_ref[0], preferred_element_type=F32)
    conv_s = jnp.dot(f_ref[1], s_ref[1], preferred_element_type=F32)
    t = lax.broadcasted_iota(jnp.int32, (blk, 1), 0)
    alt = jnp.where(t % 2 == 0, 1.0, -1.0).astype(F32)
    conv = conv_c + jnp.where(t == 0, 0.0, conv_s) + alt * s_ref[1, 0:1, :].astype(F32)
    z = g_ref[...] * (conv + skip_ref[...] * zp_ref[...])
    o_ref[...] = z
    if not last:
        rest[0][...] = z.astype(BF16)


def _block_conv(zb, zb_row0, tables, order, mats, u, gate_col, z_prev, z_prev_row0, skip, row0, n_seq, seq_len, last):
    blk = mats.shape[-1]
    n_blk = seq_len // blk

    def rows(base):
        return lambda b, n: (base // blk + b * n_blk + n, 0)

    out_spec = pl.BlockSpec((blk, HY_WIDTH), rows(0))
    out_shape = [jax.ShapeDtypeStruct((n_seq * n_blk * blk, HY_WIDTH), F32)]
    if not last:
        out_shape.append(jax.ShapeDtypeStruct((n_seq * n_blk * blk, HY_WIDTH), BF16))
    return pl.pallas_call(
        functools.partial(_block_conv_kernel, n_blk, last),
        grid=(n_seq, n_blk),
        in_specs=[
            pl.BlockSpec((seq_len, HY_WIDTH), lambda b, n: (zb_row0 // seq_len + b, 0)),
            pl.BlockSpec((None,) + tables.shape[1:], lambda b, n: (order, 0, 0, 0, 0), pipeline_mode=pl.Buffered(1)),
            pl.BlockSpec((2, blk, blk), lambda b, n: (0, 0, 0)),
            pl.BlockSpec((blk, HY_WIDTH), lambda b, n: (row0 // blk + b * n_blk + n, gate_col)),
            pl.BlockSpec((blk, HY_WIDTH), rows(z_prev_row0)),
            pl.BlockSpec((1, HY_WIDTH), lambda b, n: (0, 0)),
        ],
        out_specs=[out_spec] * len(out_shape),
        out_shape=out_shape,
        scratch_shapes=[pltpu.VMEM((n_blk, 2, blk, HY_WIDTH), F32), pltpu.VMEM((2, blk, HY_WIDTH), BF16)],
        compiler_params=_cparams("arbitrary", "arbitrary"),
        name="hyena_block_conv",
    )(zb, tables, mats, u, z_prev, skip)


def _taps_kernel(seq_len, rows, w1_ref, b1_ref, w2_ref, b2_ref, w3_ref, freq_ref, decay_ref, k_ref, norm_ref):
    i = pl.program_id(0)
    lag = i * rows + lax.broadcasted_iota(jnp.int32, (rows, 1), 0) - seq_len
    t = jnp.abs(lag).astype(F32) * (1.0 / (seq_len - 1))
    pos = jnp.abs(i * rows + lax.broadcasted_iota(jnp.int32, (1, rows), 1) - seq_len).astype(F32)
    band = lax.broadcasted_iota(jnp.int32, (HY_POS_BANDS, 1), 0).astype(F32)
    band = 1e-4 + band * ((HY_POS_BANDS - 1 - 1e-4) / (HY_POS_BANDS - 1))
    ang = band * (pos * (2.0 * math.pi / seq_len))
    dot = functools.partial(jnp.dot, precision=lax.Precision.HIGHEST, preferred_element_type=F32)
    pre = w1_ref[:, 0:1] * (pos * (1.0 / (seq_len - 1)))
    pre = pre + dot(w1_ref[:, 1:1 + HY_POS_BANDS], jnp.cos(ang))
    pre = pre - dot(w1_ref[:, 1 + HY_POS_BANDS:], jnp.sin(ang))
    freq = freq_ref[...]
    h = jnp.sin(freq * (pre + b1_ref[...]))
    h = jnp.sin(freq * (dot(w2_ref[...], h) + b2_ref[...]))
    h = lax.dot_general(h.astype(BF16), w3_ref[...].astype(BF16), (((0,), (0,)), ((), ())),
                        preferred_element_type=F32)
    h = h * jnp.exp(-t * jnp.abs(decay_ref[...]))

    @pl.when(i == 0)
    def _():
        norm_ref[...] = jnp.zeros_like(norm_ref)

    for o in range(HY_ORDER):
        c0 = o * 2 * HY_WIDTH
        k = jnp.where(lag >= 0, h[:, c0:c0 + HY_WIDTH], h[:, c0 + HY_WIDTH:c0 + 2 * HY_WIDTH])
        k = jnp.where(lag == -seq_len, 0.0, k)
        k_ref[o] = k.astype(BF16)
        norm_ref[:, o * HY_WIDTH:(o + 1) * HY_WIDTH] += jnp.sum(jnp.abs(k), axis=0, keepdims=True)


def _hyena_taps(seq_len, w1, b1, w2, b2, w3, freq, decay):
    rows = min(2 * seq_len, 512)
    full = lambda a: pl.BlockSpec(a.shape, lambda i: (0,) * a.ndim)
    args = (w1.T, b1.reshape(-1, 1), w2.T, b2.reshape(-1, 1), w3, freq.reshape(-1, 1), decay.reshape(1, -1))
    return pl.pallas_call(
        functools.partial(_taps_kernel, seq_len, rows),
        grid=(2 * seq_len // rows,),
        in_specs=[full(a) for a in args],
        out_specs=[pl.BlockSpec((HY_ORDER, rows, HY_WIDTH), lambda i: (0, i, 0)),
                   pl.BlockSpec((1, HY_ORDER * HY_WIDTH), lambda i: (0, 0))],
        out_shape=[jax.ShapeDtypeStruct((HY_ORDER, 2 * seq_len, HY_WIDTH), BF16),
                   jax.ShapeDtypeStruct((1, HY_ORDER * HY_WIDTH), F32)],
        compiler_params=_cparams("arbitrary"),
        name="hyena_taps",
    )(*args)


def _tables_kernel(group, x_ref, norm_ref, f_ref, o_ref, prev_ref):
    blk = f_ref.shape[-1]

    @pl.when(pl.program_id(1) == 0)
    def _():
        prev_ref[...] = jnp.zeros_like(prev_ref)

    f = lax.broadcasted_iota(jnp.int32, (blk, 1), 0)
    alt = jnp.where(f % 2 == 0, 1.0, -1.0).astype(F32)
    scale = jnp.where(f == 0, 1.0, 2.0).astype(F32) * (1.0 / (2 * blk)) / norm_ref[...]
    x = jnp.concatenate([x_ref[g * blk:(g + 1) * blk, :] for g in range(group)], axis=1)
    for c in range(2):
        xc = jnp.dot(f_ref[c], x, preferred_element_type=F32)
        prev = prev_ref[c]
        for g in range(group):
            cur = xc[:, g * HY_WIDTH:(g + 1) * HY_WIDTH]
            o_ref[g, c] = (cur + alt * prev) * scale
            prev = cur
        prev_ref[c] = prev


def _hyena_tables(seq_len, mats, filt):
    blk = mats.shape[-1]
    n_lag = 2 * seq_len // blk
    group = min(n_lag, 8)
    taps, norm = _hyena_taps(seq_len, *filt)
    return pl.pallas_call(
        functools.partial(_tables_kernel, group),
        grid=(HY_ORDER, n_lag // group),
        in_specs=[
            pl.BlockSpec((None, group * blk, HY_WIDTH), lambda o, q: (o, q, 0)),
            pl.BlockSpec((1, HY_WIDTH), lambda o, q: (0, o)),
            pl.BlockSpec((2, blk, blk), lambda o, q: (0, 0, 0)),
        ],
        out_specs=pl.BlockSpec((None, group, 2, blk, HY_WIDTH), lambda o, q: (o, q, 0, 0, 0)),
        out_shape=jax.ShapeDtypeStruct((HY_ORDER, n_lag, 2, blk, HY_WIDTH), F32),
        scratch_shapes=[pltpu.VMEM((2, blk, HY_WIDTH), F32)],
        compiler_params=_cparams("arbitrary", "arbitrary"),
        name="hyena_tables",
    )(taps, norm, mats)


def _hyena_path(u, vb, row0, n_seq, seq_len, mats, tables, skip):
    z1, z1b = _block_conv(vb, row0, tables, 0, mats, u, 1, u, row0, skip[0:1], row0, n_seq, seq_len, False)
    (ya,) = _block_conv(z1b, 0, tables, 1, mats, u, 2, z1, 0, skip[1:2], row0, n_seq, seq_len, True)
    return ya


S5_LANES = S5_GROUPS * S5_STATE
S5_SEGS = 8
S5_CHUNKS = 4


def _s5_kernel(n_tiles, tile, u0_ref, u1_ref, h0_ref, lam_ref, bd_ref, cd_ref, skip_ref, gw_ref, gb_ref,
               y_ref, fin_ref, bu_ref, ybwd_ref, p_ref, p8_ref, cm_ref, fl_ref, carry_ref, up_ref, st0_ref, st1_ref,
               ub_ref, yacc_ref):
    seg = tile // S5_SEGS
    n = S5_LANES
    b = pl.program_id(0)
    j = pl.program_id(1)

    @pl.when((b == 0) & (j == 0))
    def _powers():
        for d in range(2):
            lr = lam_ref[2 * d:2 * d + 1, :]
            li = lam_ref[2 * d + 1:2 * d + 2, :]

            def put(k, pr, pi, d=d):
                p_ref[d, pl.ds(k, 1), 0:n] = pr
                p_ref[d, pl.ds(k, 1), n:2 * n] = pi
                r = pl.ds(pl.multiple_of(k * S5_SEGS, S5_SEGS), S5_SEGS)
                p8_ref[d, r, 0:n] = jnp.broadcast_to(pr, (S5_SEGS, n))
                p8_ref[d, r, n:2 * n] = jnp.broadcast_to(pi, (S5_SEGS, n))

            put(0, lr, li)

            def body(k, c, lr=lr, li=li, put=put):
                pr, pi = c
                nr = pr * lr - pi * li
                ni = pr * li + pi * lr
                put(k, nr, ni)
                return nr, ni

            lax.fori_loop(1, seg, body, (lr, li))

    def sweep(d, t):
        first, last = (0, n_tiles - 1) if d == 0 else (n_tiles - 1, 0)

        @pl.when(t == first)
        def _init():
            carry_ref[0:1, 0:n] = h0_ref[2 * d:2 * d + 1, :]
            carry_ref[0:1, n:2 * n] = h0_ref[2 * d + 1:2 * d + 2, :]

        def gather(i, c):
            r = pl.ds(pl.multiple_of(i * S5_SEGS, S5_SEGS), S5_SEGS)
            up_ref[r, 0:LANE] = u0_ref[pl.ds(i, S5_SEGS, stride=seg), :]
            up_ref[r, LANE:2 * LANE] = u1_ref[pl.ds(i, S5_SEGS, stride=seg), :]
            return c

        lax.fori_loop(0, seg, gather, 0, unroll=2)
        ub_ref[...] = up_ref[...].astype(BF16)
        nq = n // S5_CHUNKS

        def lanes(q):
            return slice(q * nq, (q + 1) * nq), slice(n + q * nq, n + (q + 1) * nq)

        def rows(i):
            k = i if d == 0 else seg - 1 - i
            return slice(k * S5_SEGS, (k + 1) * S5_SEGS)

        def project_in(q):
            re, im = lanes(q)
            bu_ref[:, re] = jnp.dot(ub_ref[...], bd_ref[d, :, re], preferred_element_type=F32)
            bu_ref[:, im] = jnp.dot(ub_ref[...], bd_ref[d, :, im], preferred_element_type=F32)

        def scan(q):
            re, im = lanes(q)
            lr8 = jnp.broadcast_to(lam_ref[2 * d:2 * d + 1, re], (S5_SEGS, nq))
            li8 = jnp.broadcast_to(lam_ref[2 * d + 1:2 * d + 2, re], (S5_SEGS, nq))
            hr = hi = jnp.zeros((S5_SEGS, nq), F32)
            for i in range(seg):
                r = rows(i)
                hr, hi = lr8 * hr - li8 * hi + bu_ref[r, re], lr8 * hi + li8 * hr + bu_ref[r, im]
                bu_ref[r, re] = hr
                bu_ref[r, im] = hi
            fl_ref[:, re] = hr
            fl_ref[:, im] = hi

            plr = p_ref[d, seg - 1:seg, re]
            pli = p_ref[d, seg - 1:seg, im]
            cr = carry_ref[0:1, re]
            ci = carry_ref[0:1, im]
            for s in (range(S5_SEGS) if d == 0 else reversed(range(S5_SEGS))):
                cm_ref[s:s + 1, re] = cr
                cm_ref[s:s + 1, im] = ci
                cr, ci = (plr * cr - pli * ci + fl_ref[s:s + 1, re], plr * ci + pli * cr + fl_ref[s:s + 1, im])
            carry_ref[0:1, re] = cr
            carry_ref[0:1, im] = ci

            cmr = cm_ref[:, re]
            cmi = cm_ref[:, im]
            for i in range(seg):
                r = rows(i)
                pw = slice(i * S5_SEGS, (i + 1) * S5_SEGS)
                pr = p8_ref[d, pw, re]
                pi = p8_ref[d, pw, im]
                bu_ref[r, re] = bu_ref[r, re] + (pr * cmr - pi * cmi)
                bu_ref[r, im] = bu_ref[r, im] + (pr * cmi + pi * cmr)

        def project_out(q):
            re, im = lanes(q)
            part = (jnp.dot(bu_ref[:, re].astype(BF16), cd_ref[d, re, :], preferred_element_type=F32)
                    + jnp.dot(bu_ref[:, im].astype(BF16), cd_ref[d, im, :], preferred_element_type=F32))
            yacc_ref[...] = part if q == 0 else yacc_ref[...] + part

        project_in(0)
        for q in range(S5_CHUNKS):
            if q + 1 < S5_CHUNKS:
                project_in(q + 1)
            if q > 0:
                project_out(q - 1)
            scan(q)
        project_out(S5_CHUNKS - 1)
        y = yacc_ref[...]

        @pl.when(t == last)
        def _final():
            fin_ref[2 * d:2 * d + 1, :] = carry_ref[0:1, 0:n]
            fin_ref[2 * d + 1:2 * d + 2, :] = carry_ref[0:1, n:2 * n]

        trow = pl.ds(pl.multiple_of(t * tile, tile), tile)
        if d == 1:
            ybwd_ref[trow, :] = y
        else:
            tot = up_ref[...] * skip_ref[...] + y + ybwd_ref[trow, :]
            g = jnp.dot(tot.astype(BF16), gw_ref[...], preferred_element_type=F32) + gb_ref[...]
            res = g[:, :S5_WIDTH] * jax.nn.sigmoid(g[:, S5_WIDTH:])
            st0_ref[...] = res[:, 0:LANE]
            st1_ref[...] = res[:, LANE:2 * LANE]
            per_seg = seg // S5_SEGS

            def unpermute(m, c):
                s = m // per_seg
                i0 = (m % per_seg) * S5_SEGS
                src = pl.ds(i0 * S5_SEGS + s, S5_SEGS, stride=S5_SEGS)
                dst = pl.ds(pl.multiple_of(m * S5_SEGS, S5_SEGS), S5_SEGS)
                y_ref[dst, 0:LANE] = st0_ref[src, :]
                y_ref[dst, LANE:2 * LANE] = st1_ref[src, :]
                return c

            lax.fori_loop(0, tile // S5_SEGS, unpermute, 0, unroll=2)

    @pl.when(j < n_tiles)
    def _bwd():
        sweep(1, n_tiles - 1 - j)

    @pl.when(j >= n_tiles)
    def _fwd():
        sweep(0, j - n_tiles)


def _s5_branch(zm, row0, n_seq, seq_len, tile, h0, lam, bd, cd, skip, gw, gb):
    n_tiles = seq_len // tile
    blk0 = row0 // tile
    col = SPLIT_S5 // LANE
    seg = tile // S5_SEGS

    def tile_of(j):
        return jnp.where(j < n_tiles, n_tiles - 1 - j, j - n_tiles)

    return pl.pallas_call(
        functools.partial(_s5_kernel, n_tiles, tile),
        grid=(n_seq, 2 * n_tiles),
        in_specs=[
            pl.BlockSpec((tile, LANE), lambda b, j: (blk0 + b * n_tiles + tile_of(j), col)),
            pl.BlockSpec((tile, LANE), lambda b, j: (blk0 + b * n_tiles + tile_of(j), col + 1)),
            pl.BlockSpec((None, 4, S5_LANES), lambda b, j: (b, 0, 0)),
            pl.BlockSpec((4, S5_LANES), lambda b, j: (0, 0)),
            pl.BlockSpec((2, S5_WIDTH, 2 * S5_LANES), lambda b, j: (0, 0, 0)),
            pl.BlockSpec((2, 2 * S5_LANES, S5_WIDTH), lambda b, j: (0, 0, 0)),
            pl.BlockSpec((1, S5_WIDTH), lambda b, j: (0, 0)),
            pl.BlockSpec((S5_WIDTH, 2 * S5_WIDTH), lambda b, j: (0, 0)),
            pl.BlockSpec((1, 2 * S5_WIDTH), lambda b, j: (0, 0)),
        ],
        out_specs=[
            pl.BlockSpec((tile, S5_WIDTH), lambda b, j: (b * n_tiles + jnp.maximum(j - n_tiles, 0), 0)),
            pl.BlockSpec((None, 4, S5_LANES), lambda b, j: (b, 0, 0)),
        ],
        out_shape=[
            jax.ShapeDtypeStruct((n_seq * seq_len, S5_WIDTH), F32),
            jax.ShapeDtypeStruct((n_seq, 4, S5_LANES), F32),
        ],
        scratch_shapes=[
            pltpu.VMEM((tile, 2 * S5_LANES), F32),
            pltpu.VMEM((seq_len, S5_WIDTH), F32),
            pltpu.VMEM((2, seg, 2 * S5_LANES), F32),
            pltpu.VMEM((2, tile, 2 * S5_LANES), F32),
            pltpu.VMEM((S5_SEGS, 2 * S5_LANES), F32),
            pltpu.VMEM((S5_SEGS, 2 * S5_LANES), F32),
            pltpu.VMEM((1, 2 * S5_LANES), F32),
            pltpu.VMEM((tile, S5_WIDTH), F32),
            pltpu.VMEM((tile, LANE), F32),
            pltpu.VMEM((tile, LANE), F32),
            pltpu.VMEM((tile, S5_WIDTH), BF16),
            pltpu.VMEM((tile, S5_WIDTH), F32),
        ],
        compiler_params=_cparams("arbitrary", "arbitrary"),
        name="s5_scan",
    )(zm, zm, h0, lam, bd, cd, skip, gw, gb)


def _s5_params(lam_re, lam_im, log_step, b_re, b_im, c_re, c_im):
    step = jnp.exp(log_step)[..., None]
    mag = jnp.exp(lam_re * step)
    bar_re, bar_im = mag * jnp.cos(lam_im * step), mag * jnp.sin(lam_im * step)
    den = lam_re * lam_re + lam_im * lam_im
    q_re = ((bar_re - 1.0) * lam_re + bar_im * lam_im) / den
    q_im = (bar_im * lam_re - (bar_re - 1.0) * lam_im) / den
    bb_re = q_re[..., None] * b_re - q_im[..., None] * b_im
    bb_im = q_re[..., None] * b_im + q_im[..., None] * b_re
    eye = jnp.eye(S5_GROUPS, dtype=F32)

    def in_mat(x):
        xt = jnp.swapaxes(x, 2, 3)
        return (xt[:, :, :, None, :] * eye[None, :, None, :, None]).reshape(2, S5_WIDTH, S5_LANES)

    def out_mat(x):
        xt = jnp.swapaxes(x, 2, 3)
        return (xt[:, :, :, None, :] * eye[None, :, None, :, None]).reshape(2, S5_LANES, S5_WIDTH)

    bd = jnp.concatenate([in_mat(bb_re), in_mat(bb_im)], axis=2).astype(BF16)
    cd = jnp.concatenate([out_mat(c_re), out_mat(-c_im)], axis=1).astype(BF16)
    lam_rows = jnp.stack([bar_re[0], bar_im[0], bar_re[1], bar_im[1]])
    return lam_rows.reshape(4, S5_LANES), bd, cd


ROPE_HALF = HEAD_DIM // 4


def _rope_tables(seq_len):
    t = jnp.arange(seq_len)
    row = (t // GRID_W).astype(F32)
    col = (t % GRID_W).astype(F32)
    inv = ROPE_BASE ** (-jnp.arange(ROPE_HALF, dtype=F32) / ROPE_HALF)
    ang_r = row[:, None] * inv
    ang_c = col[:, None] * inv
    cos = jnp.concatenate([jnp.cos(ang_r), jnp.cos(ang_r), jnp.cos(ang_c), jnp.cos(ang_c)], axis=-1)
    sin = jnp.concatenate([-jnp.sin(ang_r), jnp.sin(ang_r), -jnp.sin(ang_c), jnp.sin(ang_c)], axis=-1)
    return jnp.tile(cos, (1, LANE // HEAD_DIM)), jnp.tile(sin, (1, LANE // HEAD_DIM))


def _rope(x, cos, sin):
    lane = lax.broadcasted_iota(jnp.int32, x.shape, 1)
    first = (lane % (2 * ROPE_HALF)) < ROPE_HALF
    partner = jnp.where(first, pltpu.roll(x, LANE - ROPE_HALF, axis=1), pltpu.roll(x, ROPE_HALF, axis=1))
    return x * cos + partner * sin


def _gqa(q, ks, vs, masks, sink_ref):
    t = q.shape[0]
    low = lax.broadcasted_iota(jnp.int32, (1, LANE), 1) < HEAD_DIM
    assert N_KV_HEADS * HEAD_DIM == LANE
    k_nat = [k.astype(BF16) for k in ks]
    k_swp = [pltpu.roll(k, HEAD_DIM, axis=1).astype(BF16) for k in ks]
    v_aug = [jnp.concatenate([v, jnp.ones_like(v)], axis=1).astype(BF16) for v in vs]
    outs = {}
    for keys, heads in ((k_nat, [h for h in range(N_HEADS) if h % 2 == h // Q_PER_KV]),
                        (k_swp, [h for h in range(N_HEADS) if h % 2 != h // Q_PER_KV])):
        scale = HEAD_DIM ** -0.5
        assert math.frexp(scale)[0] == 0.5
        rows = [jnp.where(low if h % 2 == 0 else jnp.logical_not(low),
                          q[:, (h // 2) * LANE:(h // 2 + 1) * LANE] * scale, 0.0) for h in heads]
        qs = jnp.concatenate(rows, axis=0).astype(BF16)
        sink_col = jnp.concatenate([jnp.full((t, 1), sink_ref[h], F32) for h in heads], axis=0)
        ss = []
        m = sink_col
        for k, mk in zip(keys, masks):
            s = lax.dot_general(qs, k, (((1,), (1,)), ((), ())), preferred_element_type=F32)
            if mk is not None:
                s = jnp.where(mk, s, NEG_INF)
            ss.append(s)
            m = jnp.maximum(m, jnp.max(s, axis=-1, keepdims=True))
        acc = jnp.zeros((len(heads) * t, 2 * LANE), F32)
        for s, v in zip(ss, v_aug):
            acc = acc + jnp.dot(jnp.exp(s - m).astype(BF16), v, preferred_element_type=F32)
        o = acc[:, :LANE] / (acc[:, LANE:LANE + 1] + jnp.exp(sink_col - m))
        for j, h in enumerate(heads):
            outs[h] = o[j * t:(j + 1) * t]
    chunks = []
    for c in range(N_HEADS // 2):
        halves = []
        for h in (2 * c, 2 * c + 1):
            halves.append(outs[h] if h // Q_PER_KV == h % 2 else pltpu.roll(outs[h], HEAD_DIM, axis=1))
        chunks.append(jnp.where(low, halves[0], halves[1]))
    return jnp.concatenate(chunks, axis=1)


def _ctx_attn_kernel(q_ref, k_ref, v_ref, sink_ref, kin_ref, vin_ref, o_ref, kout_ref, vout_ref):
    del kin_ref, vin_ref
    k, v = k_ref[...], v_ref[...]
    kout_ref[...] = k
    vout_ref[...] = v
    o_ref[...] = _gqa(q_ref[...], [k], [v], [None], sink_ref)


def _context_attention(zm, sink, layer, new_k, new_v):
    slab = pl.BlockSpec((None, None, SEQ, KV_WIDTH), lambda b: (b, layer, 0, 0))
    return pl.pallas_call(
        _ctx_attn_kernel,
        grid=(BATCH,),
        in_specs=[
            pl.BlockSpec((SEQ, ATTN_WIDTH), lambda b: (b, SPLIT_Q // ATTN_WIDTH)),
            pl.BlockSpec((SEQ, KV_WIDTH), lambda b: (b, SPLIT_K // KV_WIDTH)),
            pl.BlockSpec((SEQ, KV_WIDTH), lambda b: (b, SPLIT_V // KV_WIDTH)),
            pl.BlockSpec(memory_space=pltpu.SMEM),
            pl.BlockSpec(memory_space=pl.ANY),
            pl.BlockSpec(memory_space=pl.ANY),
        ],
        out_specs=[pl.BlockSpec((SEQ, ATTN_WIDTH), lambda b: (b, 0)), slab, slab],
        out_shape=[jax.ShapeDtypeStruct((N_PROMPT_TOK, ATTN_WIDTH), F32),
                   jax.ShapeDtypeStruct(new_k.shape, F32), jax.ShapeDtypeStruct(new_v.shape, F32)],
        input_output_aliases={4: 1, 5: 2},
        compiler_params=_cparams("arbitrary"),
        name="ctx_attention",
    )(zm, zm, zm, sink, new_k, new_v)


def _latent_attn_kernel(q_ref, kl_ref, kc_ref, kr_ref, vl_ref, vc_ref, vr_ref, ck_ref, cv_ref, cos_ref, sin_ref,
                        sink_ref, o_ref):
    i = pl.program_id(1)
    nb = DEC_SEQ // BLOCK

    def table(ref, blk):
        return ref[pl.ds(pl.multiple_of(blk * BLOCK, BLOCK), BLOCK), :]

    left, right = jnp.maximum(i - 1, 0), jnp.minimum(i + 1, nb - 1)
    cos_q, sin_q = table(cos_ref, i), table(sin_ref, i)
    q = jnp.concatenate([_rope(q_ref[:, c:c + LANE], cos_q, sin_q) for c in range(0, ATTN_WIDTH, LANE)], axis=1)
    kw = jnp.concatenate([
        _rope(kl_ref[...], table(cos_ref, left), table(sin_ref, left)),
        _rope(kc_ref[...], cos_q, sin_q),
        _rope(kr_ref[...], table(cos_ref, right), table(sin_ref, right)),
    ], axis=0)
    vw = jnp.concatenate([vl_ref[...], vc_ref[...], vr_ref[...]], axis=0)

    m_rows = Q_PER_KV * BLOCK
    r = lax.broadcasted_iota(jnp.int32, (m_rows, 3 * BLOCK), 0) % BLOCK
    c = lax.broadcasted_iota(jnp.int32, (m_rows, 3 * BLOCK), 1)
    kpos = c + (i - 1) * BLOCK
    win = (kpos >= 0) & (kpos < DEC_SEQ) & (jnp.abs(r + BLOCK - c) <= WINDOW)

    o_ref[...] = _gqa(q, [ck_ref[...], kw], [cv_ref[...], vw], [None, win], sink_ref)


def _latent_attention(zm, cache_k, cache_v, layer, cos, sin, sink):
    nb = DEC_SEQ // BLOCK
    blk0 = N_PROMPT_TOK // BLOCK
    kcol, vcol = SPLIT_K // KV_WIDTH, SPLIT_V // KV_WIDTH

    def row(b, i):
        return blk0 + b * nb + i

    def win_specs(col):
        return [
            pl.BlockSpec((BLOCK, KV_WIDTH), lambda b, i: (row(b, jnp.maximum(i - 1, 0)), col)),
            pl.BlockSpec((BLOCK, KV_WIDTH), lambda b, i: (row(b, i), col)),
            pl.BlockSpec((BLOCK, KV_WIDTH), lambda b, i: (row(b, jnp.minimum(i + 1, nb - 1)), col)),
        ]

    ctx_spec = pl.BlockSpec((None, None, PAST_LEN, KV_WIDTH), lambda b, i: (b, layer, 0, 0))
    tab_spec = pl.BlockSpec((DEC_SEQ, LANE), lambda b, i: (0, 0))
    return pl.pallas_call(
        _latent_attn_kernel,
        grid=(DEC_BATCH, nb),
        in_specs=[pl.BlockSpec((BLOCK, ATTN_WIDTH), lambda b, i: (row(b, i), SPLIT_Q // ATTN_WIDTH))]
        + win_specs(kcol) + win_specs(vcol)
        + [ctx_spec, ctx_spec, tab_spec, tab_spec, pl.BlockSpec(memory_space=pltpu.SMEM)],
        out_specs=pl.BlockSpec((BLOCK, ATTN_WIDTH), lambda b, i: (b * nb + i, 0)),
        out_shape=jax.ShapeDtypeStruct((N_SAMPLE_TOK, ATTN_WIDTH), F32),
        compiler_params=_cparams("arbitrary", "arbitrary"),
        name="latent_attention",
    )(zm, zm, zm, zm, zm, zm, zm, cache_k, cache_v, cos, sin, sink)


def kernel(x_prompt, x_sample, cache_k, cache_v, state_s5_re, state_s5_im, c, c_ctx, ada_w, ada_b, norm1_g, w_in, hy_conv_w, hy_conv_b, hy_pos_w1, hy_pos_b1, hy_pos_w2, hy_pos_b2, hy_pos_w3, hy_sin_freq, hy_decay, hy_skip, s5_lam_re, s5_lam_im, s5_log_step, s5_b_re, s5_b_im, s5_c_re, s5_c_im, s5_skip, s5_glu_w, s5_glu_b, attn_sink, proj_a, proj_b, proj_c, w_out, norm2_g, ffn_w_gate, ffn_w_up, ffn_w_down, final_norm_g):
    cvec = jnp.concatenate([c_ctx[None], c, jnp.zeros((MOD_ROWS - 1 - DEC_BATCH, D_MODEL), F32)], axis=0)
    mod = _modulation(cvec, ada_w, ada_b).reshape(DEPTH * MOD_ROWS * 6, 1, D_MODEL)

    x = jnp.concatenate([x_prompt.reshape(N_PROMPT_TOK, D_MODEL), x_sample.reshape(N_SAMPLE_TOK, D_MODEL)], axis=0)
    w_main = w_in[:, :, :MAIN_IN].astype(BF16)
    w_gate = w_in[:, :, MAIN_IN:].astype(BF16)
    pa, pb, pc, wo = proj_a.astype(BF16), proj_b.astype(BF16), proj_c.astype(BF16), w_out.astype(BF16)
    wg, wu, wd = ffn_w_gate.astype(BF16), ffn_w_up.astype(BF16), ffn_w_down.astype(BF16)
    g1, g2 = norm1_g.reshape(DEPTH, 1, D_MODEL), norm2_g.reshape(DEPTH, 1, D_MODEL)
    glu_w = s5_glu_w.astype(BF16)

    ctx_k = cache_k.reshape(DEC_BATCH, DEPTH, PAST_LEN, KV_WIDTH)
    ctx_v = cache_v.reshape(DEC_BATCH, DEPTH, PAST_LEN, KV_WIDTH)
    rope_cos, rope_sin = _rope_tables(DEC_SEQ)
    mats_p, mats_s = _dft_mats(min(SEQ, HY_BLOCK)), _dft_mats(min(DEC_SEQ, HY_BLOCK))

    new_k = jnp.zeros((BATCH, DEPTH, SEQ, KV_WIDTH), F32)
    new_v = jnp.zeros((BATCH, DEPTH, SEQ, KV_WIDTH), F32)
    sre_out, sim_out = [], []
    for l in range(DEPTH):
        filt = (hy_pos_w1[l], hy_pos_b1[l], hy_pos_w2[l], hy_pos_b2[l], hy_pos_w3[l], hy_sin_freq[l], hy_decay[l])
        zm = _in_proj(x, mod, l, g1, w_main)
        u, vb = _short_conv(zm, hy_conv_w[l], hy_conv_b[l].reshape(1, HY_IN))
        ya_p = _hyena_path(u, vb, 0, BATCH, SEQ, mats_p, _hyena_tables(SEQ, mats_p, filt), hy_skip[l])
        ya_s = _hyena_path(u, vb, N_PROMPT_TOK, DEC_BATCH, DEC_SEQ, mats_s, _hyena_tables(DEC_SEQ, mats_s, filt),
                           hy_skip[l])
        yc_p, new_k, new_v = _context_attention(zm, attn_sink[l], l, new_k, new_v)
        yc_s = _latent_attention(zm, ctx_k, ctx_v, l, rope_cos, rope_sin, attn_sink[l])

        lam, bd, cd = _s5_params(s5_lam_re[l], s5_lam_im[l], s5_log_step[l], s5_b_re[l], s5_b_im[l],
                                 s5_c_re[l], s5_c_im[l])
        s5_w = (lam, bd, cd, s5_skip[l].reshape(1, S5_WIDTH), glu_w[l], s5_glu_b[l].reshape(1, 2 * S5_WIDTH))
        h0_s = jnp.stack([state_s5_re[:, l, 0], state_s5_im[:, l, 0], state_s5_re[:, l, 1], state_s5_im[:, l, 1]],
                         axis=1).reshape(DEC_BATCH, 4, S5_LANES)
        yb_p, fin_p = _s5_branch(zm, 0, BATCH, SEQ, SEQ, jnp.zeros((BATCH, 4, S5_LANES), F32), *s5_w)
        yb_s, _ = _s5_branch(zm, N_PROMPT_TOK, DEC_BATCH, DEC_SEQ, 512, h0_s, *s5_w)
        fin_p = fin_p.reshape(BATCH, 2, 2, S5_GROUPS, S5_STATE)

        sre_out.append(fin_p[:, :, 0])
        sim_out.append(fin_p[:, :, 1])
        x = _merge((ya_p, ya_s), (yb_p, yb_s), (yc_p, yc_s), x, mod, l, g1, w_gate, pa, pb, pc, wo)
        x = _ffn(x, mod, l, g2, wg, wu, wd)

    y_prompt, y_sample = _final_norm(x, final_norm_g.reshape(1, D_MODEL))
    return (y_prompt.reshape(BATCH, SEQ, D_MODEL), y_sample.reshape(DEC_BATCH, DEC_SEQ, D_MODEL),
            new_k.reshape(BATCH, DEPTH, SEQ, N_KV_HEADS, HEAD_DIM), new_v.reshape(BATCH, DEPTH, SEQ, N_KV_HEADS, HEAD_DIM),
            jnp.stack(sre_out, axis=1), jnp.stack(sim_out, axis=1))
```

```python
import functools
import math

import jax
import jax.numpy as jnp
import numpy as np
from jax import lax
from jax.experimental import pallas as pl
from jax.experimental.pallas import tpu as pltpu

D_MODEL = 1024
BATCH = 16
SEQ = 256
DEPTH = 4
DEC_BATCH = 2
DEC_SEQ = 4096
PAST_LEN = 512
GRID_W = 64
N_BRANCH = 3
HY_WIDTH = 256
HY_ORDER = 2
HY_SHORT_K = 3
HY_POS_EMB = 33
HY_POS_BANDS = (HY_POS_EMB - 1) // 2
HY_FILTER_HIDDEN = 64
S5_WIDTH = 256
S5_GROUP = 16
S5_GROUPS = S5_WIDTH // S5_GROUP
S5_STATE = 64
N_HEADS = 8
N_KV_HEADS = 2
Q_PER_KV = N_HEADS // N_KV_HEADS
HEAD_DIM = 64
ATTN_WIDTH = N_HEADS * HEAD_DIM
KV_WIDTH = N_KV_HEADS * HEAD_DIM
WINDOW = 128
BLOCK = 128
ROPE_BASE = 10000.0
FFN_HIDDEN = ((8 * D_MODEL // 3 + 255) // 256) * 256
HY_IN = (HY_ORDER + 1) * HY_WIDTH
GATE_IN = N_BRANCH * D_MODEL
MAIN_IN = HY_IN + S5_WIDTH + ATTN_WIDTH + 2 * KV_WIDTH
SPLIT_S5 = HY_IN
SPLIT_Q = SPLIT_S5 + S5_WIDTH
SPLIT_K = SPLIT_Q + ATTN_WIDTH
SPLIT_V = SPLIT_K + KV_WIDTH

F32 = jnp.float32
BF16 = jnp.bfloat16
EPS = 1e-6
NEG_INF = -1e30

N_PROMPT_TOK = BATCH * SEQ
N_SAMPLE_TOK = DEC_BATCH * DEC_SEQ
N_TOK = N_PROMPT_TOK + N_SAMPLE_TOK
LANE = 128
MOD_ROWS = 8
TM = 512
VMEM_LIMIT = 56 * 1024 * 1024


def _cparams(*sem):
    return pltpu.CompilerParams(dimension_semantics=sem, vmem_limit_bytes=VMEM_LIMIT)


def _mod_row(i):
    n_p = N_PROMPT_TOK // TM
    per_b = DEC_SEQ // TM
    return jnp.where(i < n_p, 0, 1 + (i - n_p) // per_b)


def _mod_spec(layer, k):
    return pl.BlockSpec((None, 1, D_MODEL), lambda i: ((layer * MOD_ROWS + _mod_row(i)) * 6 + k, 0, 0))


def _resident(shape, layer=None):
    if layer is None:
        return pl.BlockSpec(shape, lambda i: (0,) * len(shape), pipeline_mode=pl.Buffered(1))
    return pl.BlockSpec((None,) + shape, lambda i: (layer,) + (0,) * len(shape), pipeline_mode=pl.Buffered(1))


def _rms(x, g):
    return x * lax.rsqrt(jnp.mean(x * x, axis=-1, keepdims=True) + EPS) * g


def _mod_kernel(c_ref, w_ref, b_ref, o_ref):
    c = c_ref[...]
    s = (c * jax.nn.sigmoid(c)).astype(BF16)
    o_ref[...] = jnp.dot(s, w_ref[...].astype(BF16), preferred_element_type=F32) + b_ref[...]


def _modulation(cvec, ada_w, ada_b):
    tn = 1536
    return pl.pallas_call(
        _mod_kernel,
        grid=(DEPTH, 6 * D_MODEL // tn),
        in_specs=[
            pl.BlockSpec((MOD_ROWS, D_MODEL), lambda l, j: (0, 0)),
            pl.BlockSpec((None, D_MODEL, tn), lambda l, j: (l, 0, j)),
            pl.BlockSpec((None, 1, tn), lambda l, j: (l, 0, j)),
        ],
        out_specs=pl.BlockSpec((None, MOD_ROWS, tn), lambda l, j: (l, 0, j)),
        out_shape=jax.ShapeDtypeStruct((DEPTH, MOD_ROWS, 6 * D_MODEL), F32),
        compiler_params=_cparams("arbitrary", "arbitrary"),
        name="adaln_mod",
    )(cvec, ada_w, ada_b.reshape(DEPTH, 1, 6 * D_MODEL))


def _inproj_kernel(x_ref, sh_ref, sc_ref, g_ref, wm_ref, zm_ref):
    h = (_rms(x_ref[...], g_ref[...]) * (1.0 + sc_ref[...]) + sh_ref[...]).astype(BF16)
    zm_ref[...] = jnp.dot(h, wm_ref[...], preferred_element_type=F32)


def _in_proj(x, mod, layer, g, w_main):
    return pl.pallas_call(
        _inproj_kernel,
        grid=(N_TOK // TM,),
        in_specs=[
            pl.BlockSpec((TM, D_MODEL), lambda i: (i, 0)),
            _mod_spec(layer, 0),
            _mod_spec(layer, 1),
            _resident((1, D_MODEL), layer),
            _resident((D_MODEL, MAIN_IN), layer),
        ],
        out_specs=pl.BlockSpec((TM, MAIN_IN), lambda i: (i, 0)),
        out_shape=jax.ShapeDtypeStruct((N_TOK, MAIN_IN), F32),
        compiler_params=_cparams("arbitrary"),
        name="in_proj",
    )(x, mod, mod, g, w_main)


N_PROMPT_TILES = N_PROMPT_TOK // TM


def _path_specs(width):
    return [pl.BlockSpec((TM, width), lambda i: (jnp.minimum(i, N_PROMPT_TILES - 1), 0)),
            pl.BlockSpec((TM, width), lambda i: (jnp.maximum(i - N_PROMPT_TILES, 0), 0))]


def _merge_kernel(yap_ref, yas_ref, ybp_ref, ybs_ref, ycp_ref, ycs_ref, x_ref, sh_ref, sc_ref, g1_ref, g_ref,
                  wg_ref, pa_ref, pb_ref, pc_ref, wo_ref, o_ref):
    is_prompt = pl.program_id(0) < N_PROMPT_TILES
    x = x_ref[...]
    h = (_rms(x, g_ref[...]) * (1.0 + sc_ref[...]) + sh_ref[...]).astype(BF16)
    branches = ((yap_ref, yas_ref, pa_ref), (ybp_ref, ybs_ref, pb_ref), (ycp_ref, ycs_ref, pc_ref))
    m = jnp.zeros((TM, D_MODEL), F32)
    for k, (p_ref, s_ref, w_ref) in enumerate(branches):
        y = jnp.where(is_prompt, p_ref[...], s_ref[...]).astype(BF16)
        gate = jax.nn.sigmoid(jnp.dot(h, wg_ref[:, k * D_MODEL:(k + 1) * D_MODEL], preferred_element_type=F32))
        m = m + gate * jnp.dot(y, w_ref[...], preferred_element_type=F32)
    o_ref[...] = x + g1_ref[...] * jnp.dot(m.astype(BF16), wo_ref[...], preferred_element_type=F32)


def _merge(ya, yb, yc, x, mod, layer, g, w_gate, pa, pb, pc, wo):
    return pl.pallas_call(
        _merge_kernel,
        grid=(N_TOK // TM,),
        in_specs=_path_specs(HY_WIDTH) + _path_specs(S5_WIDTH) + _path_specs(ATTN_WIDTH) + [
            pl.BlockSpec((TM, D_MODEL), lambda i: (i, 0)),
            _mod_spec(layer, 0),
            _mod_spec(layer, 1),
            _mod_spec(layer, 2),
            _resident((1, D_MODEL), layer),
            _resident((D_MODEL, GATE_IN), layer),
            _resident((HY_WIDTH, D_MODEL), layer),
            _resident((S5_WIDTH, D_MODEL), layer),
            _resident((ATTN_WIDTH, D_MODEL), layer),
            _resident((D_MODEL, D_MODEL), layer),
        ],
        out_specs=pl.BlockSpec((TM, D_MODEL), lambda i: (i, 0)),
        out_shape=jax.ShapeDtypeStruct((N_TOK, D_MODEL), F32),
        compiler_params=_cparams("arbitrary"),
        name="merge_out",
    )(*ya, *yb, *yc, x, mod, mod, mod, g, w_gate, pa, pb, pc, wo)


def _ffn_kernel(x_ref, sh_ref, sc_ref, gt_ref, g_ref, wg_ref, wu_ref, wd_ref, o_ref):
    x = x_ref[...]
    h = (_rms(x, g_ref[...]) * (1.0 + sc_ref[...]) + sh_ref[...]).astype(BF16)
    a = jnp.dot(h, wg_ref[...], preferred_element_type=F32)
    b = jnp.dot(h, wu_ref[...], preferred_element_type=F32)
    f = (a * jax.nn.sigmoid(a) * b).astype(BF16)
    o_ref[...] = x + gt_ref[...] * jnp.dot(f, wd_ref[...], preferred_element_type=F32)


def _ffn(x, mod, layer, g, wg, wu, wd):
    return pl.pallas_call(
        _ffn_kernel,
        grid=(N_TOK // TM,),
        in_specs=[
            pl.BlockSpec((TM, D_MODEL), lambda i: (i, 0)),
            _mod_spec(layer, 3),
            _mod_spec(layer, 4),
            _mod_spec(layer, 5),
            _resident((1, D_MODEL), layer),
            _resident((D_MODEL, FFN_HIDDEN), layer),
            _resident((D_MODEL, FFN_HIDDEN), layer),
            _resident((FFN_HIDDEN, D_MODEL), layer),
        ],
        out_specs=pl.BlockSpec((TM, D_MODEL), lambda i: (i, 0)),
        out_shape=jax.ShapeDtypeStruct((N_TOK, D_MODEL), F32),
        compiler_params=_cparams("arbitrary"),
        name="ffn",
    )(x, mod, mod, mod, g, wg, wu, wd)


def _final_norm_kernel(x_ref, g_ref, op_ref, os_ref):
    y = _rms(x_ref[...], g_ref[...])
    is_prompt = pl.program_id(0) < N_PROMPT_TILES

    @pl.when(is_prompt)
    def _():
        op_ref[...] = y

    @pl.when(jnp.logical_not(is_prompt))
    def _():
        os_ref[...] = y


def _final_norm(x, g):
    return pl.pallas_call(
        _final_norm_kernel,
        grid=(N_TOK // TM,),
        in_specs=[pl.BlockSpec((TM, D_MODEL), lambda i: (i, 0)), _resident((1, D_MODEL))],
        out_specs=_path_specs(D_MODEL),
        out_shape=[jax.ShapeDtypeStruct((N_PROMPT_TOK, D_MODEL), F32),
                   jax.ShapeDtypeStruct((N_SAMPLE_TOK, D_MODEL), F32)],
        compiler_params=_cparams("arbitrary"),
        name="final_norm",
    )(x, g)


SUBLANE = 8
HY_TILE = 1024
assert N_PROMPT_TOK % HY_TILE == 0 and N_SAMPLE_TOK % HY_TILE == 0
assert SEQ & (SEQ - 1) == 0 and DEC_SEQ & (DEC_SEQ - 1) == 0
HY_BLOCK = 512
HY_ROWS = 64


def _dft_mats(blk):
    n = 2 * blk
    idx = jnp.arange(blk, dtype=jnp.int32)
    ang = (2.0 * math.pi / n) * ((idx[:, None] * idx[None, :]) % n).astype(F32)
    alt = jnp.where(idx % 2 == 0, 1.0, -1.0).astype(F32)
    return jnp.stack([jnp.cos(ang), jnp.where((idx == 0)[:, None], alt[None, :], -jnp.sin(ang))]).astype(BF16)


def _short_conv_kernel(x_ref, xp_ref, xn_ref, w_ref, b_ref, u_ref, vb_ref):
    i = pl.program_id(0)
    seq = jnp.where(i < N_PROMPT_TOK // HY_TILE, SEQ, DEC_SEQ)
    x = x_ref[...]
    row = lax.broadcasted_iota(jnp.int32, (HY_TILE, 1), 0)
    pos = (i * HY_TILE + row) & (seq - 1)
    x_prev = jnp.where(row == 0, xp_ref[SUBLANE - 1:SUBLANE, :], pltpu.roll(x, 1, axis=0))
    x_prev = jnp.where(pos == 0, 0.0, x_prev)
    x_next = jnp.where(row == HY_TILE - 1, xn_ref[0:1, :], pltpu.roll(x, HY_TILE - 1, axis=0))
    x_next = jnp.where(pos == seq - 1, 0.0, x_next)
    u = b_ref[...] + x_prev * w_ref[0:1, :] + x * w_ref[1:2, :] + x_next * w_ref[2:3, :]
    u_ref[...] = u
    vb_ref[...] = u[:, :HY_WIDTH].astype(BF16)


def _short_conv(zm, w, b):
    per8 = HY_TILE // SUBLANE
    last8 = N_TOK // SUBLANE - 1
    return pl.pallas_call(
        _short_conv_kernel,
        grid=(N_TOK // HY_TILE,),
        in_specs=[
            pl.BlockSpec((HY_TILE, HY_IN), lambda i: (i, 0)),
            pl.BlockSpec((SUBLANE, HY_IN), lambda i: (jnp.maximum(i * per8 - 1, 0), 0)),
            pl.BlockSpec((SUBLANE, HY_IN), lambda i: (jnp.minimum((i + 1) * per8, last8), 0)),
            pl.BlockSpec((HY_SHORT_K, HY_IN), lambda i: (0, 0)),
            pl.BlockSpec((1, HY_IN), lambda i: (0, 0)),
        ],
        out_specs=[
            pl.BlockSpec((HY_TILE, HY_IN), lambda i: (i, 0)),
            pl.BlockSpec((HY_TILE, HY_WIDTH), lambda i: (i, 0)),
        ],
        out_shape=[
            jax.ShapeDtypeStruct((N_TOK, HY_IN), F32),
            jax.ShapeDtypeStruct((N_TOK, HY_WIDTH), BF16),
        ],
        compiler_params=_cparams("arbitrary"),
        name="hyena_short_conv",
    )(zm, zm, zm, w, b)


def _block_conv_kernel(n_blk, last, zin_ref, k_ref, f_ref, g_ref, zp_ref, skip_ref, o_ref, *rest):
    z_ref, s_ref = rest[-2:]
    n = pl.program_id(1)
    blk = f_ref.shape[-1]

    @pl.when(n == 0)
    def _spectra():
        group = min(n_blk, 8)
        for g0 in range(0, n_blk, group):
            z = jnp.concatenate([zin_ref[(g0 + g) * blk:(g0 + g + 1) * blk, :] for g in range(group)], axis=1)
            for c in range(2):
                x = jnp.dot(f_ref[c], z, preferred_element_type=F32)
                for g in range(group):
                    z_ref[g0 + g, c] = x[:, g * HY_WIDTH:(g + 1) * HY_WIDTH]

    def product(r, first_row_packed):
        acc_r = acc_i = jnp.zeros((r.size, HY_WIDTH), F32)
        fix_r = fix_i = acc_r
        for i in range(n_blk):
            d = n - i + n_blk
            zr, zi = z_ref[i, 0, r, :], z_ref[i, 1, r, :]
            kr, ki = k_ref[d, 0, r, :], k_ref[d, 1, r, :]
            acc_r = acc_r + (zr * kr - zi * ki)
            acc_i = acc_i + (zr * ki + zi * kr)
            if first_row_packed:
                fix_r = fix_r + zr * kr
                fix_i = fix_i + zi * ki
        if first_row_packed:
            row0 = lax.broadcasted_iota(jnp.int32, (r.size, 1), 0) == 0
            acc_r = jnp.where(row0, fix_r, acc_r)
            acc_i = jnp.where(row0, fix_i, acc_i)
        s_ref[0, r, :] = acc_r.astype(BF16)
        s_ref[1, r, :] = acc_i.astype(BF16)

    def chunk(c, carry):
        product(pl.ds(pl.multiple_of(c * HY_ROWS, HY_ROWS), HY_ROWS), False)
        return carry

    lax.fori_loop(0, blk // HY_ROWS, chunk, 0)
    product(pl.ds(0, 2 * SUBLANE), True)

    conv_c = jnp.dot(f_ref[0], s_ref[0], preferred_element_type=F32)
    conv_s = jnp.dot(f_ref[1], s_ref[1], preferred_element_type=F32)
    t = lax.broadcasted_iota(jnp.int32, (blk, 1), 0)
    alt = jnp.where(t % 2 == 0, 1.0, -1.0).astype(F32)
    conv = conv_c + jnp.where(t == 0, 0.0, conv_s) + alt * s_ref[1, 0:1, :].astype(F32)
    z = g_ref[...] * (conv + skip_ref[...] * zp_ref[...])
    o_ref[...] = z
    if not last:
        rest[0][...] = z.astype(BF16)


def _block_conv(zb, zb_row0, tables, order, mats, u, gate_col, z_prev, z_prev_row0, skip, row0, n_seq, seq_len, last):
    blk = mats.shape[-1]
    n_blk = seq_len // blk

    def rows(base):
        return lambda b, n: (base // blk + b * n_blk + n, 0)

    out_spec = pl.BlockSpec((blk, HY_WIDTH), rows(0))
    out_shape = [jax.ShapeDtypeStruct((n_seq * n_blk * blk, HY_WIDTH), F32)]
    if not last:
        out_shape.append(jax.ShapeDtypeStruct((n_seq * n_blk * blk, HY_WIDTH), BF16))
    return pl.pallas_call(
        functools.partial(_block_conv_kernel, n_blk, last),
        grid=(n_seq, n_blk),
        in_specs=[
            pl.BlockSpec((seq_len, HY_WIDTH), lambda b, n: (zb_row0 // seq_len + b, 0)),
            pl.BlockSpec((None,) + tables.shape[1:], lambda b, n: (order, 0, 0, 0, 0), pipeline_mode=pl.Buffered(1)),
            pl.BlockSpec((2, blk, blk), lambda b, n: (0, 0, 0)),
            pl.BlockSpec((blk, HY_WIDTH), lambda b, n: (row0 // blk + b * n_blk + n, gate_col)),
            pl.BlockSpec((blk, HY_WIDTH), rows(z_prev_row0)),
            pl.BlockSpec((1, HY_WIDTH), lambda b, n: (0, 0)),
        ],
        out_specs=[out_spec] * len(out_shape),
        out_shape=out_shape,
        scratch_shapes=[pltpu.VMEM((n_blk, 2, blk, HY_WIDTH), F32), pltpu.VMEM((2, blk, HY_WIDTH), BF16)],
        compiler_params=_cparams("arbitrary", "arbitrary"),
        name="hyena_block_conv",
    )(zb, tables, mats, u, z_prev, skip)


def _taps_kernel(seq_len, rows, w1_ref, b1_ref, w2_ref, b2_ref, w3_ref, freq_ref, decay_ref, k_ref, norm_ref):
    i = pl.program_id(0)
    lag = i * rows + lax.broadcasted_iota(jnp.int32, (rows, 1), 0) - seq_len
    t = jnp.abs(lag).astype(F32) * (1.0 / (seq_len - 1))
    pos = jnp.abs(i * rows + lax.broadcasted_iota(jnp.int32, (1, rows), 1) - seq_len).astype(F32)
    band = lax.broadcasted_iota(jnp.int32, (HY_POS_BANDS, 1), 0).astype(F32)
    band = 1e-4 + band * ((HY_POS_BANDS - 1 - 1e-4) / (HY_POS_BANDS - 1))
    ang = band * (pos * (2.0 * math.pi / seq_len))
    dot = functools.partial(jnp.dot, precision=lax.Precision.HIGHEST, preferred_element_type=F32)
    pre = w1_ref[:, 0:1] * (pos * (1.0 / (seq_len - 1)))
    pre = pre + dot(w1_ref[:, 1:1 + HY_POS_BANDS], jnp.cos(ang))
    pre = pre - dot(w1_ref[:, 1 + HY_POS_BANDS:], jnp.sin(ang))
    freq = freq_ref[...]
    h = jnp.sin(freq * (pre + b1_ref[...]))
    h = jnp.sin(freq * (dot(w2_ref[...], h) + b2_ref[...]))
    h = lax.dot_general(h.astype(BF16), w3_ref[...].astype(BF16), (((0,), (0,)), ((), ())),
                        preferred_element_type=F32)
    h = h * jnp.exp(-t * jnp.abs(decay_ref[...]))

    @pl.when(i == 0)
    def _():
        norm_ref[...] = jnp.zeros_like(norm_ref)

    for o in range(HY_ORDER):
        c0 = o * 2 * HY_WIDTH
        k = jnp.where(lag >= 0, h[:, c0:c0 + HY_WIDTH], h[:, c0 + HY_WIDTH:c0 + 2 * HY_WIDTH])
        k = jnp.where(lag == -seq_len, 0.0, k)
        k_ref[o] = k.astype(BF16)
        norm_ref[:, o * HY_WIDTH:(o + 1) * HY_WIDTH] += jnp.sum(jnp.abs(k), axis=0, keepdims=True)


def _hyena_taps(seq_len, w1, b1, w2, b2, w3, freq, decay):
    rows = min(2 * seq_len, 512)
    full = lambda a: pl.BlockSpec(a.shape, lambda i: (0,) * a.ndim)
    args = (w1.T, b1.reshape(-1, 1), w2.T, b2.reshape(-1, 1), w3, freq.reshape(-1, 1), decay.reshape(1, -1))
    return pl.pallas_call(
        functools.partial(_taps_kernel, seq_len, rows),
        grid=(2 * seq_len // rows,),
        in_specs=[full(a) for a in args],
        out_specs=[pl.BlockSpec((HY_ORDER, rows, HY_WIDTH), lambda i: (0, i, 0)),
                   pl.BlockSpec((1, HY_ORDER * HY_WIDTH), lambda i: (0, 0))],
        out_shape=[jax.ShapeDtypeStruct((HY_ORDER, 2 * seq_len, HY_WIDTH), BF16),
                   jax.ShapeDtypeStruct((1, HY_ORDER * HY_WIDTH), F32)],
        compiler_params=_cparams("arbitrary"),
        name="hyena_taps",
    )(*args)


def _tables_kernel(group, x_ref, norm_ref, f_ref, o_ref, prev_ref):
    blk = f_ref.shape[-1]

    @pl.when(pl.program_id(1) == 0)
    def _():
        prev_ref[...] = jnp.zeros_like(prev_ref)

    f = lax.broadcasted_iota(jnp.int32, (blk, 1), 0)
    alt = jnp.where(f % 2 == 0, 1.0, -1.0).astype(F32)
    scale = jnp.where(f == 0, 1.0, 2.0).astype(F32) * (1.0 / (2 * blk)) / norm_ref[...]
    x = jnp.concatenate([x_ref[g * blk:(g + 1) * blk, :] for g in range(group)], axis=1)
    for c in range(2):
        xc = jnp.dot(f_ref[c], x, preferred_element_type=F32)
        prev = prev_ref[c]
        for g in range(group):
            cur = xc[:, g * HY_WIDTH:(g + 1) * HY_WIDTH]
            o_ref[g, c] = (cur + alt * prev) * scale
            prev = cur
        prev_ref[c] = prev


def _hyena_tables(seq_len, mats, filt):
    blk = mats.shape[-1]
    n_lag = 2 * seq_len // blk
    group = min(n_lag, 8)
    taps, norm = _hyena_taps(seq_len, *filt)
    return pl.pallas_call(
        functools.partial(_tables_kernel, group),
        grid=(HY_ORDER, n_lag // group),
        in_specs=[
            pl.BlockSpec((None, group * blk, HY_WIDTH), lambda o, q: (o, q, 0)),
            pl.BlockSpec((1, HY_WIDTH), lambda o, q: (0, o)),
            pl.BlockSpec((2, blk, blk), lambda o, q: (0, 0, 0)),
        ],
        out_specs=pl.BlockSpec((None, group, 2, blk, HY_WIDTH), lambda o, q: (o, q, 0, 0, 0)),
        out_shape=jax.ShapeDtypeStruct((HY_ORDER, n_lag, 2, blk, HY_WIDTH), F32),
        scratch_shapes=[pltpu.VMEM((2, blk, HY_WIDTH), F32)],
        compiler_params=_cparams("arbitrary", "arbitrary"),
        name="hyena_tables",
    )(taps, norm, mats)


def _hyena_path(u, vb, row0, n_seq, seq_len, mats, tables, skip):
    z1, z1b = _block_conv(vb, row0, tables, 0, mats, u, 1, u, row0, skip[0:1], row0, n_seq, seq_len, False)
    (ya,) = _block_conv(z1b, 0, tables, 1, mats, u, 2, z1, 0, skip[1:2], row0, n_seq, seq_len, True)
    return ya


S5_LANES = S5_GROUPS * S5_STATE
S5_SEGS = 8
S5_CHUNKS = 4


def _s5_kernel(n_tiles, tile, u0_ref, u1_ref, h0_ref, lam_ref, bd_ref, cd_ref, skip_ref, gw_ref, gb_ref,
               y_ref, fin_ref, bu_ref, ybwd_ref, p_ref, p8_ref, cm_ref, fl_ref, carry_ref, up_ref, st0_ref, st1_ref,
               ub_ref, yacc_ref):
    seg = tile // S5_SEGS
    n = S5_LANES
    b = pl.program_id(0)
    j = pl.program_id(1)

    @pl.when((b == 0) & (j == 0))
    def _powers():
        for d in range(2):
            lr = lam_ref[2 * d:2 * d + 1, :]
            li = lam_ref[2 * d + 1:2 * d + 2, :]

            def put(k, pr, pi, d=d):
                p_ref[d, pl.ds(k, 1), 0:n] = pr
                p_ref[d, pl.ds(k, 1), n:2 * n] = pi
                r = pl.ds(pl.multiple_of(k * S5_SEGS, S5_SEGS), S5_SEGS)
                p8_ref[d, r, 0:n] = jnp.broadcast_to(pr, (S5_SEGS, n))
                p8_ref[d, r, n:2 * n] = jnp.broadcast_to(pi, (S5_SEGS, n))

            put(0, lr, li)

            def body(k, c, lr=lr, li=li, put=put):
                pr, pi = c
                nr = pr * lr - pi * li
                ni = pr * li + pi * lr
                put(k, nr, ni)
                return nr, ni

            lax.fori_loop(1, seg, body, (lr, li))

    def sweep(d, t):
        first, last = (0, n_tiles - 1) if d == 0 else (n_tiles - 1, 0)

        @pl.when(t == first)
        def _init():
            carry_ref[0:1, 0:n] = h0_ref[2 * d:2 * d + 1, :]
            carry_ref[0:1, n:2 * n] = h0_ref[2 * d + 1:2 * d + 2, :]

        def gather(i, c):
            r = pl.ds(pl.multiple_of(i * S5_SEGS, S5_SEGS), S5_SEGS)
            up_ref[r, 0:LANE] = u0_ref[pl.ds(i, S5_SEGS, stride=seg), :]
            up_ref[r, LANE:2 * LANE] = u1_ref[pl.ds(i, S5_SEGS, stride=seg), :]
            return c

        lax.fori_loop(0, seg, gather, 0, unroll=2)
        ub_ref[...] = up_ref[...].astype(BF16)
        nq = n // S5_CHUNKS

        def lanes(q):
            return slice(q * nq, (q + 1) * nq), slice(n + q * nq, n + (q + 1) * nq)

        def rows(i):
            k = i if d == 0 else seg - 1 - i
            return slice(k * S5_SEGS, (k + 1) * S5_SEGS)

        def project_in(q):
            re, im = lanes(q)
            bu_ref[:, re] = jnp.dot(ub_ref[...], bd_ref[d, :, re], preferred_element_type=F32)
            bu_ref[:, im] = jnp.dot(ub_ref[...], bd_ref[d, :, im], preferred_element_type=F32)

        def scan(q):
            re, im = lanes(q)
            lr8 = jnp.broadcast_to(lam_ref[2 * d:2 * d + 1, re], (S5_SEGS, nq))
            li8 = jnp.broadcast_to(lam_ref[2 * d + 1:2 * d + 2, re], (S5_SEGS, nq))
            hr = hi = jnp.zeros((S5_SEGS, nq), F32)
            for i in range(seg):
                r = rows(i)
                hr, hi = lr8 * hr - li8 * hi + bu_ref[r, re], lr8 * hi + li8 * hr + bu_ref[r, im]
                bu_ref[r, re] = hr
                bu_ref[r, im] = hi
            fl_ref[:, re] = hr
            fl_ref[:, im] = hi

            plr = p_ref[d, seg - 1:seg, re]
            pli = p_ref[d, seg - 1:seg, im]
            cr = carry_ref[0:1, re]
            ci = carry_ref[0:1, im]
            for s in (range(S5_SEGS) if d == 0 else reversed(range(S5_SEGS))):
                cm_ref[s:s + 1, re] = cr
                cm_ref[s:s + 1, im] = ci
                cr, ci = (plr * cr - pli * ci + fl_ref[s:s + 1, re], plr * ci + pli * cr + fl_ref[s:s + 1, im])
            carry_ref[0:1, re] = cr
            carry_ref[0:1, im] = ci

            cmr = cm_ref[:, re]
            cmi = cm_ref[:, im]
            for i in range(seg):
                r = rows(i)
                pw = slice(i * S5_SEGS, (i + 1) * S5_SEGS)
                pr = p8_ref[d, pw, re]
                pi = p8_ref[d, pw, im]
                bu_ref[r, re] = bu_ref[r, re] + (pr * cmr - pi * cmi)
                bu_ref[r, im] = bu_ref[r, im] + (pr * cmi + pi * cmr)

        def project_out(q):
            re, im = lanes(q)
            part = (jnp.dot(bu_ref[:, re].astype(BF16), cd_ref[d, re, :], preferred_element_type=F32)
                    + jnp.dot(bu_ref[:, im].astype(BF16), cd_ref[d, im, :], preferred_element_type=F32))
            yacc_ref[...] = part if q == 0 else yacc_ref[...] + part

        project_in(0)
        for q in range(S5_CHUNKS):
            if q + 1 < S5_CHUNKS:
                project_in(q + 1)
            if q > 0:
                project_out(q - 1)
            scan(q)
        project_out(S5_CHUNKS - 1)
        y = yacc_ref[...]

        @pl.when(t == last)
        def _final():
            fin_ref[2 * d:2 * d + 1, :] = carry_ref[0:1, 0:n]
            fin_ref[2 * d + 1:2 * d + 2, :] = carry_ref[0:1, n:2 * n]

        trow = pl.ds(pl.multiple_of(t * tile, tile), tile)
        if d == 1:
            ybwd_ref[trow, :] = y
        else:
            tot = up_ref[...] * skip_ref[...] + y + ybwd_ref[trow, :]
            g = jnp.dot(tot.astype(BF16), gw_ref[...], preferred_element_type=F32) + gb_ref[...]
            res = g[:, :S5_WIDTH] * jax.nn.sigmoid(g[:, S5_WIDTH:])
            st0_ref[...] = res[:, 0:LANE]
            st1_ref[...] = res[:, LANE:2 * LANE]
            per_seg = seg // S5_SEGS

            def unpermute(m, c):
                s = m // per_seg
                i0 = (m % per_seg) * S5_SEGS
                src = pl.ds(i0 * S5_SEGS + s, S5_SEGS, stride=S5_SEGS)
                dst = pl.ds(pl.multiple_of(m * S5_SEGS, S5_SEGS), S5_SEGS)
                y_ref[dst, 0:LANE] = st0_ref[src, :]
                y_ref[dst, LANE:2 * LANE] = st1_ref[src, :]
                return c

            lax.fori_loop(0, tile // S5_SEGS, unpermute, 0, unroll=2)

    @pl.when(j < n_tiles)
    def _bwd():
        sweep(1, n_tiles - 1 - j)

    @pl.when(j >= n_tiles)
    def _fwd():
        sweep(0, j - n_tiles)


def _s5_branch(zm, row0, n_seq, seq_len, tile, h0, lam, bd, cd, skip, gw, gb):
    n_tiles = seq_len // tile
    blk0 = row0 // tile
    col = SPLIT_S5 // LANE
    seg = tile // S5_SEGS

    def tile_of(j):
        return jnp.where(j < n_tiles, n_tiles - 1 - j, j - n_tiles)

    return pl.pallas_call(
        functools.partial(_s5_kernel, n_tiles, tile),
        grid=(n_seq, 2 * n_tiles),
        in_specs=[
            pl.BlockSpec((tile, LANE), lambda b, j: (blk0 + b * n_tiles + tile_of(j), col)),
            pl.BlockSpec((tile, LANE), lambda b, j: (blk0 + b * n_tiles + tile_of(j), col + 1)),
            pl.BlockSpec((None, 4, S5_LANES), lambda b, j: (b, 0, 0)),
            pl.BlockSpec((4, S5_LANES), lambda b, j: (0, 0)),
            pl.BlockSpec((2, S5_WIDTH, 2 * S5_LANES), lambda b, j: (0, 0, 0)),
            pl.BlockSpec((2, 2 * S5_LANES, S5_WIDTH), lambda b, j: (0, 0, 0)),
            pl.BlockSpec((1, S5_WIDTH), lambda b, j: (0, 0)),
            pl.BlockSpec((S5_WIDTH, 2 * S5_WIDTH), lambda b, j: (0, 0)),
            pl.BlockSpec((1, 2 * S5_WIDTH), lambda b, j: (0, 0)),
        ],
        out_specs=[
            pl.BlockSpec((tile, S5_WIDTH), lambda b, j: (b * n_tiles + jnp.maximum(j - n_tiles, 0), 0)),
            pl.BlockSpec((None, 4, S5_LANES), lambda b, j: (b, 0, 0)),
        ],
        out_shape=[
            jax.ShapeDtypeStruct((n_seq * seq_len, S5_WIDTH), F32),
            jax.ShapeDtypeStruct((n_seq, 4, S5_LANES), F32),
        ],
        scratch_shapes=[
            pltpu.VMEM((tile, 2 * S5_LANES), F32),
            pltpu.VMEM((seq_len, S5_WIDTH), F32),
            pltpu.VMEM((2, seg, 2 * S5_LANES), F32),
            pltpu.VMEM((2, tile, 2 * S5_LANES), F32),
            pltpu.VMEM((S5_SEGS, 2 * S5_LANES), F32),
            pltpu.VMEM((S5_SEGS, 2 * S5_LANES), F32),
            pltpu.VMEM((1, 2 * S5_LANES), F32),
            pltpu.VMEM((tile, S5_WIDTH), F32),
            pltpu.VMEM((tile, LANE), F32),
            pltpu.VMEM((tile, LANE), F32),
            pltpu.VMEM((tile, S5_WIDTH), BF16),
            pltpu.VMEM((tile, S5_WIDTH), F32),
        ],
        compiler_params=_cparams("arbitrary", "arbitrary"),
        name="s5_scan",
    )(zm, zm, h0, lam, bd, cd, skip, gw, gb)


def _s5_params(lam_re, lam_im, log_step, b_re, b_im, c_re, c_im):
    step = jnp.exp(log_step)[..., None]
    mag = jnp.exp(lam_re * step)
    bar_re, bar_im = mag * jnp.cos(lam_im * step), mag * jnp.sin(lam_im * step)
    den = lam_re * lam_re + lam_im * lam_im
    q_re = ((bar_re - 1.0) * lam_re + bar_im * lam_im) / den
    q_im = (bar_im * lam_re - (bar_re - 1.0) * lam_im) / den
    bb_re = q_re[..., None] * b_re - q_im[..., None] * b_im
    bb_im = q_re[..., None] * b_im + q_im[..., None] * b_re
    eye = jnp.eye(S5_GROUPS, dtype=F32)

    def in_mat(x):
        xt = jnp.swapaxes(x, 2, 3)
        return (xt[:, :, :, None, :] * eye[None, :, None, :, None]).reshape(2, S5_WIDTH, S5_LANES)

    def out_mat(x):
        xt = jnp.swapaxes(x, 2, 3)
        return (xt[:, :, :, None, :] * eye[None, :, None, :, None]).reshape(2, S5_LANES, S5_WIDTH)

    bd = jnp.concatenate([in_mat(bb_re), in_mat(bb_im)], axis=2).astype(BF16)
    cd = jnp.concatenate([out_mat(c_re), out_mat(-c_im)], axis=1).astype(BF16)
    lam_rows = jnp.stack([bar_re[0], bar_im[0], bar_re[1], bar_im[1]])
    return lam_rows.reshape(4, S5_LANES), bd, cd


ROPE_HALF = HEAD_DIM // 4
CTX_SEQS = 2


def _rope_tables(seq_len):
    t = jnp.arange(seq_len)
    row = (t // GRID_W).astype(F32)
    col = (t % GRID_W).astype(F32)
    inv = ROPE_BASE ** (-jnp.arange(ROPE_HALF, dtype=F32) / ROPE_HALF)
    ang_r = row[:, None] * inv
    ang_c = col[:, None] * inv
    cos = jnp.concatenate([jnp.cos(ang_r), jnp.cos(ang_r), jnp.cos(ang_c), jnp.cos(ang_c)], axis=-1)
    sin = jnp.concatenate([-jnp.sin(ang_r), jnp.sin(ang_r), -jnp.sin(ang_c), jnp.sin(ang_c)], axis=-1)
    return jnp.tile(cos, (1, LANE // HEAD_DIM)), jnp.tile(sin, (1, LANE // HEAD_DIM))


def _rope(x, cos, sin):
    lane = lax.broadcasted_iota(jnp.int32, x.shape, 1)
    first = (lane % (2 * ROPE_HALF)) < ROPE_HALF
    partner = jnp.where(first, pltpu.roll(x, LANE - ROPE_HALF, axis=1), pltpu.roll(x, ROPE_HALF, axis=1))
    return x * cos + partner * sin


def _gqa(q, ks, vs, biases, sink_ref):
    t = q.shape[0]
    low = lax.broadcasted_iota(jnp.int32, (1, LANE), 1) < HEAD_DIM
    assert N_KV_HEADS * HEAD_DIM == LANE
    k_nat = [k.astype(BF16) for k in ks]
    k_swp = [pltpu.roll(k, HEAD_DIM, axis=1).astype(BF16) for k in ks]
    v_aug = [jnp.concatenate([v, jnp.ones_like(v)], axis=1).astype(BF16) for v in vs]
    outs = {}
    for keys, heads in ((k_nat, [h for h in range(N_HEADS) if h % 2 == h // Q_PER_KV]),
                        (k_swp, [h for h in range(N_HEADS) if h % 2 != h // Q_PER_KV])):
        scale = HEAD_DIM ** -0.5
        assert math.frexp(scale)[0] == 0.5
        rows = [jnp.where(low if h % 2 == 0 else jnp.logical_not(low),
                          q[:, (h // 2) * LANE:(h // 2 + 1) * LANE] * scale, 0.0) for h in heads]
        qs = jnp.concatenate(rows, axis=0).astype(BF16)
        sink_col = jnp.concatenate([jnp.full((t, 1), sink_ref[h], F32) for h in heads], axis=0)
        ss = []
        m = sink_col
        for k, bias in zip(keys, biases):
            s = lax.dot_general(qs, k, (((1,), (1,)), ((), ())), preferred_element_type=F32)
            if bias is not None:
                s = s + bias
            ss.append(s)
            m = jnp.maximum(m, jnp.max(s, axis=-1, keepdims=True))
        acc = jnp.zeros((len(heads) * t, 2 * LANE), F32)
        for s, v in zip(ss, v_aug):
            acc = acc + jnp.dot(jnp.exp(s - m).astype(BF16), v, preferred_element_type=F32)
        o = acc[:, :LANE] / (acc[:, LANE:LANE + 1] + jnp.exp(sink_col - m))
        for j, h in enumerate(heads):
            outs[h] = o[j * t:(j + 1) * t]
    chunks = []
    for c in range(N_HEADS // 2):
        halves = []
        for h in (2 * c, 2 * c + 1):
            halves.append(outs[h] if h // Q_PER_KV == h % 2 else pltpu.roll(outs[h], HEAD_DIM, axis=1))
        chunks.append(jnp.where(low, halves[0], halves[1]))
    return jnp.concatenate(chunks, axis=1)


def _ctx_attn_kernel(q_ref, k_ref, v_ref, sink_ref, kin_ref, vin_ref, o_ref, kout_ref, vout_ref):
    del kin_ref, vin_ref
    for g in range(CTX_SEQS):
        rows = slice(g * SEQ, (g + 1) * SEQ)
        k, v = k_ref[rows, :], v_ref[rows, :]
        kout_ref[g] = k
        vout_ref[g] = v
        o_ref[rows, :] = _gqa(q_ref[rows, :], [k], [v], [None], sink_ref)


def _context_attention(zm, sink, layer, new_k, new_v):
    slab = pl.BlockSpec((CTX_SEQS, None, SEQ, KV_WIDTH), lambda b: (b, layer, 0, 0))
    rows = CTX_SEQS * SEQ
    return pl.pallas_call(
        _ctx_attn_kernel,
        grid=(BATCH // CTX_SEQS,),
        in_specs=[
            pl.BlockSpec((rows, ATTN_WIDTH), lambda b: (b, SPLIT_Q // ATTN_WIDTH)),
            pl.BlockSpec((rows, KV_WIDTH), lambda b: (b, SPLIT_K // KV_WIDTH)),
            pl.BlockSpec((rows, KV_WIDTH), lambda b: (b, SPLIT_V // KV_WIDTH)),
            pl.BlockSpec(memory_space=pltpu.SMEM),
            pl.BlockSpec(memory_space=pl.ANY),
            pl.BlockSpec(memory_space=pl.ANY),
        ],
        out_specs=[pl.BlockSpec((rows, ATTN_WIDTH), lambda b: (b, 0)), slab, slab],
        out_shape=[jax.ShapeDtypeStruct((N_PROMPT_TOK, ATTN_WIDTH), F32),
                   jax.ShapeDtypeStruct(new_k.shape, F32), jax.ShapeDtypeStruct(new_v.shape, F32)],
        input_output_aliases={4: 1, 5: 2},
        compiler_params=_cparams("arbitrary"),
        name="ctx_attention",
    )(zm, zm, zm, sink, new_k, new_v)


def _latent_attn_kernel(q_ref, kl_ref, kc_ref, kr_ref, vl_ref, vc_ref, vr_ref, ck_ref, cv_ref, cos_ref, sin_ref,
                        band_ref, sink_ref, o_ref):
    i = pl.program_id(1)
    nb = DEC_SEQ // BLOCK

    def table(ref, blk):
        return ref[pl.ds(pl.multiple_of(blk * BLOCK, BLOCK), BLOCK), :]

    left, right = jnp.maximum(i - 1, 0), jnp.minimum(i + 1, nb - 1)
    cos_q, sin_q = table(cos_ref, i), table(sin_ref, i)
    q = jnp.concatenate([_rope(q_ref[:, c:c + LANE], cos_q, sin_q) for c in range(0, ATTN_WIDTH, LANE)], axis=1)
    kw = jnp.concatenate([
        _rope(kl_ref[...], table(cos_ref, left), table(sin_ref, left)),
        _rope(kc_ref[...], cos_q, sin_q),
        _rope(kr_ref[...], table(cos_ref, right), table(sin_ref, right)),
    ], axis=0)
    vw = jnp.concatenate([vl_ref[...], vc_ref[...], vr_ref[...]], axis=0)

    c = lax.broadcasted_iota(jnp.int32, (1, 3 * BLOCK), 1)
    outside = ((c < BLOCK) & (i == 0)) | ((c >= 2 * BLOCK) & (i == nb - 1))
    bias = band_ref[...] + jnp.where(outside, NEG_INF, 0.0)

    o_ref[...] = _gqa(q, [ck_ref[...], kw], [cv_ref[...], vw], [None, bias], sink_ref)


def _latent_attention(zm, cache_k, cache_v, layer, cos, sin, sink):
    nb = DEC_SEQ // BLOCK
    blk0 = N_PROMPT_TOK // BLOCK
    kcol, vcol = SPLIT_K // KV_WIDTH, SPLIT_V // KV_WIDTH

    def row(b, i):
        return blk0 + b * nb + i

    def win_specs(col):
        return [
            pl.BlockSpec((BLOCK, KV_WIDTH), lambda b, i: (row(b, jnp.maximum(i - 1, 0)), col)),
            pl.BlockSpec((BLOCK, KV_WIDTH), lambda b, i: (row(b, i), col)),
            pl.BlockSpec((BLOCK, KV_WIDTH), lambda b, i: (row(b, jnp.minimum(i + 1, nb - 1)), col)),
        ]

    ctx_spec = pl.BlockSpec((None, None, PAST_LEN, KV_WIDTH), lambda b, i: (b, layer, 0, 0))
    tab_spec = pl.BlockSpec((DEC_SEQ, LANE), lambda b, i: (0, 0))
    r = jnp.arange(Q_PER_KV * BLOCK)[:, None] % BLOCK
    c = jnp.arange(3 * BLOCK)[None, :]
    band = jnp.where(jnp.abs(r + BLOCK - c) <= WINDOW, 0.0, NEG_INF).astype(F32)
    return pl.pallas_call(
        _latent_attn_kernel,
        grid=(DEC_BATCH, nb),
        in_specs=[pl.BlockSpec((BLOCK, ATTN_WIDTH), lambda b, i: (row(b, i), SPLIT_Q // ATTN_WIDTH))]
        + win_specs(kcol) + win_specs(vcol)
        + [ctx_spec, ctx_spec, tab_spec, tab_spec, pl.BlockSpec(band.shape, lambda b, i: (0, 0)),
           pl.BlockSpec(memory_space=pltpu.SMEM)],
        out_specs=pl.BlockSpec((BLOCK, ATTN_WIDTH), lambda b, i: (b * nb + i, 0)),
        out_shape=jax.ShapeDtypeStruct((N_SAMPLE_TOK, ATTN_WIDTH), F32),
        compiler_params=_cparams("arbitrary", "arbitrary"),
        name="latent_attention",
    )(zm, zm, zm, zm, zm, zm, zm, cache_k, cache_v, cos, sin, band, sink)


def kernel(x_prompt, x_sample, cache_k, cache_v, state_s5_re, state_s5_im, c, c_ctx, ada_w, ada_b, norm1_g, w_in, hy_conv_w, hy_conv_b, hy_pos_w1, hy_pos_b1, hy_pos_w2, hy_pos_b2, hy_pos_w3, hy_sin_freq, hy_decay, hy_skip, s5_lam_re, s5_lam_im, s5_log_step, s5_b_re, s5_b_im, s5_c_re, s5_c_im, s5_skip, s5_glu_w, s5_glu_b, attn_sink, proj_a, proj_b, proj_c, w_out, norm2_g, ffn_w_gate, ffn_w_up, ffn_w_down, final_norm_g):
    cvec = jnp.concatenate([c_ctx[None], c, jnp.zeros((MOD_ROWS - 1 - DEC_BATCH, D_MODEL), F32)], axis=0)
    mod = _modulation(cvec, ada_w, ada_b).reshape(DEPTH * MOD_ROWS * 6, 1, D_MODEL)

    x = jnp.concatenate([x_prompt.reshape(N_PROMPT_TOK, D_MODEL), x_sample.reshape(N_SAMPLE_TOK, D_MODEL)], axis=0)
    w_main = w_in[:, :, :MAIN_IN].astype(BF16)
    w_gate = w_in[:, :, MAIN_IN:].astype(BF16)
    pa, pb, pc, wo = proj_a.astype(BF16), proj_b.astype(BF16), proj_c.astype(BF16), w_out.astype(BF16)
    wg, wu, wd = ffn_w_gate.astype(BF16), ffn_w_up.astype(BF16), ffn_w_down.astype(BF16)
    g1, g2 = norm1_g.reshape(DEPTH, 1, D_MODEL), norm2_g.reshape(DEPTH, 1, D_MODEL)
    glu_w = s5_glu_w.astype(BF16)

    ctx_k = cache_k.reshape(DEC_BATCH, DEPTH, PAST_LEN, KV_WIDTH)
    ctx_v = cache_v.reshape(DEC_BATCH, DEPTH, PAST_LEN, KV_WIDTH)
    rope_cos, rope_sin = _rope_tables(DEC_SEQ)
    mats_p, mats_s = _dft_mats(min(SEQ, HY_BLOCK)), _dft_mats(min(DEC_SEQ, HY_BLOCK))

    new_k = jnp.zeros((BATCH, DEPTH, SEQ, KV_WIDTH), F32)
    new_v = jnp.zeros((BATCH, DEPTH, SEQ, KV_WIDTH), F32)
    sre_out, sim_out = [], []
    for l in range(DEPTH):
        filt = (hy_pos_w1[l], hy_pos_b1[l], hy_pos_w2[l], hy_pos_b2[l], hy_pos_w3[l], hy_sin_freq[l], hy_decay[l])
        zm = _in_proj(x, mod, l, g1, w_main)
        u, vb = _short_conv(zm, hy_conv_w[l], hy_conv_b[l].reshape(1, HY_IN))
        ya_p = _hyena_path(u, vb, 0, BATCH, SEQ, mats_p, _hyena_tables(SEQ, mats_p, filt), hy_skip[l])
        ya_s = _hyena_path(u, vb, N_PROMPT_TOK, DEC_BATCH, DEC_SEQ, mats_s, _hyena_tables(DEC_SEQ, mats_s, filt),
                           hy_skip[l])
        yc_p, new_k, new_v = _context_attention(zm, attn_sink[l], l, new_k, new_v)
        yc_s = _latent_attention(zm, ctx_k, ctx_v, l, rope_cos, rope_sin, attn_sink[l])

        lam, bd, cd = _s5_params(s5_lam_re[l], s5_lam_im[l], s5_log_step[l], s5_b_re[l], s5_b_im[l],
                                 s5_c_re[l], s5_c_im[l])
        s5_w = (lam, bd, cd, s5_skip[l].reshape(1, S5_WIDTH), glu_w[l], s5_glu_b[l].reshape(1, 2 * S5_WIDTH))
        h0_s = jnp.stack([state_s5_re[:, l, 0], state_s5_im[:, l, 0], state_s5_re[:, l, 1], state_s5_im[:, l, 1]],
                         axis=1).reshape(DEC_BATCH, 4, S5_LANES)
        yb_p, fin_p = _s5_branch(zm, 0, BATCH, SEQ, SEQ, jnp.zeros((BATCH, 4, S5_LANES), F32), *s5_w)
        yb_s, _ = _s5_branch(zm, N_PROMPT_TOK, DEC_BATCH, DEC_SEQ, 512, h0_s, *s5_w)
        fin_p = fin_p.reshape(BATCH, 2, 2, S5_GROUPS, S5_STATE)

        sre_out.append(fin_p[:, :, 0])
        sim_out.append(fin_p[:, :, 1])
        x = _merge((ya_p, ya_s), (yb_p, yb_s), (yc_p, yc_s), x, mod, l, g1, w_gate, pa, pb, pc, wo)
        x = _ffn(x, mod, l, g2, wg, wu, wd)

    y_prompt, y_sample = _final_norm(x, final_norm_g.reshape(1, D_MODEL))
    return (y_prompt.reshape(BATCH, SEQ, D_MODEL), y_sample.reshape(DEC_BATCH, DEC_SEQ, D_MODEL),
            new_k.reshape(BATCH, DEPTH, SEQ, N_KV_HEADS, HEAD_DIM), new_v.reshape(BATCH, DEPTH, SEQ, N_KV_HEADS, HEAD_DIM),
            jnp.stack(sre_out, axis=1), jnp.stack(sim_out, axis=1))
```

```python
import functools
import math

import jax
import jax.numpy as jnp
from jax import lax
from jax.experimental import pallas as pl
from jax.experimental.pallas import tpu as pltpu

D_MODEL = 1024
BATCH = 16
SEQ = 256
DEPTH = 4
DEC_BATCH = 2
DEC_SEQ = 4096
PAST_LEN = 512
GRID_W = 64
N_BRANCH = 3
HY_WIDTH = 256
HY_ORDER = 2
HY_SHORT_K = 3
HY_POS_EMB = 33
HY_POS_BANDS = (HY_POS_EMB - 1) // 2
S5_WIDTH = 256
S5_GROUP = 16
S5_GROUPS = S5_WIDTH // S5_GROUP
S5_STATE = 64
N_HEADS = 8
N_KV_HEADS = 2
Q_PER_KV = N_HEADS // N_KV_HEADS
HEAD_DIM = 64
ATTN_WIDTH = N_HEADS * HEAD_DIM
KV_WIDTH = N_KV_HEADS * HEAD_DIM
WINDOW = 128
BLOCK = 128
ROPE_BASE = 10000.0
FFN_HIDDEN = ((8 * D_MODEL // 3 + 255) // 256) * 256
HY_IN = (HY_ORDER + 1) * HY_WIDTH
GATE_IN = N_BRANCH * D_MODEL
MAIN_IN = HY_IN + S5_WIDTH + ATTN_WIDTH + 2 * KV_WIDTH
SPLIT_S5 = HY_IN
SPLIT_Q = SPLIT_S5 + S5_WIDTH
SPLIT_K = SPLIT_Q + ATTN_WIDTH
SPLIT_V = SPLIT_K + KV_WIDTH

F32 = jnp.float32
BF16 = jnp.bfloat16
EPS = 1e-6
NEG_INF = -1e30

N_PROMPT_TOK = BATCH * SEQ
N_SAMPLE_TOK = DEC_BATCH * DEC_SEQ
N_TOK = N_PROMPT_TOK + N_SAMPLE_TOK
LANE = 128
MOD_ROWS = 8
TM = 512
V7X_VMEM_BYTES = 64 * 1024 * 1024
VMEM_LIMIT = V7X_VMEM_BYTES * 7 // 8


def _cparams(*sem):
    return pltpu.CompilerParams(dimension_semantics=sem, vmem_limit_bytes=VMEM_LIMIT)


def _mod_row(i):
    n_p = N_PROMPT_TOK // TM
    per_b = DEC_SEQ // TM
    return jnp.where(i < n_p, 0, 1 + (i - n_p) // per_b)


def _mod_spec(layer, k):
    return pl.BlockSpec((None, 1, D_MODEL), lambda i: ((layer * MOD_ROWS + _mod_row(i)) * 6 + k, 0, 0))


def _resident(shape, layer=None):
    if layer is None:
        return pl.BlockSpec(shape, lambda i: (0,) * len(shape), pipeline_mode=pl.Buffered(1))
    return pl.BlockSpec((None,) + shape, lambda i: (layer,) + (0,) * len(shape), pipeline_mode=pl.Buffered(1))


def _rms(x, g):
    return x * lax.rsqrt(jnp.mean(x * x, axis=-1, keepdims=True) + EPS) * g


def _mod_kernel(c_ref, w_ref, b_ref, o_ref):
    c = c_ref[...]
    s = (c * jax.nn.sigmoid(c)).astype(BF16)
    o_ref[...] = jnp.dot(s, w_ref[...].astype(BF16), preferred_element_type=F32) + b_ref[...]


def _modulation(cvec, ada_w, ada_b):
    tn = 1536
    return pl.pallas_call(
        _mod_kernel,
        grid=(DEPTH, 6 * D_MODEL // tn),
        in_specs=[
            pl.BlockSpec((MOD_ROWS, D_MODEL), lambda l, j: (0, 0)),
            pl.BlockSpec((None, D_MODEL, tn), lambda l, j: (l, 0, j)),
            pl.BlockSpec((None, 1, tn), lambda l, j: (l, 0, j)),
        ],
        out_specs=pl.BlockSpec((None, MOD_ROWS, tn), lambda l, j: (l, 0, j)),
        out_shape=jax.ShapeDtypeStruct((DEPTH, MOD_ROWS, 6 * D_MODEL), F32),
        compiler_params=_cparams("arbitrary", "arbitrary"),
        name="adaln_mod",
    )(cvec, ada_w, ada_b.reshape(DEPTH, 1, 6 * D_MODEL))


def _inproj_kernel(x_ref, sh_ref, sc_ref, g_ref, wm_ref, zm_ref):
    h = (_rms(x_ref[...], g_ref[...]) * (1.0 + sc_ref[...]) + sh_ref[...]).astype(BF16)
    zm_ref[...] = jnp.dot(h, wm_ref[...], preferred_element_type=F32)


def _in_proj(x, mod, layer, g, w_main):
    return pl.pallas_call(
        _inproj_kernel,
        grid=(N_TOK // TM,),
        in_specs=[
            pl.BlockSpec((TM, D_MODEL), lambda i: (i, 0)),
            _mod_spec(layer, 0),
            _mod_spec(layer, 1),
            _resident((1, D_MODEL), layer),
            _resident((D_MODEL, MAIN_IN), layer),
        ],
        out_specs=pl.BlockSpec((TM, MAIN_IN), lambda i: (i, 0)),
        out_shape=jax.ShapeDtypeStruct((N_TOK, MAIN_IN), F32),
        compiler_params=_cparams("arbitrary"),
        name="in_proj",
    )(x, mod, mod, g, w_main)


N_PROMPT_TILES = N_PROMPT_TOK // TM


def _path_specs(width):
    return [pl.BlockSpec((TM, width), lambda i: (jnp.minimum(i, N_PROMPT_TILES - 1), 0)),
            pl.BlockSpec((TM, width), lambda i: (jnp.maximum(i - N_PROMPT_TILES, 0), 0))]


def _merge_kernel(yap_ref, yas_ref, ybp_ref, ybs_ref, ycp_ref, ycs_ref, x_ref, sh_ref, sc_ref, g1_ref, g_ref,
                  wg_ref, pa_ref, pb_ref, pc_ref, wo_ref, o_ref):
    is_prompt = pl.program_id(0) < N_PROMPT_TILES
    x = x_ref[...]
    h = (_rms(x, g_ref[...]) * (1.0 + sc_ref[...]) + sh_ref[...]).astype(BF16)
    branches = ((yap_ref, yas_ref, pa_ref), (ybp_ref, ybs_ref, pb_ref), (ycp_ref, ycs_ref, pc_ref))
    m = jnp.zeros((TM, D_MODEL), F32)
    for k, (p_ref, s_ref, w_ref) in enumerate(branches):
        y = jnp.where(is_prompt, p_ref[...], s_ref[...]).astype(BF16)
        gate = jax.nn.sigmoid(jnp.dot(h, wg_ref[:, k * D_MODEL:(k + 1) * D_MODEL], preferred_element_type=F32))
        m = m + gate * jnp.dot(y, w_ref[...], preferred_element_type=F32)
    o_ref[...] = x + g1_ref[...] * jnp.dot(m.astype(BF16), wo_ref[...], preferred_element_type=F32)


def _merge(ya, yb, yc, x, mod, layer, g, w_gate, pa, pb, pc, wo):
    return pl.pallas_call(
        _merge_kernel,
        grid=(N_TOK // TM,),
        in_specs=_path_specs(HY_WIDTH) + _path_specs(S5_WIDTH) + _path_specs(ATTN_WIDTH) + [
            pl.BlockSpec((TM, D_MODEL), lambda i: (i, 0)),
            _mod_spec(layer, 0),
            _mod_spec(layer, 1),
            _mod_spec(layer, 2),
            _resident((1, D_MODEL), layer),
            _resident((D_MODEL, GATE_IN), layer),
            _resident((HY_WIDTH, D_MODEL), layer),
            _resident((S5_WIDTH, D_MODEL), layer),
            _resident((ATTN_WIDTH, D_MODEL), layer),
            _resident((D_MODEL, D_MODEL), layer),
        ],
        out_specs=pl.BlockSpec((TM, D_MODEL), lambda i: (i, 0)),
        out_shape=jax.ShapeDtypeStruct((N_TOK, D_MODEL), F32),
        compiler_params=_cparams("arbitrary"),
        name="merge_out",
    )(*ya, *yb, *yc, x, mod, mod, mod, g, w_gate, pa, pb, pc, wo)


def _ffn_kernel(x_ref, sh_ref, sc_ref, gt_ref, g_ref, wg_ref, wu_ref, wd_ref, o_ref):
    x = x_ref[...]
    h = (_rms(x, g_ref[...]) * (1.0 + sc_ref[...]) + sh_ref[...]).astype(BF16)
    a = jnp.dot(h, wg_ref[...], preferred_element_type=F32)
    b = jnp.dot(h, wu_ref[...], preferred_element_type=F32)
    f = (a * jax.nn.sigmoid(a) * b).astype(BF16)
    o_ref[...] = x + gt_ref[...] * jnp.dot(f, wd_ref[...], preferred_element_type=F32)


def _ffn(x, mod, layer, g, wg, wu, wd):
    return pl.pallas_call(
        _ffn_kernel,
        grid=(N_TOK // TM,),
        in_specs=[
            pl.BlockSpec((TM, D_MODEL), lambda i: (i, 0)),
            _mod_spec(layer, 3),
            _mod_spec(layer, 4),
            _mod_spec(layer, 5),
            _resident((1, D_MODEL), layer),
            _resident((D_MODEL, FFN_HIDDEN), layer),
            _resident((D_MODEL, FFN_HIDDEN), layer),
            _resident((FFN_HIDDEN, D_MODEL), layer),
        ],
        out_specs=pl.BlockSpec((TM, D_MODEL), lambda i: (i, 0)),
        out_shape=jax.ShapeDtypeStruct((N_TOK, D_MODEL), F32),
        compiler_params=_cparams("arbitrary"),
        name="ffn",
    )(x, mod, mod, mod, g, wg, wu, wd)


def _final_norm_kernel(x_ref, g_ref, op_ref, os_ref):
    y = _rms(x_ref[...], g_ref[...])
    is_prompt = pl.program_id(0) < N_PROMPT_TILES

    @pl.when(is_prompt)
    def _():
        op_ref[...] = y

    @pl.when(jnp.logical_not(is_prompt))
    def _():
        os_ref[...] = y


def _final_norm(x, g):
    return pl.pallas_call(
        _final_norm_kernel,
        grid=(N_TOK // TM,),
        in_specs=[pl.BlockSpec((TM, D_MODEL), lambda i: (i, 0)), _resident((1, D_MODEL))],
        out_specs=_path_specs(D_MODEL),
        out_shape=[jax.ShapeDtypeStruct((N_PROMPT_TOK, D_MODEL), F32),
                   jax.ShapeDtypeStruct((N_SAMPLE_TOK, D_MODEL), F32)],
        compiler_params=_cparams("arbitrary"),
        name="final_norm",
    )(x, g)


SUBLANE = 8
HY_TILE = 1024
assert N_PROMPT_TOK % HY_TILE == 0 and N_SAMPLE_TOK % HY_TILE == 0
assert SEQ & (SEQ - 1) == 0 and DEC_SEQ & (DEC_SEQ - 1) == 0
HY_BLOCK = 512
HY_ROWS = 64


def _dft_mats(blk):
    n = 2 * blk
    idx = jnp.arange(blk, dtype=jnp.int32)
    ang = (2.0 * math.pi / n) * ((idx[:, None] * idx[None, :]) % n).astype(F32)
    alt = jnp.where(idx % 2 == 0, 1.0, -1.0).astype(F32)
    return jnp.stack([jnp.cos(ang), jnp.where((idx == 0)[:, None], alt[None, :], -jnp.sin(ang))]).astype(BF16)


def _short_conv_kernel(x_ref, xp_ref, xn_ref, w_ref, b_ref, u_ref, vb_ref):
    i = pl.program_id(0)
    seq = jnp.where(i < N_PROMPT_TOK // HY_TILE, SEQ, DEC_SEQ)
    x = x_ref[...]
    row = lax.broadcasted_iota(jnp.int32, (HY_TILE, 1), 0)
    pos = (i * HY_TILE + row) & (seq - 1)
    x_prev = jnp.where(row == 0, xp_ref[SUBLANE - 1:SUBLANE, :], pltpu.roll(x, 1, axis=0))
    x_prev = jnp.where(pos == 0, 0.0, x_prev)
    x_next = jnp.where(row == HY_TILE - 1, xn_ref[0:1, :], pltpu.roll(x, HY_TILE - 1, axis=0))
    x_next = jnp.where(pos == seq - 1, 0.0, x_next)
    u = b_ref[...] + x_prev * w_ref[0:1, :] + x * w_ref[1:2, :] + x_next * w_ref[2:3, :]
    u_ref[...] = u
    vb_ref[...] = u[:, :HY_WIDTH].astype(BF16)


def _short_conv(zm, w, b):
    per8 = HY_TILE // SUBLANE
    last8 = N_TOK // SUBLANE - 1
    return pl.pallas_call(
        _short_conv_kernel,
        grid=(N_TOK // HY_TILE,),
        in_specs=[
            pl.BlockSpec((HY_TILE, HY_IN), lambda i: (i, 0)),
            pl.BlockSpec((SUBLANE, HY_IN), lambda i: (jnp.maximum(i * per8 - 1, 0), 0)),
            pl.BlockSpec((SUBLANE, HY_IN), lambda i: (jnp.minimum((i + 1) * per8, last8), 0)),
            pl.BlockSpec((HY_SHORT_K, HY_IN), lambda i: (0, 0)),
            pl.BlockSpec((1, HY_IN), lambda i: (0, 0)),
        ],
        out_specs=[
            pl.BlockSpec((HY_TILE, HY_IN), lambda i: (i, 0)),
            pl.BlockSpec((HY_TILE, HY_WIDTH), lambda i: (i, 0)),
        ],
        out_shape=[
            jax.ShapeDtypeStruct((N_TOK, HY_IN), F32),
            jax.ShapeDtypeStruct((N_TOK, HY_WIDTH), BF16),
        ],
        compiler_params=_cparams("arbitrary"),
        name="hyena_short_conv",
    )(zm, zm, zm, w, b)


def _block_conv_kernel(n_blk, last, zin_ref, k_ref, f_ref, g_ref, zp_ref, skip_ref, o_ref, *rest):
    z_ref, s_ref = rest[-2:]
    n = pl.program_id(1)
    blk = f_ref.shape[-1]

    @pl.when(n == 0)
    def _spectra():
        group = min(n_blk, 8)
        for g0 in range(0, n_blk, group):
            z = jnp.concatenate([zin_ref[(g0 + g) * blk:(g0 + g + 1) * blk, :] for g in range(group)], axis=1)
            for c in range(2):
                x = jnp.dot(f_ref[c], z, preferred_element_type=F32)
                for g in range(group):
                    z_ref[g0 + g, c] = x[:, g * HY_WIDTH:(g + 1) * HY_WIDTH]

    def product(r, first_row_packed):
        acc_r = acc_i = jnp.zeros((r.size, HY_WIDTH), F32)
        fix_r = fix_i = acc_r
        for i in range(n_blk):
            d = n - i + n_blk
            zr, zi = z_ref[i, 0, r, :], z_ref[i, 1, r, :]
            kr, ki = k_ref[d, 0, r, :], k_ref[d, 1, r, :]
            acc_r = acc_r + (zr * kr - zi * ki)
            acc_i = acc_i + (zr * ki + zi * kr)
            if first_row_packed:
                fix_r = fix_r + zr * kr
                fix_i = fix_i + zi * ki
        if first_row_packed:
            row0 = lax.broadcasted_iota(jnp.int32, (r.size, 1), 0) == 0
            acc_r = jnp.where(row0, fix_r, acc_r)
            acc_i = jnp.where(row0, fix_i, acc_i)
        s_ref[0, r, :] = acc_r.astype(BF16)
        s_ref[1, r, :] = acc_i.astype(BF16)

    def chunk(c, carry):
        product(pl.ds(pl.multiple_of(c * HY_ROWS, HY_ROWS), HY_ROWS), False)
        return carry

    lax.fori_loop(0, blk // HY_ROWS, chunk, 0)
    product(pl.ds(0, 2 * SUBLANE), True)

    conv_c = jnp.dot(f_ref[0], s_ref[0], preferred_element_type=F32)
    conv_s = jnp.dot(f_ref[1], s_ref[1], preferred_element_type=F32)
    t = lax.broadcasted_iota(jnp.int32, (blk, 1), 0)
    alt = jnp.where(t % 2 == 0, 1.0, -1.0).astype(F32)
    conv = conv_c + jnp.where(t == 0, 0.0, conv_s) + alt * s_ref[1, 0:1, :].astype(F32)
    z = g_ref[...] * (conv + skip_ref[...] * zp_ref[...])
    o_ref[...] = z
    if not last:
        rest[0][...] = z.astype(BF16)


def _block_conv(zb, zb_row0, tables, order, mats, u, gate_col, z_prev, z_prev_row0, skip, row0, n_seq, seq_len, last):
    blk = mats.shape[-1]
    n_blk = seq_len // blk

    def rows(base):
        return lambda b, n: (base // blk + b * n_blk + n, 0)

    out_spec = pl.BlockSpec((blk, HY_WIDTH), rows(0))
    out_shape = [jax.ShapeDtypeStruct((n_seq * n_blk * blk, HY_WIDTH), F32)]
    if not last:
        out_shape.append(jax.ShapeDtypeStruct((n_seq * n_blk * blk, HY_WIDTH), BF16))
    return pl.pallas_call(
        functools.partial(_block_conv_kernel, n_blk, last),
        grid=(n_seq, n_blk),
        in_specs=[
            pl.BlockSpec((seq_len, HY_WIDTH), lambda b, n: (zb_row0 // seq_len + b, 0)),
            pl.BlockSpec((None,) + tables.shape[1:], lambda b, n: (order, 0, 0, 0, 0), pipeline_mode=pl.Buffered(1)),
            pl.BlockSpec((2, blk, blk), lambda b, n: (0, 0, 0)),
            pl.BlockSpec((blk, HY_WIDTH), lambda b, n: (row0 // blk + b * n_blk + n, gate_col)),
            pl.BlockSpec((blk, HY_WIDTH), rows(z_prev_row0)),
            pl.BlockSpec((1, HY_WIDTH), lambda b, n: (0, 0)),
        ],
        out_specs=[out_spec] * len(out_shape),
        out_shape=out_shape,
        scratch_shapes=[pltpu.VMEM((n_blk, 2, blk, HY_WIDTH), F32), pltpu.VMEM((2, blk, HY_WIDTH), BF16)],
        compiler_params=_cparams("arbitrary", "arbitrary"),
        name="hyena_block_conv",
    )(zb, tables, mats, u, z_prev, skip)


def _taps_kernel(seq_len, rows, w1_ref, b1_ref, w2_ref, b2_ref, w3_ref, freq_ref, decay_ref, k_ref, norm_ref):
    i = pl.program_id(0)
    lag = i * rows + lax.broadcasted_iota(jnp.int32, (rows, 1), 0) - seq_len
    t = jnp.abs(lag).astype(F32) * (1.0 / (seq_len - 1))
    pos = jnp.abs(i * rows + lax.broadcasted_iota(jnp.int32, (1, rows), 1) - seq_len).astype(F32)
    band = lax.broadcasted_iota(jnp.int32, (HY_POS_BANDS, 1), 0).astype(F32)
    band = 1e-4 + band * ((HY_POS_BANDS - 1 - 1e-4) / (HY_POS_BANDS - 1))
    ang = band * (pos * (2.0 * math.pi / seq_len))
    dot = functools.partial(jnp.dot, precision=lax.Precision.HIGHEST, preferred_element_type=F32)
    pre = w1_ref[:, 0:1] * (pos * (1.0 / (seq_len - 1)))
    pre = pre + dot(w1_ref[:, 1:1 + HY_POS_BANDS], jnp.cos(ang))
    pre = pre - dot(w1_ref[:, 1 + HY_POS_BANDS:], jnp.sin(ang))
    freq = freq_ref[...]
    h = jnp.sin(freq * (pre + b1_ref[...]))
    h = jnp.sin(freq * (dot(w2_ref[...], h) + b2_ref[...]))
    h = lax.dot_general(h.astype(BF16), w3_ref[...].astype(BF16), (((0,), (0,)), ((), ())),
                        preferred_element_type=F32)
    h = h * jnp.exp(-t * jnp.abs(decay_ref[...]))

    @pl.when(i == 0)
    def _():
        norm_ref[...] = jnp.zeros_like(norm_ref)

    for o in range(HY_ORDER):
        c0 = o * 2 * HY_WIDTH
        k = jnp.where(lag >= 0, h[:, c0:c0 + HY_WIDTH], h[:, c0 + HY_WIDTH:c0 + 2 * HY_WIDTH])
        k = jnp.where(lag == -seq_len, 0.0, k)
        k_ref[o] = k.astype(BF16)
        norm_ref[:, o * HY_WIDTH:(o + 1) * HY_WIDTH] += jnp.sum(jnp.abs(k), axis=0, keepdims=True)


def _hyena_taps(seq_len, w1, b1, w2, b2, w3, freq, decay):
    rows = min(2 * seq_len, 512)
    full = lambda a: pl.BlockSpec(a.shape, lambda i: (0,) * a.ndim)
    args = (w1.T, b1.reshape(-1, 1), w2.T, b2.reshape(-1, 1), w3, freq.reshape(-1, 1), decay.reshape(1, -1))
    return pl.pallas_call(
        functools.partial(_taps_kernel, seq_len, rows),
        grid=(2 * seq_len // rows,),
        in_specs=[full(a) for a in args],
        out_specs=[pl.BlockSpec((HY_ORDER, rows, HY_WIDTH), lambda i: (0, i, 0)),
                   pl.BlockSpec((1, HY_ORDER * HY_WIDTH), lambda i: (0, 0))],
        out_shape=[jax.ShapeDtypeStruct((HY_ORDER, 2 * seq_len, HY_WIDTH), BF16),
                   jax.ShapeDtypeStruct((1, HY_ORDER * HY_WIDTH), F32)],
        compiler_params=_cparams("arbitrary"),
        name="hyena_taps",
    )(*args)


def _tables_kernel(group, x_ref, norm_ref, f_ref, o_ref, prev_ref):
    blk = f_ref.shape[-1]

    @pl.when(pl.program_id(1) == 0)
    def _():
        prev_ref[...] = jnp.zeros_like(prev_ref)

    f = lax.broadcasted_iota(jnp.int32, (blk, 1), 0)
    alt = jnp.where(f % 2 == 0, 1.0, -1.0).astype(F32)
    scale = jnp.where(f == 0, 1.0, 2.0).astype(F32) * (1.0 / (2 * blk)) / norm_ref[...]
    x = jnp.concatenate([x_ref[g * blk:(g + 1) * blk, :] for g in range(group)], axis=1)
    for c in range(2):
        xc = jnp.dot(f_ref[c], x, preferred_element_type=F32)
        prev = prev_ref[c]
        for g in range(group):
            cur = xc[:, g * HY_WIDTH:(g + 1) * HY_WIDTH]
            o_ref[g, c] = (cur + alt * prev) * scale
            prev = cur
        prev_ref[c] = prev


def _hyena_tables(seq_len, mats, filt):
    blk = mats.shape[-1]
    n_lag = 2 * seq_len // blk
    group = min(n_lag, 8)
    taps, norm = _hyena_taps(seq_len, *filt)
    return pl.pallas_call(
        functools.partial(_tables_kernel, group),
        grid=(HY_ORDER, n_lag // group),
        in_specs=[
            pl.BlockSpec((None, group * blk, HY_WIDTH), lambda o, q: (o, q, 0)),
            pl.BlockSpec((1, HY_WIDTH), lambda o, q: (0, o)),
            pl.BlockSpec((2, blk, blk), lambda o, q: (0, 0, 0)),
        ],
        out_specs=pl.BlockSpec((None, group, 2, blk, HY_WIDTH), lambda o, q: (o, q, 0, 0, 0)),
        out_shape=jax.ShapeDtypeStruct((HY_ORDER, n_lag, 2, blk, HY_WIDTH), F32),
        scratch_shapes=[pltpu.VMEM((2, blk, HY_WIDTH), F32)],
        compiler_params=_cparams("arbitrary", "arbitrary"),
        name="hyena_tables",
    )(taps, norm, mats)


def _hyena_path(u, vb, row0, n_seq, seq_len, mats, tables, skip):
    z1, z1b = _block_conv(vb, row0, tables, 0, mats, u, 1, u, row0, skip[0:1], row0, n_seq, seq_len, False)
    (ya,) = _block_conv(z1b, 0, tables, 1, mats, u, 2, z1, 0, skip[1:2], row0, n_seq, seq_len, True)
    return ya


S5_LANES = S5_GROUPS * S5_STATE
S5_SEGS = 8
S5_CHUNKS = 4


def _s5_kernel(n_tiles, tile, u0_ref, u1_ref, h0_ref, lam_ref, bd_ref, cd_ref, skip_ref, gw_ref, gb_ref,
               y_ref, fin_ref, bu_ref, ybwd_ref, p_ref, p8_ref, cm_ref, fl_ref, carry_ref, up_ref, st0_ref, st1_ref,
               ub_ref, yacc_ref):
    seg = tile // S5_SEGS
    n = S5_LANES
    b = pl.program_id(0)
    j = pl.program_id(1)

    @pl.when((b == 0) & (j == 0))
    def _powers():
        for d in range(2):
            lr = lam_ref[2 * d:2 * d + 1, :]
            li = lam_ref[2 * d + 1:2 * d + 2, :]

            def put(k, pr, pi, d=d):
                p_ref[d, pl.ds(k, 1), 0:n] = pr
                p_ref[d, pl.ds(k, 1), n:2 * n] = pi
                r = pl.ds(pl.multiple_of(k * S5_SEGS, S5_SEGS), S5_SEGS)
                p8_ref[d, r, 0:n] = jnp.broadcast_to(pr, (S5_SEGS, n))
                p8_ref[d, r, n:2 * n] = jnp.broadcast_to(pi, (S5_SEGS, n))

            put(0, lr, li)

            def body(k, c, lr=lr, li=li, put=put):
                pr, pi = c
                nr = pr * lr - pi * li
                ni = pr * li + pi * lr
                put(k, nr, ni)
                return nr, ni

            lax.fori_loop(1, seg, body, (lr, li))

    def sweep(d, t):
        first, last = (0, n_tiles - 1) if d == 0 else (n_tiles - 1, 0)

        @pl.when(t == first)
        def _init():
            carry_ref[0:1, 0:n] = h0_ref[2 * d:2 * d + 1, :]
            carry_ref[0:1, n:2 * n] = h0_ref[2 * d + 1:2 * d + 2, :]

        def gather(i, c):
            r = pl.ds(pl.multiple_of(i * S5_SEGS, S5_SEGS), S5_SEGS)
            up_ref[r, 0:LANE] = u0_ref[pl.ds(i, S5_SEGS, stride=seg), :]
            up_ref[r, LANE:2 * LANE] = u1_ref[pl.ds(i, S5_SEGS, stride=seg), :]
            return c

        lax.fori_loop(0, seg, gather, 0, unroll=2)
        ub_ref[...] = up_ref[...].astype(BF16)
        nq = n // S5_CHUNKS

        def lanes(q):
            return slice(q * nq, (q + 1) * nq), slice(n + q * nq, n + (q + 1) * nq)

        def rows(i):
            k = i if d == 0 else seg - 1 - i
            return slice(k * S5_SEGS, (k + 1) * S5_SEGS)

        def project_in(q):
            re, im = lanes(q)
            bu_ref[:, re] = jnp.dot(ub_ref[...], bd_ref[d, :, re], preferred_element_type=F32)
            bu_ref[:, im] = jnp.dot(ub_ref[...], bd_ref[d, :, im], preferred_element_type=F32)

        def scan(q):
            re, im = lanes(q)
            lr8 = jnp.broadcast_to(lam_ref[2 * d:2 * d + 1, re], (S5_SEGS, nq))
            li8 = jnp.broadcast_to(lam_ref[2 * d + 1:2 * d + 2, re], (S5_SEGS, nq))
            hr = hi = jnp.zeros((S5_SEGS, nq), F32)
            for i in range(seg):
                r = rows(i)
                hr, hi = lr8 * hr - li8 * hi + bu_ref[r, re], lr8 * hi + li8 * hr + bu_ref[r, im]
                bu_ref[r, re] = hr
                bu_ref[r, im] = hi
            fl_ref[:, re] = hr
            fl_ref[:, im] = hi

            plr = p_ref[d, seg - 1:seg, re]
            pli = p_ref[d, seg - 1:seg, im]
            cr = carry_ref[0:1, re]
            ci = carry_ref[0:1, im]
            for s in (range(S5_SEGS) if d == 0 else reversed(range(S5_SEGS))):
                cm_ref[s:s + 1, re] = cr
                cm_ref[s:s + 1, im] = ci
                cr, ci = (plr * cr - pli * ci + fl_ref[s:s + 1, re], plr * ci + pli * cr + fl_ref[s:s + 1, im])
            carry_ref[0:1, re] = cr
            carry_ref[0:1, im] = ci

            cmr = cm_ref[:, re]
            cmi = cm_ref[:, im]
            for i in range(seg):
                r = rows(i)
                pw = slice(i * S5_SEGS, (i + 1) * S5_SEGS)
                pr = p8_ref[d, pw, re]
                pi = p8_ref[d, pw, im]
                bu_ref[r, re] = bu_ref[r, re] + (pr * cmr - pi * cmi)
                bu_ref[r, im] = bu_ref[r, im] + (pr * cmi + pi * cmr)

        def project_out(q):
            re, im = lanes(q)
            part = (jnp.dot(bu_ref[:, re].astype(BF16), cd_ref[d, re, :], preferred_element_type=F32)
                    + jnp.dot(bu_ref[:, im].astype(BF16), cd_ref[d, im, :], preferred_element_type=F32))
            yacc_ref[...] = part if q == 0 else yacc_ref[...] + part

        project_in(0)
        for q in range(S5_CHUNKS):
            if q + 1 < S5_CHUNKS:
                project_in(q + 1)
            if q > 0:
                project_out(q - 1)
            scan(q)
        project_out(S5_CHUNKS - 1)
        y = yacc_ref[...]

        @pl.when(t == last)
        def _final():
            fin_ref[2 * d:2 * d + 1, :] = carry_ref[0:1, 0:n]
            fin_ref[2 * d + 1:2 * d + 2, :] = carry_ref[0:1, n:2 * n]

        trow = pl.ds(pl.multiple_of(t * tile, tile), tile)
        if d == 1:
            ybwd_ref[trow, :] = y
        else:
            tot = up_ref[...] * skip_ref[...] + y + ybwd_ref[trow, :]
            g = jnp.dot(tot.astype(BF16), gw_ref[...], preferred_element_type=F32) + gb_ref[...]
            res = g[:, :S5_WIDTH] * jax.nn.sigmoid(g[:, S5_WIDTH:])
            st0_ref[...] = res[:, 0:LANE]
            st1_ref[...] = res[:, LANE:2 * LANE]
            per_seg = seg // S5_SEGS

            def unpermute(m, c):
                s = m // per_seg
                i0 = (m % per_seg) * S5_SEGS
                src = pl.ds(i0 * S5_SEGS + s, S5_SEGS, stride=S5_SEGS)
                dst = pl.ds(pl.multiple_of(m * S5_SEGS, S5_SEGS), S5_SEGS)
                y_ref[dst, 0:LANE] = st0_ref[src, :]
                y_ref[dst, LANE:2 * LANE] = st1_ref[src, :]
                return c

            lax.fori_loop(0, tile // S5_SEGS, unpermute, 0, unroll=2)

    @pl.when(j < n_tiles)
    def _bwd():
        sweep(1, n_tiles - 1 - j)

    @pl.when(j >= n_tiles)
    def _fwd():
        sweep(0, j - n_tiles)


def _s5_branch(zm, row0, n_seq, seq_len, tile, h0, lam, bd, cd, skip, gw, gb):
    n_tiles = seq_len // tile
    blk0 = row0 // tile
    col = SPLIT_S5 // LANE
    seg = tile // S5_SEGS

    def tile_of(j):
        return jnp.where(j < n_tiles, n_tiles - 1 - j, j - n_tiles)

    return pl.pallas_call(
        functools.partial(_s5_kernel, n_tiles, tile),
        grid=(n_seq, 2 * n_tiles),
        in_specs=[
            pl.BlockSpec((tile, LANE), lambda b, j: (blk0 + b * n_tiles + tile_of(j), col)),
            pl.BlockSpec((tile, LANE), lambda b, j: (blk0 + b * n_tiles + tile_of(j), col + 1)),
            pl.BlockSpec((None, 4, S5_LANES), lambda b, j: (b, 0, 0)),
            pl.BlockSpec((4, S5_LANES), lambda b, j: (0, 0)),
            pl.BlockSpec((2, S5_WIDTH, 2 * S5_LANES), lambda b, j: (0, 0, 0)),
            pl.BlockSpec((2, 2 * S5_LANES, S5_WIDTH), lambda b, j: (0, 0, 0)),
            pl.BlockSpec((1, S5_WIDTH), lambda b, j: (0, 0)),
            pl.BlockSpec((S5_WIDTH, 2 * S5_WIDTH), lambda b, j: (0, 0)),
            pl.BlockSpec((1, 2 * S5_WIDTH), lambda b, j: (0, 0)),
        ],
        out_specs=[
            pl.BlockSpec((tile, S5_WIDTH), lambda b, j: (b * n_tiles + jnp.maximum(j - n_tiles, 0), 0)),
            pl.BlockSpec((None, 4, S5_LANES), lambda b, j: (b, 0, 0)),
        ],
        out_shape=[
            jax.ShapeDtypeStruct((n_seq * seq_len, S5_WIDTH), F32),
            jax.ShapeDtypeStruct((n_seq, 4, S5_LANES), F32),
        ],
        scratch_shapes=[
            pltpu.VMEM((tile, 2 * S5_LANES), F32),
            pltpu.VMEM((seq_len, S5_WIDTH), F32),
            pltpu.VMEM((2, seg, 2 * S5_LANES), F32),
            pltpu.VMEM((2, tile, 2 * S5_LANES), F32),
            pltpu.VMEM((S5_SEGS, 2 * S5_LANES), F32),
            pltpu.VMEM((S5_SEGS, 2 * S5_LANES), F32),
            pltpu.VMEM((1, 2 * S5_LANES), F32),
            pltpu.VMEM((tile, S5_WIDTH), F32),
            pltpu.VMEM((tile, LANE), F32),
            pltpu.VMEM((tile, LANE), F32),
            pltpu.VMEM((tile, S5_WIDTH), BF16),
            pltpu.VMEM((tile, S5_WIDTH), F32),
        ],
        compiler_params=_cparams("arbitrary", "arbitrary"),
        name="s5_scan",
    )(zm, zm, h0, lam, bd, cd, skip, gw, gb)


def _s5_params(lam_re, lam_im, log_step, b_re, b_im, c_re, c_im):
    step = jnp.exp(log_step)[..., None]
    mag = jnp.exp(lam_re * step)
    bar_re, bar_im = mag * jnp.cos(lam_im * step), mag * jnp.sin(lam_im * step)
    den = lam_re * lam_re + lam_im * lam_im
    q_re = ((bar_re - 1.0) * lam_re + bar_im * lam_im) / den
    q_im = (bar_im * lam_re - (bar_re - 1.0) * lam_im) / den
    bb_re = q_re[..., None] * b_re - q_im[..., None] * b_im
    bb_im = q_re[..., None] * b_im + q_im[..., None] * b_re
    eye = jnp.eye(S5_GROUPS, dtype=F32)

    def in_mat(x):
        xt = jnp.swapaxes(x, 2, 3)
        return (xt[:, :, :, None, :] * eye[None, :, None, :, None]).reshape(2, S5_WIDTH, S5_LANES)

    def out_mat(x):
        xt = jnp.swapaxes(x, 2, 3)
        return (xt[:, :, :, None, :] * eye[None, :, None, :, None]).reshape(2, S5_LANES, S5_WIDTH)

    bd = jnp.concatenate([in_mat(bb_re), in_mat(bb_im)], axis=2).astype(BF16)
    cd = jnp.concatenate([out_mat(c_re), out_mat(-c_im)], axis=1).astype(BF16)
    lam_rows = jnp.stack([bar_re[0], bar_im[0], bar_re[1], bar_im[1]])
    return lam_rows.reshape(4, S5_LANES), bd, cd


ROPE_HALF = HEAD_DIM // 4
CTX_SEQS = 4


def _rope_tables(seq_len):
    t = jnp.arange(seq_len)
    row = (t // GRID_W).astype(F32)
    col = (t % GRID_W).astype(F32)
    inv = ROPE_BASE ** (-jnp.arange(ROPE_HALF, dtype=F32) / ROPE_HALF)
    ang_r = row[:, None] * inv
    ang_c = col[:, None] * inv
    cos = jnp.concatenate([jnp.cos(ang_r), jnp.cos(ang_r), jnp.cos(ang_c), jnp.cos(ang_c)], axis=-1)
    sin = jnp.concatenate([-jnp.sin(ang_r), jnp.sin(ang_r), -jnp.sin(ang_c), jnp.sin(ang_c)], axis=-1)
    return jnp.tile(cos, (1, LANE // HEAD_DIM)), jnp.tile(sin, (1, LANE // HEAD_DIM))


def _rope(x, cos, sin):
    lane = lax.broadcasted_iota(jnp.int32, x.shape, 1)
    first = (lane % (2 * ROPE_HALF)) < ROPE_HALF
    partner = jnp.where(first, pltpu.roll(x, LANE - ROPE_HALF, axis=1), pltpu.roll(x, ROPE_HALF, axis=1))
    return x * cos + partner * sin


def _gqa(q, ks, vs, biases, sink_ref):
    t = q.shape[0]
    low = lax.broadcasted_iota(jnp.int32, (1, LANE), 1) < HEAD_DIM
    assert N_KV_HEADS * HEAD_DIM == LANE
    k_nat = [k.astype(BF16) for k in ks]
    k_swp = [pltpu.roll(k, HEAD_DIM, axis=1).astype(BF16) for k in ks]
    v_aug = [jnp.concatenate([v, jnp.ones_like(v)], axis=1).astype(BF16) for v in vs]
    outs = {}
    for keys, heads in ((k_nat, [h for h in range(N_HEADS) if h % 2 == h // Q_PER_KV]),
                        (k_swp, [h for h in range(N_HEADS) if h % 2 != h // Q_PER_KV])):
        scale = HEAD_DIM ** -0.5
        assert math.frexp(scale)[0] == 0.5
        rows = [jnp.where(low if h % 2 == 0 else jnp.logical_not(low),
                          q[:, (h // 2) * LANE:(h // 2 + 1) * LANE] * scale, 0.0) for h in heads]
        qs = jnp.concatenate(rows, axis=0).astype(BF16)
        sink_col = jnp.concatenate([jnp.full((t, 1), sink_ref[h], F32) for h in heads], axis=0)
        ss = []
        m = sink_col
        for k, bias in zip(keys, biases):
            s = lax.dot_general(qs, k, (((1,), (1,)), ((), ())), preferred_element_type=F32)
            if bias is not None:
                s = s + bias
            ss.append(s)
            m = jnp.maximum(m, jnp.max(s, axis=-1, keepdims=True))
        acc = jnp.zeros((len(heads) * t, 2 * LANE), F32)
        for s, v in zip(ss, v_aug):
            acc = acc + jnp.dot(jnp.exp(s - m).astype(BF16), v, preferred_element_type=F32)
        o = acc[:, :LANE] / (acc[:, LANE:LANE + 1] + jnp.exp(sink_col - m))
        for j, h in enumerate(heads):
            outs[h] = o[j * t:(j + 1) * t]
    chunks = []
    for c in range(N_HEADS // 2):
        halves = []
        for h in (2 * c, 2 * c + 1):
            halves.append(outs[h] if h // Q_PER_KV == h % 2 else pltpu.roll(outs[h], HEAD_DIM, axis=1))
        chunks.append(jnp.where(low, halves[0], halves[1]))
    return jnp.concatenate(chunks, axis=1)


def _ctx_attn_kernel(q_ref, k_ref, v_ref, sink_ref, kin_ref, vin_ref, o_ref, kout_ref, vout_ref):
    del kin_ref, vin_ref
    for g in range(CTX_SEQS):
        rows = slice(g * SEQ, (g + 1) * SEQ)
        k, v = k_ref[rows, :], v_ref[rows, :]
        kout_ref[g] = k
        vout_ref[g] = v
        o_ref[rows, :] = _gqa(q_ref[rows, :], [k], [v], [None], sink_ref)


def _context_attention(zm, sink, layer, new_k, new_v):
    slab = pl.BlockSpec((CTX_SEQS, None, SEQ, KV_WIDTH), lambda b: (b, layer, 0, 0))
    rows = CTX_SEQS * SEQ
    return pl.pallas_call(
        _ctx_attn_kernel,
        grid=(BATCH // CTX_SEQS,),
        in_specs=[
            pl.BlockSpec((rows, ATTN_WIDTH), lambda b: (b, SPLIT_Q // ATTN_WIDTH)),
            pl.BlockSpec((rows, KV_WIDTH), lambda b: (b, SPLIT_K // KV_WIDTH)),
            pl.BlockSpec((rows, KV_WIDTH), lambda b: (b, SPLIT_V // KV_WIDTH)),
            pl.BlockSpec(memory_space=pltpu.SMEM),
            pl.BlockSpec(memory_space=pl.ANY),
            pl.BlockSpec(memory_space=pl.ANY),
        ],
        out_specs=[pl.BlockSpec((rows, ATTN_WIDTH), lambda b: (b, 0)), slab, slab],
        out_shape=[jax.ShapeDtypeStruct((N_PROMPT_TOK, ATTN_WIDTH), F32),
                   jax.ShapeDtypeStruct(new_k.shape, F32), jax.ShapeDtypeStruct(new_v.shape, F32)],
        input_output_aliases={4: 1, 5: 2},
        compiler_params=_cparams("arbitrary"),
        name="ctx_attention",
    )(zm, zm, zm, sink, new_k, new_v)


def _latent_attn_kernel(*refs):
    per_seq = 9
    cos_ref, sin_ref, band_ref, sink_ref, o_ref = refs[DEC_BATCH * per_seq:]
    i = pl.program_id(0)
    nb = DEC_SEQ // BLOCK

    def table(ref, blk):
        return ref[pl.ds(pl.multiple_of(blk * BLOCK, BLOCK), BLOCK), :]

    left, right = jnp.maximum(i - 1, 0), jnp.minimum(i + 1, nb - 1)
    cos_q, sin_q = table(cos_ref, i), table(sin_ref, i)
    cos_l, sin_l = table(cos_ref, left), table(sin_ref, left)
    cos_r, sin_r = table(cos_ref, right), table(sin_ref, right)

    c = lax.broadcasted_iota(jnp.int32, (1, 3 * BLOCK), 1)
    outside = ((c < BLOCK) & (i == 0)) | ((c >= 2 * BLOCK) & (i == nb - 1))
    bias = band_ref[...] + jnp.where(outside, NEG_INF, 0.0)

    for b in range(DEC_BATCH):
        q_ref, kl_ref, kc_ref, kr_ref, vl_ref, vc_ref, vr_ref, ck_ref, cv_ref = refs[b * per_seq:(b + 1) * per_seq]
        q = jnp.concatenate([_rope(q_ref[:, c0:c0 + LANE], cos_q, sin_q) for c0 in range(0, ATTN_WIDTH, LANE)], axis=1)
        kw = jnp.concatenate([_rope(kl_ref[...], cos_l, sin_l), _rope(kc_ref[...], cos_q, sin_q),
                              _rope(kr_ref[...], cos_r, sin_r)], axis=0)
        vw = jnp.concatenate([vl_ref[...], vc_ref[...], vr_ref[...]], axis=0)
        o_ref[b] = _gqa(q, [ck_ref[...], kw], [cv_ref[...], vw], [None, bias], sink_ref)


def _latent_attention(zm, cache_k, cache_v, layer, cos, sin, sink):
    nb = DEC_SEQ // BLOCK
    blk0 = N_PROMPT_TOK // BLOCK
    kcol, vcol = SPLIT_K // KV_WIDTH, SPLIT_V // KV_WIDTH

    def seq_specs(b):
        def row(i):
            return blk0 + b * nb + i

        def win_specs(col):
            return [pl.BlockSpec((BLOCK, KV_WIDTH), lambda i: (row(jnp.maximum(i - 1, 0)), col)),
                    pl.BlockSpec((BLOCK, KV_WIDTH), lambda i: (row(i), col)),
                    pl.BlockSpec((BLOCK, KV_WIDTH), lambda i: (row(jnp.minimum(i + 1, nb - 1)), col))]

        ctx_spec = pl.BlockSpec((None, None, PAST_LEN, KV_WIDTH), lambda i: (b, layer, 0, 0))
        return ([pl.BlockSpec((BLOCK, ATTN_WIDTH), lambda i: (row(i), SPLIT_Q // ATTN_WIDTH))]
                + win_specs(kcol) + win_specs(vcol) + [ctx_spec, ctx_spec])

    tab_spec = pl.BlockSpec((DEC_SEQ, LANE), lambda i: (0, 0))
    r = jnp.arange(Q_PER_KV * BLOCK)[:, None] % BLOCK
    c = jnp.arange(3 * BLOCK)[None, :]
    band = jnp.where(jnp.abs(r + BLOCK - c) <= WINDOW, 0.0, NEG_INF).astype(F32)
    seq_args = (zm,) * 7 + (cache_k, cache_v)
    out = pl.pallas_call(
        _latent_attn_kernel,
        grid=(nb,),
        in_specs=[spec for b in range(DEC_BATCH) for spec in seq_specs(b)]
        + [tab_spec, tab_spec, pl.BlockSpec(band.shape, lambda i: (0, 0)), pl.BlockSpec(memory_space=pltpu.SMEM)],
        out_specs=pl.BlockSpec((DEC_BATCH, BLOCK, ATTN_WIDTH), lambda i: (0, i, 0)),
        out_shape=jax.ShapeDtypeStruct((DEC_BATCH, DEC_SEQ, ATTN_WIDTH), F32),
        compiler_params=_cparams("arbitrary"),
        name="latent_attention",
    )(*(seq_args * DEC_BATCH), cos, sin, band, sink)
    return out.reshape(N_SAMPLE_TOK, ATTN_WIDTH)


def kernel(x_prompt, x_sample, cache_k, cache_v, state_s5_re, state_s5_im, c, c_ctx, ada_w, ada_b, norm1_g, w_in, hy_conv_w, hy_conv_b, hy_pos_w1, hy_pos_b1, hy_pos_w2, hy_pos_b2, hy_pos_w3, hy_sin_freq, hy_decay, hy_skip, s5_lam_re, s5_lam_im, s5_log_step, s5_b_re, s5_b_im, s5_c_re, s5_c_im, s5_skip, s5_glu_w, s5_glu_b, attn_sink, proj_a, proj_b, proj_c, w_out, norm2_g, ffn_w_gate, ffn_w_up, ffn_w_down, final_norm_g):
    cvec = jnp.concatenate([c_ctx[None], c, jnp.zeros((MOD_ROWS - 1 - DEC_BATCH, D_MODEL), F32)], axis=0)
    mod = _modulation(cvec, ada_w, ada_b).reshape(DEPTH * MOD_ROWS * 6, 1, D_MODEL)

    x = jnp.concatenate([x_prompt.reshape(N_PROMPT_TOK, D_MODEL), x_sample.reshape(N_SAMPLE_TOK, D_MODEL)], axis=0)
    w_main = w_in[:, :, :MAIN_IN].astype(BF16)
    w_gate = w_in[:, :, MAIN_IN:].astype(BF16)
    pa, pb, pc, wo = proj_a.astype(BF16), proj_b.astype(BF16), proj_c.astype(BF16), w_out.astype(BF16)
    wg, wu, wd = ffn_w_gate.astype(BF16), ffn_w_up.astype(BF16), ffn_w_down.astype(BF16)
    g1, g2 = norm1_g.reshape(DEPTH, 1, D_MODEL), norm2_g.reshape(DEPTH, 1, D_MODEL)
    glu_w = s5_glu_w.astype(BF16)

    ctx_k = cache_k.reshape(DEC_BATCH, DEPTH, PAST_LEN, KV_WIDTH)
    ctx_v = cache_v.reshape(DEC_BATCH, DEPTH, PAST_LEN, KV_WIDTH)
    rope_cos, rope_sin = _rope_tables(DEC_SEQ)
    mats_p, mats_s = _dft_mats(min(SEQ, HY_BLOCK)), _dft_mats(min(DEC_SEQ, HY_BLOCK))

    new_k = jnp.zeros((BATCH, DEPTH, SEQ, KV_WIDTH), F32)
    new_v = jnp.zeros((BATCH, DEPTH, SEQ, KV_WIDTH), F32)
    sre_out, sim_out = [], []
    for l in range(DEPTH):
        filt = (hy_pos_w1[l], hy_pos_b1[l], hy_pos_w2[l], hy_pos_b2[l], hy_pos_w3[l], hy_sin_freq[l], hy_decay[l])
        zm = _in_proj(x, mod, l, g1, w_main)
        u, vb = _short_conv(zm, hy_conv_w[l], hy_conv_b[l].reshape(1, HY_IN))
        ya_p = _hyena_path(u, vb, 0, BATCH, SEQ, mats_p, _hyena_tables(SEQ, mats_p, filt), hy_skip[l])
        ya_s = _hyena_path(u, vb, N_PROMPT_TOK, DEC_BATCH, DEC_SEQ, mats_s, _hyena_tables(DEC_SEQ, mats_s, filt),
                           hy_skip[l])
        yc_p, new_k, new_v = _context_attention(zm, attn_sink[l], l, new_k, new_v)
        yc_s = _latent_attention(zm, ctx_k, ctx_v, l, rope_cos, rope_sin, attn_sink[l])

        lam, bd, cd = _s5_params(s5_lam_re[l], s5_lam_im[l], s5_log_step[l], s5_b_re[l], s5_b_im[l],
                                 s5_c_re[l], s5_c_im[l])
        s5_w = (lam, bd, cd, s5_skip[l].reshape(1, S5_WIDTH), glu_w[l], s5_glu_b[l].reshape(1, 2 * S5_WIDTH))
        h0_s = jnp.stack([state_s5_re[:, l, 0], state_s5_im[:, l, 0], state_s5_re[:, l, 1], state_s5_im[:, l, 1]],
                         axis=1).reshape(DEC_BATCH, 4, S5_LANES)
        yb_p, fin_p = _s5_branch(zm, 0, BATCH, SEQ, SEQ, jnp.zeros((BATCH, 4, S5_LANES), F32), *s5_w)
        yb_s, _ = _s5_branch(zm, N_PROMPT_TOK, DEC_BATCH, DEC_SEQ, 512, h0_s, *s5_w)
        fin_p = fin_p.reshape(BATCH, 2, 2, S5_GROUPS, S5_STATE)

        sre_out.append(fin_p[:, :, 0])
        sim_out.append(fin_p[:, :, 1])
        x = _merge((ya_p, ya_s), (yb_p, yb_s), (yc_p, yc_s), x, mod, l, g1, w_gate, pa, pb, pc, wo)
        x = _ffn(x, mod, l, g2, wg, wu, wd)

    y_prompt, y_sample = _final_norm(x, final_norm_g.reshape(1, D_MODEL))
    return (y_prompt.reshape(BATCH, SEQ, D_MODEL), y_sample.reshape(DEC_BATCH, DEC_SEQ, D_MODEL),
            new_k.reshape(BATCH, DEPTH, SEQ, N_KV_HEADS, HEAD_DIM), new_v.reshape(BATCH, DEPTH, SEQ, N_KV_HEADS, HEAD_DIM),
            jnp.stack(sre_out, axis=1), jnp.stack(sim_out, axis=1))
```

```python
import functools
import math

import jax
import jax.numpy as jnp
from jax import lax
from jax.experimental import pallas as pl
from jax.experimental.pallas import tpu as pltpu

D_MODEL = 1024
BATCH = 16
SEQ = 256
DEPTH = 4
DEC_BATCH = 2
DEC_SEQ = 4096
PAST_LEN = 512
GRID_W = 64
N_BRANCH = 3
HY_WIDTH = 256
HY_ORDER = 2
HY_SHORT_K = 3
HY_POS_EMB = 33
HY_POS_BANDS = (HY_POS_EMB - 1) // 2
S5_WIDTH = 256
S5_GROUP = 16
S5_GROUPS = S5_WIDTH // S5_GROUP
S5_STATE = 64
N_HEADS = 8
N_KV_HEADS = 2
Q_PER_KV = N_HEADS // N_KV_HEADS
HEAD_DIM = 64
ATTN_WIDTH = N_HEADS * HEAD_DIM
KV_WIDTH = N_KV_HEADS * HEAD_DIM
WINDOW = 128
BLOCK = 128
ROPE_BASE = 10000.0
FFN_HIDDEN = ((8 * D_MODEL // 3 + 255) // 256) * 256
HY_IN = (HY_ORDER + 1) * HY_WIDTH
GATE_IN = N_BRANCH * D_MODEL
MAIN_IN = HY_IN + S5_WIDTH + ATTN_WIDTH + 2 * KV_WIDTH
SPLIT_S5 = HY_IN
SPLIT_Q = SPLIT_S5 + S5_WIDTH
SPLIT_K = SPLIT_Q + ATTN_WIDTH
SPLIT_V = SPLIT_K + KV_WIDTH

F32 = jnp.float32
BF16 = jnp.bfloat16
EPS = 1e-6
NEG_INF = -1e30

N_PROMPT_TOK = BATCH * SEQ
N_SAMPLE_TOK = DEC_BATCH * DEC_SEQ
N_TOK = N_PROMPT_TOK + N_SAMPLE_TOK
LANE = 128
MOD_ROWS = 8
TM = 512
V7X_VMEM_BYTES = 64 * 1024 * 1024
VMEM_LIMIT = V7X_VMEM_BYTES * 7 // 8


def _cparams(*sem):
    return pltpu.CompilerParams(dimension_semantics=sem, vmem_limit_bytes=VMEM_LIMIT)


def _mod_row(i):
    n_p = N_PROMPT_TOK // TM
    per_b = DEC_SEQ // TM
    return jnp.where(i < n_p, 0, 1 + (i - n_p) // per_b)


def _mod_spec(layer, k):
    return pl.BlockSpec((None, 1, D_MODEL), lambda i: ((layer * MOD_ROWS + _mod_row(i)) * 6 + k, 0, 0))


def _resident(shape, layer=None):
    if layer is None:
        return pl.BlockSpec(shape, lambda i: (0,) * len(shape), pipeline_mode=pl.Buffered(1))
    return pl.BlockSpec((None,) + shape, lambda i: (layer,) + (0,) * len(shape), pipeline_mode=pl.Buffered(1))


def _rms(x, g):
    return x * lax.rsqrt(jnp.mean(x * x, axis=-1, keepdims=True) + EPS) * g


def _mod_kernel(c_ref, w_ref, b_ref, o_ref):
    c = c_ref[...]
    s = (c * jax.nn.sigmoid(c)).astype(BF16)
    o_ref[...] = jnp.dot(s, w_ref[...].astype(BF16), preferred_element_type=F32) + b_ref[...]


def _modulation(cvec, ada_w, ada_b):
    tn = 6 * D_MODEL // 4
    return pl.pallas_call(
        _mod_kernel,
        grid=(DEPTH, 6 * D_MODEL // tn),
        in_specs=[
            pl.BlockSpec((MOD_ROWS, D_MODEL), lambda l, j: (0, 0)),
            pl.BlockSpec((None, D_MODEL, tn), lambda l, j: (l, 0, j)),
            pl.BlockSpec((None, 1, tn), lambda l, j: (l, 0, j)),
        ],
        out_specs=pl.BlockSpec((None, MOD_ROWS, tn), lambda l, j: (l, 0, j)),
        out_shape=jax.ShapeDtypeStruct((DEPTH, MOD_ROWS, 6 * D_MODEL), F32),
        compiler_params=_cparams("arbitrary", "arbitrary"),
        name="adaln_mod",
    )(cvec, ada_w, ada_b.reshape(DEPTH, 1, 6 * D_MODEL))


def _inproj_kernel(x_ref, sh_ref, sc_ref, g_ref, wm_ref, zm_ref):
    h = (_rms(x_ref[...], g_ref[...]) * (1.0 + sc_ref[...]) + sh_ref[...]).astype(BF16)
    zm_ref[...] = jnp.dot(h, wm_ref[...], preferred_element_type=F32)


def _in_proj(x, mod, layer, g, w_main):
    return pl.pallas_call(
        _inproj_kernel,
        grid=(N_TOK // TM,),
        in_specs=[
            pl.BlockSpec((TM, D_MODEL), lambda i: (i, 0)),
            _mod_spec(layer, 0),
            _mod_spec(layer, 1),
            _resident((1, D_MODEL), layer),
            _resident((D_MODEL, MAIN_IN), layer),
        ],
        out_specs=pl.BlockSpec((TM, MAIN_IN), lambda i: (i, 0)),
        out_shape=jax.ShapeDtypeStruct((N_TOK, MAIN_IN), F32),
        compiler_params=_cparams("arbitrary"),
        name="in_proj",
    )(x, mod, mod, g, w_main)


N_PROMPT_TILES = N_PROMPT_TOK // TM


def _path_specs(width):
    return [pl.BlockSpec((TM, width), lambda i: (jnp.minimum(i, N_PROMPT_TILES - 1), 0)),
            pl.BlockSpec((TM, width), lambda i: (jnp.maximum(i - N_PROMPT_TILES, 0), 0))]


def _merge_kernel(yap_ref, yas_ref, ybp_ref, ybs_ref, ycp_ref, ycs_ref, x_ref, sh_ref, sc_ref, g1_ref, g_ref,
                  wg_ref, pa_ref, pb_ref, pc_ref, wo_ref, o_ref):
    is_prompt = pl.program_id(0) < N_PROMPT_TILES
    x = x_ref[...]
    h = (_rms(x, g_ref[...]) * (1.0 + sc_ref[...]) + sh_ref[...]).astype(BF16)
    branches = ((yap_ref, yas_ref, pa_ref), (ybp_ref, ybs_ref, pb_ref), (ycp_ref, ycs_ref, pc_ref))
    m = jnp.zeros((TM, D_MODEL), F32)
    for k, (p_ref, s_ref, w_ref) in enumerate(branches):
        y = jnp.where(is_prompt, p_ref[...], s_ref[...]).astype(BF16)
        gate = jax.nn.sigmoid(jnp.dot(h, wg_ref[:, k * D_MODEL:(k + 1) * D_MODEL], preferred_element_type=F32))
        m = m + gate * jnp.dot(y, w_ref[...], preferred_element_type=F32)
    o_ref[...] = x + g1_ref[...] * jnp.dot(m.astype(BF16), wo_ref[...], preferred_element_type=F32)


def _merge(ya, yb, yc, x, mod, layer, g, w_gate, pa, pb, pc, wo):
    return pl.pallas_call(
        _merge_kernel,
        grid=(N_TOK // TM,),
        in_specs=_path_specs(HY_WIDTH) + _path_specs(S5_WIDTH) + _path_specs(ATTN_WIDTH) + [
            pl.BlockSpec((TM, D_MODEL), lambda i: (i, 0)),
            _mod_spec(layer, 0),
            _mod_spec(layer, 1),
            _mod_spec(layer, 2),
            _resident((1, D_MODEL), layer),
            _resident((D_MODEL, GATE_IN), layer),
            _resident((HY_WIDTH, D_MODEL), layer),
            _resident((S5_WIDTH, D_MODEL), layer),
            _resident((ATTN_WIDTH, D_MODEL), layer),
            _resident((D_MODEL, D_MODEL), layer),
        ],
        out_specs=pl.BlockSpec((TM, D_MODEL), lambda i: (i, 0)),
        out_shape=jax.ShapeDtypeStruct((N_TOK, D_MODEL), F32),
        compiler_params=_cparams("arbitrary"),
        name="merge_out",
    )(*ya, *yb, *yc, x, mod, mod, mod, g, w_gate, pa, pb, pc, wo)


def _ffn_kernel(x_ref, sh_ref, sc_ref, gt_ref, g_ref, wg_ref, wu_ref, wd_ref, o_ref):
    x = x_ref[...]
    h = (_rms(x, g_ref[...]) * (1.0 + sc_ref[...]) + sh_ref[...]).astype(BF16)
    a = jnp.dot(h, wg_ref[...], preferred_element_type=F32)
    b = jnp.dot(h, wu_ref[...], preferred_element_type=F32)
    f = (a * jax.nn.sigmoid(a) * b).astype(BF16)
    o_ref[...] = x + gt_ref[...] * jnp.dot(f, wd_ref[...], preferred_element_type=F32)


def _ffn(x, mod, layer, g, wg, wu, wd):
    return pl.pallas_call(
        _ffn_kernel,
        grid=(N_TOK // TM,),
        in_specs=[
            pl.BlockSpec((TM, D_MODEL), lambda i: (i, 0)),
            _mod_spec(layer, 3),
            _mod_spec(layer, 4),
            _mod_spec(layer, 5),
            _resident((1, D_MODEL), layer),
            _resident((D_MODEL, FFN_HIDDEN), layer),
            _resident((D_MODEL, FFN_HIDDEN), layer),
            _resident((FFN_HIDDEN, D_MODEL), layer),
        ],
        out_specs=pl.BlockSpec((TM, D_MODEL), lambda i: (i, 0)),
        out_shape=jax.ShapeDtypeStruct((N_TOK, D_MODEL), F32),
        compiler_params=_cparams("arbitrary"),
        name="ffn",
    )(x, mod, mod, mod, g, wg, wu, wd)


def _final_norm_kernel(x_ref, g_ref, op_ref, os_ref):
    y = _rms(x_ref[...], g_ref[...])
    is_prompt = pl.program_id(0) < N_PROMPT_TILES

    @pl.when(is_prompt)
    def _():
        op_ref[...] = y

    @pl.when(jnp.logical_not(is_prompt))
    def _():
        os_ref[...] = y


def _final_norm(x, g):
    return pl.pallas_call(
        _final_norm_kernel,
        grid=(N_TOK // TM,),
        in_specs=[pl.BlockSpec((TM, D_MODEL), lambda i: (i, 0)), _resident((1, D_MODEL))],
        out_specs=_path_specs(D_MODEL),
        out_shape=[jax.ShapeDtypeStruct((N_PROMPT_TOK, D_MODEL), F32),
                   jax.ShapeDtypeStruct((N_SAMPLE_TOK, D_MODEL), F32)],
        compiler_params=_cparams("arbitrary"),
        name="final_norm",
    )(x, g)


SUBLANE = 8
HY_TILE = 1024
assert N_PROMPT_TOK % HY_TILE == 0 and N_SAMPLE_TOK % HY_TILE == 0
assert SEQ & (SEQ - 1) == 0 and DEC_SEQ & (DEC_SEQ - 1) == 0
HY_BLOCK = 512
HY_ROWS = 64
HY_TAPS_ROWS = 1024


def _dft_mats(blk):
    n = 2 * blk
    idx = jnp.arange(blk, dtype=jnp.int32)
    ang = (2.0 * math.pi / n) * ((idx[:, None] * idx[None, :]) % n).astype(F32)
    alt = jnp.where(idx % 2 == 0, 1.0, -1.0).astype(F32)
    return jnp.stack([jnp.cos(ang), jnp.where((idx == 0)[:, None], alt[None, :], -jnp.sin(ang))]).astype(BF16)


def _short_conv_kernel(x_ref, xp_ref, xn_ref, w_ref, b_ref, u_ref, vb_ref):
    i = pl.program_id(0)
    seq = jnp.where(i < N_PROMPT_TOK // HY_TILE, SEQ, DEC_SEQ)
    x = x_ref[...]
    row = lax.broadcasted_iota(jnp.int32, (HY_TILE, 1), 0)
    pos = (i * HY_TILE + row) & (seq - 1)
    x_prev = jnp.where(row == 0, xp_ref[SUBLANE - 1:SUBLANE, :], pltpu.roll(x, 1, axis=0))
    x_prev = jnp.where(pos == 0, 0.0, x_prev)
    x_next = jnp.where(row == HY_TILE - 1, xn_ref[0:1, :], pltpu.roll(x, HY_TILE - 1, axis=0))
    x_next = jnp.where(pos == seq - 1, 0.0, x_next)
    u = b_ref[...] + x_prev * w_ref[0:1, :] + x * w_ref[1:2, :] + x_next * w_ref[2:3, :]
    u_ref[...] = u
    vb_ref[...] = u[:, :HY_WIDTH].astype(BF16)


def _short_conv(zm, w, b):
    per8 = HY_TILE // SUBLANE
    last8 = N_TOK // SUBLANE - 1
    return pl.pallas_call(
        _short_conv_kernel,
        grid=(N_TOK // HY_TILE,),
        in_specs=[
            pl.BlockSpec((HY_TILE, HY_IN), lambda i: (i, 0)),
            pl.BlockSpec((SUBLANE, HY_IN), lambda i: (jnp.maximum(i * per8 - 1, 0), 0)),
            pl.BlockSpec((SUBLANE, HY_IN), lambda i: (jnp.minimum((i + 1) * per8, last8), 0)),
            pl.BlockSpec((HY_SHORT_K, HY_IN), lambda i: (0, 0)),
            pl.BlockSpec((1, HY_IN), lambda i: (0, 0)),
        ],
        out_specs=[
            pl.BlockSpec((HY_TILE, HY_IN), lambda i: (i, 0)),
            pl.BlockSpec((HY_TILE, HY_WIDTH), lambda i: (i, 0)),
        ],
        out_shape=[
            jax.ShapeDtypeStruct((N_TOK, HY_IN), F32),
            jax.ShapeDtypeStruct((N_TOK, HY_WIDTH), BF16),
        ],
        compiler_params=_cparams("arbitrary"),
        name="hyena_short_conv",
    )(zm, zm, zm, w, b)


def _block_conv_kernel(n_blk, last, zin_ref, k_ref, f_ref, g_ref, zp_ref, skip_ref, o_ref, *rest):
    z_ref, s_ref = rest[-2:]
    n = pl.program_id(1)
    blk = f_ref.shape[-1]

    @pl.when(n == 0)
    def _spectra():
        group = min(n_blk, 8)
        for g0 in range(0, n_blk, group):
            z = jnp.concatenate([zin_ref[(g0 + g) * blk:(g0 + g + 1) * blk, :] for g in range(group)], axis=1)
            for c in range(2):
                x = jnp.dot(f_ref[c], z, preferred_element_type=F32)
                for g in range(group):
                    z_ref[g0 + g, c] = x[:, g * HY_WIDTH:(g + 1) * HY_WIDTH]

    def product(r, first_row_packed):
        acc_r = acc_i = jnp.zeros((r.size, HY_WIDTH), F32)
        fix_r = fix_i = acc_r
        for i in range(n_blk):
            d = n - i + n_blk
            zr, zi = z_ref[i, 0, r, :], z_ref[i, 1, r, :]
            kr, ki = k_ref[d, 0, r, :], k_ref[d, 1, r, :]
            acc_r = acc_r + (zr * kr - zi * ki)
            acc_i = acc_i + (zr * ki + zi * kr)
            if first_row_packed:
                fix_r = fix_r + zr * kr
                fix_i = fix_i + zi * ki
        if first_row_packed:
            row0 = lax.broadcasted_iota(jnp.int32, (r.size, 1), 0) == 0
            acc_r = jnp.where(row0, fix_r, acc_r)
            acc_i = jnp.where(row0, fix_i, acc_i)
        s_ref[0, r, :] = acc_r.astype(BF16)
        s_ref[1, r, :] = acc_i.astype(BF16)

    def chunk(c, carry):
        product(pl.ds(pl.multiple_of(c * HY_ROWS, HY_ROWS), HY_ROWS), False)
        return carry

    lax.fori_loop(0, blk // HY_ROWS, chunk, 0)
    product(pl.ds(0, 2 * SUBLANE), True)

    conv_c = jnp.dot(f_ref[0], s_ref[0], preferred_element_type=F32)
    conv_s = jnp.dot(f_ref[1], s_ref[1], preferred_element_type=F32)
    t = lax.broadcasted_iota(jnp.int32, (blk, 1), 0)
    alt = jnp.where(t % 2 == 0, 1.0, -1.0).astype(F32)
    conv = conv_c + jnp.where(t == 0, 0.0, conv_s) + alt * s_ref[1, 0:1, :].astype(F32)
    z = g_ref[...] * (conv + skip_ref[...] * zp_ref[...])
    o_ref[...] = z
    if not last:
        rest[0][...] = z.astype(BF16)


def _block_conv(zb, zb_row0, tables, order, mats, u, gate_col, z_prev, z_prev_row0, skip, row0, n_seq, seq_len, last):
    blk = mats.shape[-1]
    n_blk = seq_len // blk

    def rows(base):
        return lambda b, n: (base // blk + b * n_blk + n, 0)

    out_spec = pl.BlockSpec((blk, HY_WIDTH), rows(0))
    out_shape = [jax.ShapeDtypeStruct((n_seq * n_blk * blk, HY_WIDTH), F32)]
    if not last:
        out_shape.append(jax.ShapeDtypeStruct((n_seq * n_blk * blk, HY_WIDTH), BF16))
    return pl.pallas_call(
        functools.partial(_block_conv_kernel, n_blk, last),
        grid=(n_seq, n_blk),
        in_specs=[
            pl.BlockSpec((seq_len, HY_WIDTH), lambda b, n: (zb_row0 // seq_len + b, 0)),
            pl.BlockSpec((None,) + tables.shape[1:], lambda b, n: (order, 0, 0, 0, 0), pipeline_mode=pl.Buffered(1)),
            pl.BlockSpec((2, blk, blk), lambda b, n: (0, 0, 0)),
            pl.BlockSpec((blk, HY_WIDTH), lambda b, n: (row0 // blk + b * n_blk + n, gate_col)),
            pl.BlockSpec((blk, HY_WIDTH), rows(z_prev_row0)),
            pl.BlockSpec((1, HY_WIDTH), lambda b, n: (0, 0)),
        ],
        out_specs=[out_spec] * len(out_shape),
        out_shape=out_shape,
        scratch_shapes=[pltpu.VMEM((n_blk, 2, blk, HY_WIDTH), F32), pltpu.VMEM((2, blk, HY_WIDTH), BF16)],
        compiler_params=_cparams("arbitrary", "arbitrary"),
        name="hyena_block_conv",
    )(zb, tables, mats, u, z_prev, skip)


def _taps_kernel(seq_len, rows, w1_ref, b1_ref, w2_ref, b2_ref, w3_ref, freq_ref, decay_ref, k_ref, norm_ref):
    i = pl.program_id(0)
    lag = i * rows + lax.broadcasted_iota(jnp.int32, (rows, 1), 0) - seq_len
    t = jnp.abs(lag).astype(F32) * (1.0 / (seq_len - 1))
    pos = jnp.abs(i * rows + lax.broadcasted_iota(jnp.int32, (1, rows), 1) - seq_len).astype(F32)
    band = lax.broadcasted_iota(jnp.int32, (HY_POS_BANDS, 1), 0).astype(F32)
    band = 1e-4 + band * ((HY_POS_BANDS - 1 - 1e-4) / (HY_POS_BANDS - 1))
    ang = band * (pos * (2.0 * math.pi / seq_len))
    dot = functools.partial(jnp.dot, precision=lax.Precision.HIGHEST, preferred_element_type=F32)
    pre = w1_ref[:, 0:1] * (pos * (1.0 / (seq_len - 1)))
    pre = pre + dot(w1_ref[:, 1:1 + HY_POS_BANDS], jnp.cos(ang))
    pre = pre - dot(w1_ref[:, 1 + HY_POS_BANDS:], jnp.sin(ang))
    freq = freq_ref[...]
    h = jnp.sin(freq * (pre + b1_ref[...]))
    h = jnp.sin(freq * (dot(w2_ref[...], h) + b2_ref[...]))
    h = lax.dot_general(h.astype(BF16), w3_ref[...].astype(BF16), (((0,), (0,)), ((), ())),
                        preferred_element_type=F32)
    h = h * jnp.exp(-t * jnp.abs(decay_ref[...]))

    @pl.when(i == 0)
    def _():
        norm_ref[...] = jnp.zeros_like(norm_ref)

    for o in range(HY_ORDER):
        c0 = o * 2 * HY_WIDTH
        k = jnp.where(lag >= 0, h[:, c0:c0 + HY_WIDTH], h[:, c0 + HY_WIDTH:c0 + 2 * HY_WIDTH])
        k = jnp.where(lag == -seq_len, 0.0, k)
        k_ref[o] = k.astype(BF16)
        norm_ref[:, o * HY_WIDTH:(o + 1) * HY_WIDTH] += jnp.sum(jnp.abs(k), axis=0, keepdims=True)


def _hyena_taps(seq_len, w1, b1, w2, b2, w3, freq, decay):
    rows = min(2 * seq_len, HY_TAPS_ROWS)
    full = lambda a: pl.BlockSpec(a.shape, lambda i: (0,) * a.ndim)
    args = (w1.T, b1.reshape(-1, 1), w2.T, b2.reshape(-1, 1), w3, freq.reshape(-1, 1), decay.reshape(1, -1))
    return pl.pallas_call(
        functools.partial(_taps_kernel, seq_len, rows),
        grid=(2 * seq_len // rows,),
        in_specs=[full(a) for a in args],
        out_specs=[pl.BlockSpec((HY_ORDER, rows, HY_WIDTH), lambda i: (0, i, 0)),
                   pl.BlockSpec((1, HY_ORDER * HY_WIDTH), lambda i: (0, 0))],
        out_shape=[jax.ShapeDtypeStruct((HY_ORDER, 2 * seq_len, HY_WIDTH), BF16),
                   jax.ShapeDtypeStruct((1, HY_ORDER * HY_WIDTH), F32)],
        compiler_params=_cparams("arbitrary"),
        name="hyena_taps",
    )(*args)


def _tables_kernel(group, x_ref, norm_ref, f_ref, o_ref, prev_ref):
    blk = f_ref.shape[-1]

    @pl.when(pl.program_id(1) == 0)
    def _():
        prev_ref[...] = jnp.zeros_like(prev_ref)

    f = lax.broadcasted_iota(jnp.int32, (blk, 1), 0)
    alt = jnp.where(f % 2 == 0, 1.0, -1.0).astype(F32)
    scale = jnp.where(f == 0, 1.0, 2.0).astype(F32) * (1.0 / (2 * blk)) / norm_ref[...]
    x = jnp.concatenate([x_ref[g * blk:(g + 1) * blk, :] for g in range(group)], axis=1)
    for c in range(2):
        xc = jnp.dot(f_ref[c], x, preferred_element_type=F32)
        prev = prev_ref[c]
        for g in range(group):
            cur = xc[:, g * HY_WIDTH:(g + 1) * HY_WIDTH]
            o_ref[g, c] = (cur + alt * prev) * scale
            prev = cur
        prev_ref[c] = prev


def _hyena_tables(seq_len, mats, filt):
    blk = mats.shape[-1]
    n_lag = 2 * seq_len // blk
    group = min(n_lag, 8)
    taps, norm = _hyena_taps(seq_len, *filt)
    return pl.pallas_call(
        functools.partial(_tables_kernel, group),
        grid=(HY_ORDER, n_lag // group),
        in_specs=[
            pl.BlockSpec((None, group * blk, HY_WIDTH), lambda o, q: (o, q, 0)),
            pl.BlockSpec((1, HY_WIDTH), lambda o, q: (0, o)),
            pl.BlockSpec((2, blk, blk), lambda o, q: (0, 0, 0)),
        ],
        out_specs=pl.BlockSpec((None, group, 2, blk, HY_WIDTH), lambda o, q: (o, q, 0, 0, 0)),
        out_shape=jax.ShapeDtypeStruct((HY_ORDER, n_lag, 2, blk, HY_WIDTH), F32),
        scratch_shapes=[pltpu.VMEM((2, blk, HY_WIDTH), F32)],
        compiler_params=_cparams("arbitrary", "arbitrary"),
        name="hyena_tables",
    )(taps, norm, mats)


def _hyena_path(u, vb, row0, n_seq, seq_len, mats, tables, skip):
    z1, z1b = _block_conv(vb, row0, tables, 0, mats, u, 1, u, row0, skip[0:1], row0, n_seq, seq_len, False)
    (ya,) = _block_conv(z1b, 0, tables, 1, mats, u, 2, z1, 0, skip[1:2], row0, n_seq, seq_len, True)
    return ya


S5_LANES = S5_GROUPS * S5_STATE
S5_SEGS = 8
S5_CHUNKS = 4
S5_TILE = 512


def _s5_kernel(n_tiles, tile, u0_ref, u1_ref, h0_ref, lam_ref, bd_ref, cd_ref, skip_ref, gw_ref, gb_ref,
               y_ref, fin_ref, bu_ref, ybwd_ref, p_ref, p8_ref, cm_ref, fl_ref, carry_ref, up_ref, st0_ref, st1_ref,
               ub_ref, yacc_ref):
    seg = tile // S5_SEGS
    n = S5_LANES
    b = pl.program_id(0)
    j = pl.program_id(1)

    @pl.when((b == 0) & (j == 0))
    def _powers():
        for d in range(2):
            lr = lam_ref[2 * d:2 * d + 1, :]
            li = lam_ref[2 * d + 1:2 * d + 2, :]

            def put(k, pr, pi, d=d):
                p_ref[d, pl.ds(k, 1), 0:n] = pr
                p_ref[d, pl.ds(k, 1), n:2 * n] = pi
                r = pl.ds(pl.multiple_of(k * S5_SEGS, S5_SEGS), S5_SEGS)
                p8_ref[d, r, 0:n] = jnp.broadcast_to(pr, (S5_SEGS, n))
                p8_ref[d, r, n:2 * n] = jnp.broadcast_to(pi, (S5_SEGS, n))

            put(0, lr, li)

            def body(k, c, lr=lr, li=li, put=put):
                pr, pi = c
                nr = pr * lr - pi * li
                ni = pr * li + pi * lr
                put(k, nr, ni)
                return nr, ni

            lax.fori_loop(1, seg, body, (lr, li))

    def sweep(d, t):
        first, last = (0, n_tiles - 1) if d == 0 else (n_tiles - 1, 0)

        @pl.when(t == first)
        def _init():
            carry_ref[0:1, 0:n] = h0_ref[2 * d:2 * d + 1, :]
            carry_ref[0:1, n:2 * n] = h0_ref[2 * d + 1:2 * d + 2, :]

        def gather(i, c):
            r = pl.ds(pl.multiple_of(i * S5_SEGS, S5_SEGS), S5_SEGS)
            up_ref[r, 0:LANE] = u0_ref[pl.ds(i, S5_SEGS, stride=seg), :]
            up_ref[r, LANE:2 * LANE] = u1_ref[pl.ds(i, S5_SEGS, stride=seg), :]
            return c

        lax.fori_loop(0, seg, gather, 0, unroll=2)
        ub_ref[...] = up_ref[...].astype(BF16)
        nq = n // S5_CHUNKS

        def lanes(q):
            return slice(q * nq, (q + 1) * nq), slice(n + q * nq, n + (q + 1) * nq)

        def rows(i):
            k = i if d == 0 else seg - 1 - i
            return slice(k * S5_SEGS, (k + 1) * S5_SEGS)

        def project_in(q):
            re, im = lanes(q)
            bu_ref[:, re] = jnp.dot(ub_ref[...], bd_ref[d, :, re], preferred_element_type=F32)
            bu_ref[:, im] = jnp.dot(ub_ref[...], bd_ref[d, :, im], preferred_element_type=F32)

        def scan(q):
            re, im = lanes(q)
            lr8 = jnp.broadcast_to(lam_ref[2 * d:2 * d + 1, re], (S5_SEGS, nq))
            li8 = jnp.broadcast_to(lam_ref[2 * d + 1:2 * d + 2, re], (S5_SEGS, nq))
            hr = hi = jnp.zeros((S5_SEGS, nq), F32)
            for i in range(seg):
                r = rows(i)
                hr, hi = lr8 * hr - li8 * hi + bu_ref[r, re], lr8 * hi + li8 * hr + bu_ref[r, im]
                bu_ref[r, re] = hr
                bu_ref[r, im] = hi
            fl_ref[:, re] = hr
            fl_ref[:, im] = hi

            plr = p_ref[d, seg - 1:seg, re]
            pli = p_ref[d, seg - 1:seg, im]
            cr = carry_ref[0:1, re]
            ci = carry_ref[0:1, im]
            for s in (range(S5_SEGS) if d == 0 else reversed(range(S5_SEGS))):
                cm_ref[s:s + 1, re] = cr
                cm_ref[s:s + 1, im] = ci
                cr, ci = (plr * cr - pli * ci + fl_ref[s:s + 1, re], plr * ci + pli * cr + fl_ref[s:s + 1, im])
            carry_ref[0:1, re] = cr
            carry_ref[0:1, im] = ci

            cmr = cm_ref[:, re]
            cmi = cm_ref[:, im]
            for i in range(seg):
                r = rows(i)
                pw = slice(i * S5_SEGS, (i + 1) * S5_SEGS)
                pr = p8_ref[d, pw, re]
                pi = p8_ref[d, pw, im]
                bu_ref[r, re] = bu_ref[r, re] + (pr * cmr - pi * cmi)
                bu_ref[r, im] = bu_ref[r, im] + (pr * cmi + pi * cmr)

        def project_out(q):
            re, im = lanes(q)
            part = (jnp.dot(bu_ref[:, re].astype(BF16), cd_ref[d, re, :], preferred_element_type=F32)
                    + jnp.dot(bu_ref[:, im].astype(BF16), cd_ref[d, im, :], preferred_element_type=F32))
            yacc_ref[...] = part if q == 0 else yacc_ref[...] + part

        project_in(0)
        for q in range(S5_CHUNKS):
            if q + 1 < S5_CHUNKS:
                project_in(q + 1)
            if q > 0:
                project_out(q - 1)
            scan(q)
        project_out(S5_CHUNKS - 1)
        y = yacc_ref[...]

        @pl.when(t == last)
        def _final():
            fin_ref[2 * d:2 * d + 1, :] = carry_ref[0:1, 0:n]
            fin_ref[2 * d + 1:2 * d + 2, :] = carry_ref[0:1, n:2 * n]

        trow = pl.ds(pl.multiple_of(t * tile, tile), tile)
        if d == 1:
            ybwd_ref[trow, :] = y
        else:
            tot = up_ref[...] * skip_ref[...] + y + ybwd_ref[trow, :]
            g = jnp.dot(tot.astype(BF16), gw_ref[...], preferred_element_type=F32) + gb_ref[...]
            res = g[:, :S5_WIDTH] * jax.nn.sigmoid(g[:, S5_WIDTH:])
            st0_ref[...] = res[:, 0:LANE]
            st1_ref[...] = res[:, LANE:2 * LANE]
            per_seg = seg // S5_SEGS

            def unpermute(m, c):
                s = m // per_seg
                i0 = (m % per_seg) * S5_SEGS
                src = pl.ds(i0 * S5_SEGS + s, S5_SEGS, stride=S5_SEGS)
                dst = pl.ds(pl.multiple_of(m * S5_SEGS, S5_SEGS), S5_SEGS)
                y_ref[dst, 0:LANE] = st0_ref[src, :]
                y_ref[dst, LANE:2 * LANE] = st1_ref[src, :]
                return c

            lax.fori_loop(0, tile // S5_SEGS, unpermute, 0, unroll=2)

    @pl.when(j < n_tiles)
    def _bwd():
        sweep(1, n_tiles - 1 - j)

    @pl.when(j >= n_tiles)
    def _fwd():
        sweep(0, j - n_tiles)


def _s5_branch(zm, row0, n_seq, seq_len, tile, h0, lam, bd, cd, skip, gw, gb):
    n_tiles = seq_len // tile
    blk0 = row0 // tile
    col = SPLIT_S5 // LANE
    seg = tile // S5_SEGS

    def tile_of(j):
        return jnp.where(j < n_tiles, n_tiles - 1 - j, j - n_tiles)

    return pl.pallas_call(
        functools.partial(_s5_kernel, n_tiles, tile),
        grid=(n_seq, 2 * n_tiles),
        in_specs=[
            pl.BlockSpec((tile, LANE), lambda b, j: (blk0 + b * n_tiles + tile_of(j), col)),
            pl.BlockSpec((tile, LANE), lambda b, j: (blk0 + b * n_tiles + tile_of(j), col + 1)),
            pl.BlockSpec((None, 4, S5_LANES), lambda b, j: (b, 0, 0)),
            pl.BlockSpec((4, S5_LANES), lambda b, j: (0, 0)),
            pl.BlockSpec((2, S5_WIDTH, 2 * S5_LANES), lambda b, j: (0, 0, 0)),
            pl.BlockSpec((2, 2 * S5_LANES, S5_WIDTH), lambda b, j: (0, 0, 0)),
            pl.BlockSpec((1, S5_WIDTH), lambda b, j: (0, 0)),
            pl.BlockSpec((S5_WIDTH, 2 * S5_WIDTH), lambda b, j: (0, 0)),
            pl.BlockSpec((1, 2 * S5_WIDTH), lambda b, j: (0, 0)),
        ],
        out_specs=[
            pl.BlockSpec((tile, S5_WIDTH), lambda b, j: (b * n_tiles + jnp.maximum(j - n_tiles, 0), 0)),
            pl.BlockSpec((None, 4, S5_LANES), lambda b, j: (b, 0, 0)),
        ],
        out_shape=[
            jax.ShapeDtypeStruct((n_seq * seq_len, S5_WIDTH), F32),
            jax.ShapeDtypeStruct((n_seq, 4, S5_LANES), F32),
        ],
        scratch_shapes=[
            pltpu.VMEM((tile, 2 * S5_LANES), F32),
            pltpu.VMEM((seq_len, S5_WIDTH), F32),
            pltpu.VMEM((2, seg, 2 * S5_LANES), F32),
            pltpu.VMEM((2, tile, 2 * S5_LANES), F32),
            pltpu.VMEM((S5_SEGS, 2 * S5_LANES), F32),
            pltpu.VMEM((S5_SEGS, 2 * S5_LANES), F32),
            pltpu.VMEM((1, 2 * S5_LANES), F32),
            pltpu.VMEM((tile, S5_WIDTH), F32),
            pltpu.VMEM((tile, LANE), F32),
            pltpu.VMEM((tile, LANE), F32),
            pltpu.VMEM((tile, S5_WIDTH), BF16),
            pltpu.VMEM((tile, S5_WIDTH), F32),
        ],
        compiler_params=_cparams("arbitrary", "arbitrary"),
        name="s5_scan",
    )(zm, zm, h0, lam, bd, cd, skip, gw, gb)


def _s5_params(lam_re, lam_im, log_step, b_re, b_im, c_re, c_im):
    step = jnp.exp(log_step)[..., None]
    mag = jnp.exp(lam_re * step)
    bar_re, bar_im = mag * jnp.cos(lam_im * step), mag * jnp.sin(lam_im * step)
    den = lam_re * lam_re + lam_im * lam_im
    q_re = ((bar_re - 1.0) * lam_re + bar_im * lam_im) / den
    q_im = (bar_im * lam_re - (bar_re - 1.0) * lam_im) / den
    bb_re = q_re[..., None] * b_re - q_im[..., None] * b_im
    bb_im = q_re[..., None] * b_im + q_im[..., None] * b_re
    eye = jnp.eye(S5_GROUPS, dtype=F32)

    def in_mat(x):
        xt = jnp.swapaxes(x, 2, 3)
        return (xt[:, :, :, None, :] * eye[None, :, None, :, None]).reshape(2, S5_WIDTH, S5_LANES)

    def out_mat(x):
        xt = jnp.swapaxes(x, 2, 3)
        return (xt[:, :, :, None, :] * eye[None, :, None, :, None]).reshape(2, S5_LANES, S5_WIDTH)

    bd = jnp.concatenate([in_mat(bb_re), in_mat(bb_im)], axis=2).astype(BF16)
    cd = jnp.concatenate([out_mat(c_re), out_mat(-c_im)], axis=1).astype(BF16)
    lam_rows = jnp.stack([bar_re[0], bar_im[0], bar_re[1], bar_im[1]])
    return lam_rows.reshape(4, S5_LANES), bd, cd


ROPE_HALF = HEAD_DIM // 4
CTX_SEQS = 4


def _rope_tables(seq_len):
    t = jnp.arange(seq_len)
    row = (t // GRID_W).astype(F32)
    col = (t % GRID_W).astype(F32)
    inv = ROPE_BASE ** (-jnp.arange(ROPE_HALF, dtype=F32) / ROPE_HALF)
    ang_r = row[:, None] * inv
    ang_c = col[:, None] * inv
    cos = jnp.concatenate([jnp.cos(ang_r), jnp.cos(ang_r), jnp.cos(ang_c), jnp.cos(ang_c)], axis=-1)
    sin = jnp.concatenate([-jnp.sin(ang_r), jnp.sin(ang_r), -jnp.sin(ang_c), jnp.sin(ang_c)], axis=-1)
    return jnp.tile(cos, (1, LANE // HEAD_DIM)), jnp.tile(sin, (1, LANE // HEAD_DIM))


def _rope(x, cos, sin):
    lane = lax.broadcasted_iota(jnp.int32, x.shape, 1)
    first = (lane % (2 * ROPE_HALF)) < ROPE_HALF
    partner = jnp.where(first, pltpu.roll(x, LANE - ROPE_HALF, axis=1), pltpu.roll(x, ROPE_HALF, axis=1))
    return x * cos + partner * sin


def _gqa(q, ks, vs, biases, sink_ref):
    t = q.shape[0]
    low = lax.broadcasted_iota(jnp.int32, (1, LANE), 1) < HEAD_DIM
    assert N_KV_HEADS * HEAD_DIM == LANE
    k_nat = [k.astype(BF16) for k in ks]
    k_swp = [pltpu.roll(k, HEAD_DIM, axis=1).astype(BF16) for k in ks]
    v_aug = [jnp.concatenate([v, jnp.ones_like(v)], axis=1).astype(BF16) for v in vs]
    outs = {}
    for keys, heads in ((k_nat, [h for h in range(N_HEADS) if h % 2 == h // Q_PER_KV]),
                        (k_swp, [h for h in range(N_HEADS) if h % 2 != h // Q_PER_KV])):
        scale = HEAD_DIM ** -0.5
        assert math.frexp(scale)[0] == 0.5
        rows = [jnp.where(low if h % 2 == 0 else jnp.logical_not(low),
                          q[:, (h // 2) * LANE:(h // 2 + 1) * LANE] * scale, 0.0) for h in heads]
        qs = jnp.concatenate(rows, axis=0).astype(BF16)
        sink_col = jnp.concatenate([jnp.full((t, 1), sink_ref[h], F32) for h in heads], axis=0)
        ss = []
        m = sink_col
        for k, bias in zip(keys, biases):
            s = lax.dot_general(qs, k, (((1,), (1,)), ((), ())), preferred_element_type=F32)
            if bias is not None:
                s = s + bias
            ss.append(s)
            m = jnp.maximum(m, jnp.max(s, axis=-1, keepdims=True))
        acc = jnp.zeros((len(heads) * t, 2 * LANE), F32)
        for s, v in zip(ss, v_aug):
            acc = acc + jnp.dot(jnp.exp(s - m).astype(BF16), v, preferred_element_type=F32)
        o = acc[:, :LANE] / (acc[:, LANE:LANE + 1] + jnp.exp(sink_col - m))
        for j, h in enumerate(heads):
            outs[h] = o[j * t:(j + 1) * t]
    chunks = []
    for c in range(N_HEADS // 2):
        halves = []
        for h in (2 * c, 2 * c + 1):
            halves.append(outs[h] if h // Q_PER_KV == h % 2 else pltpu.roll(outs[h], HEAD_DIM, axis=1))
        chunks.append(jnp.where(low, halves[0], halves[1]))
    return jnp.concatenate(chunks, axis=1)


def _ctx_attn_kernel(q_ref, k_ref, v_ref, sink_ref, kin_ref, vin_ref, o_ref, kout_ref, vout_ref):
    del kin_ref, vin_ref
    for g in range(CTX_SEQS):
        rows = slice(g * SEQ, (g + 1) * SEQ)
        k, v = k_ref[rows, :], v_ref[rows, :]
        kout_ref[g] = k
        vout_ref[g] = v
        o_ref[rows, :] = _gqa(q_ref[rows, :], [k], [v], [None], sink_ref)


def _context_attention(zm, sink, layer, new_k, new_v):
    slab = pl.BlockSpec((CTX_SEQS, None, SEQ, KV_WIDTH), lambda b: (b, layer, 0, 0))
    rows = CTX_SEQS * SEQ
    return pl.pallas_call(
        _ctx_attn_kernel,
        grid=(BATCH // CTX_SEQS,),
        in_specs=[
            pl.BlockSpec((rows, ATTN_WIDTH), lambda b: (b, SPLIT_Q // ATTN_WIDTH)),
            pl.BlockSpec((rows, KV_WIDTH), lambda b: (b, SPLIT_K // KV_WIDTH)),
            pl.BlockSpec((rows, KV_WIDTH), lambda b: (b, SPLIT_V // KV_WIDTH)),
            pl.BlockSpec(memory_space=pltpu.SMEM),
            pl.BlockSpec(memory_space=pl.ANY),
            pl.BlockSpec(memory_space=pl.ANY),
        ],
        out_specs=[pl.BlockSpec((rows, ATTN_WIDTH), lambda b: (b, 0)), slab, slab],
        out_shape=[jax.ShapeDtypeStruct((N_PROMPT_TOK, ATTN_WIDTH), F32),
                   jax.ShapeDtypeStruct(new_k.shape, F32), jax.ShapeDtypeStruct(new_v.shape, F32)],
        input_output_aliases={4: 1, 5: 2},
        compiler_params=_cparams("arbitrary"),
        name="ctx_attention",
    )(zm, zm, zm, sink, new_k, new_v)


def _latent_attn_kernel(*refs):
    per_seq = 9
    cos_ref, sin_ref, band_ref, sink_ref, o_ref = refs[DEC_BATCH * per_seq:]
    i = pl.program_id(0)
    nb = DEC_SEQ // BLOCK

    def table(ref, blk):
        return ref[pl.ds(pl.multiple_of(blk * BLOCK, BLOCK), BLOCK), :]

    left, right = jnp.maximum(i - 1, 0), jnp.minimum(i + 1, nb - 1)
    cos_q, sin_q = table(cos_ref, i), table(sin_ref, i)
    cos_l, sin_l = table(cos_ref, left), table(sin_ref, left)
    cos_r, sin_r = table(cos_ref, right), table(sin_ref, right)

    c = lax.broadcasted_iota(jnp.int32, (1, 3 * BLOCK), 1)
    outside = ((c < BLOCK) & (i == 0)) | ((c >= 2 * BLOCK) & (i == nb - 1))
    bias = band_ref[...] + jnp.where(outside, NEG_INF, 0.0)

    for b in range(DEC_BATCH):
        q_ref, kl_ref, kc_ref, kr_ref, vl_ref, vc_ref, vr_ref, ck_ref, cv_ref = refs[b * per_seq:(b + 1) * per_seq]
        q = jnp.concatenate([_rope(q_ref[:, c0:c0 + LANE], cos_q, sin_q) for c0 in range(0, ATTN_WIDTH, LANE)], axis=1)
        kw = jnp.concatenate([_rope(kl_ref[...], cos_l, sin_l), _rope(kc_ref[...], cos_q, sin_q),
                              _rope(kr_ref[...], cos_r, sin_r)], axis=0)
        vw = jnp.concatenate([vl_ref[...], vc_ref[...], vr_ref[...]], axis=0)
        o_ref[b] = _gqa(q, [ck_ref[...], kw], [cv_ref[...], vw], [None, bias], sink_ref)


def _latent_attention(zm, cache_k, cache_v, layer, cos, sin, sink):
    nb = DEC_SEQ // BLOCK
    blk0 = N_PROMPT_TOK // BLOCK
    kcol, vcol = SPLIT_K // KV_WIDTH, SPLIT_V // KV_WIDTH

    def seq_specs(b):
        def row(i):
            return blk0 + b * nb + i

        def win_specs(col):
            return [pl.BlockSpec((BLOCK, KV_WIDTH), lambda i: (row(jnp.maximum(i - 1, 0)), col)),
                    pl.BlockSpec((BLOCK, KV_WIDTH), lambda i: (row(i), col)),
                    pl.BlockSpec((BLOCK, KV_WIDTH), lambda i: (row(jnp.minimum(i + 1, nb - 1)), col))]

        ctx_spec = pl.BlockSpec((None, None, PAST_LEN, KV_WIDTH), lambda i: (b, layer, 0, 0))
        return ([pl.BlockSpec((BLOCK, ATTN_WIDTH), lambda i: (row(i), SPLIT_Q // ATTN_WIDTH))]
                + win_specs(kcol) + win_specs(vcol) + [ctx_spec, ctx_spec])

    tab_spec = pl.BlockSpec((DEC_SEQ, LANE), lambda i: (0, 0))
    r = jnp.arange(Q_PER_KV * BLOCK)[:, None] % BLOCK
    c = jnp.arange(3 * BLOCK)[None, :]
    band = jnp.where(jnp.abs(r + BLOCK - c) <= WINDOW, 0.0, NEG_INF).astype(F32)
    seq_args = (zm,) * 7 + (cache_k, cache_v)
    out = pl.pallas_call(
        _latent_attn_kernel,
        grid=(nb,),
        in_specs=[spec for b in range(DEC_BATCH) for spec in seq_specs(b)]
        + [tab_spec, tab_spec, pl.BlockSpec(band.shape, lambda i: (0, 0)), pl.BlockSpec(memory_space=pltpu.SMEM)],
        out_specs=pl.BlockSpec((DEC_BATCH, BLOCK, ATTN_WIDTH), lambda i: (0, i, 0)),
        out_shape=jax.ShapeDtypeStruct((DEC_BATCH, DEC_SEQ, ATTN_WIDTH), F32),
        compiler_params=_cparams("arbitrary"),
        name="latent_attention",
    )(*(seq_args * DEC_BATCH), cos, sin, band, sink)
    return out.reshape(N_SAMPLE_TOK, ATTN_WIDTH)


def kernel(x_prompt, x_sample, cache_k, cache_v, state_s5_re, state_s5_im, c, c_ctx, ada_w, ada_b, norm1_g, w_in, hy_conv_w, hy_conv_b, hy_pos_w1, hy_pos_b1, hy_pos_w2, hy_pos_b2, hy_pos_w3, hy_sin_freq, hy_decay, hy_skip, s5_lam_re, s5_lam_im, s5_log_step, s5_b_re, s5_b_im, s5_c_re, s5_c_im, s5_skip, s5_glu_w, s5_glu_b, attn_sink, proj_a, proj_b, proj_c, w_out, norm2_g, ffn_w_gate, ffn_w_up, ffn_w_down, final_norm_g):
    cvec = jnp.concatenate([c_ctx[None], c, jnp.zeros((MOD_ROWS - 1 - DEC_BATCH, D_MODEL), F32)], axis=0)
    mod = _modulation(cvec, ada_w, ada_b).reshape(DEPTH * MOD_ROWS * 6, 1, D_MODEL)

    x = jnp.concatenate([x_prompt.reshape(N_PROMPT_TOK, D_MODEL), x_sample.reshape(N_SAMPLE_TOK, D_MODEL)], axis=0)
    w_main = w_in[:, :, :MAIN_IN].astype(BF16)
    w_gate = w_in[:, :, MAIN_IN:].astype(BF16)
    pa, pb, pc, wo = proj_a.astype(BF16), proj_b.astype(BF16), proj_c.astype(BF16), w_out.astype(BF16)
    wg, wu, wd = ffn_w_gate.astype(BF16), ffn_w_up.astype(BF16), ffn_w_down.astype(BF16)
    g1, g2 = norm1_g.reshape(DEPTH, 1, D_MODEL), norm2_g.reshape(DEPTH, 1, D_MODEL)
    glu_w = s5_glu_w.astype(BF16)

    ctx_k = cache_k.reshape(DEC_BATCH, DEPTH, PAST_LEN, KV_WIDTH)
    ctx_v = cache_v.reshape(DEC_BATCH, DEPTH, PAST_LEN, KV_WIDTH)
    rope_cos, rope_sin = _rope_tables(DEC_SEQ)
    mats_p, mats_s = _dft_mats(min(SEQ, HY_BLOCK)), _dft_mats(min(DEC_SEQ, HY_BLOCK))

    new_k = jnp.zeros((BATCH, DEPTH, SEQ, KV_WIDTH), F32)
    new_v = jnp.zeros((BATCH, DEPTH, SEQ, KV_WIDTH), F32)
    sre_out, sim_out = [], []
    for l in range(DEPTH):
        filt = (hy_pos_w1[l], hy_pos_b1[l], hy_pos_w2[l], hy_pos_b2[l], hy_pos_w3[l], hy_sin_freq[l], hy_decay[l])
        zm = _in_proj(x, mod, l, g1, w_main)
        u, vb = _short_conv(zm, hy_conv_w[l], hy_conv_b[l].reshape(1, HY_IN))
        ya_p = _hyena_path(u, vb, 0, BATCH, SEQ, mats_p, _hyena_tables(SEQ, mats_p, filt), hy_skip[l])
        ya_s = _hyena_path(u, vb, N_PROMPT_TOK, DEC_BATCH, DEC_SEQ, mats_s, _hyena_tables(DEC_SEQ, mats_s, filt),
                           hy_skip[l])
        yc_p, new_k, new_v = _context_attention(zm, attn_sink[l], l, new_k, new_v)
        yc_s = _latent_attention(zm, ctx_k, ctx_v, l, rope_cos, rope_sin, attn_sink[l])

        lam, bd, cd = _s5_params(s5_lam_re[l], s5_lam_im[l], s5_log_step[l], s5_b_re[l], s5_b_im[l],
                                 s5_c_re[l], s5_c_im[l])
        s5_w = (lam, bd, cd, s5_skip[l].reshape(1, S5_WIDTH), glu_w[l], s5_glu_b[l].reshape(1, 2 * S5_WIDTH))
        h0_s = jnp.stack([state_s5_re[:, l, 0], state_s5_im[:, l, 0], state_s5_re[:, l, 1], state_s5_im[:, l, 1]],
                         axis=1).reshape(DEC_BATCH, 4, S5_LANES)
        yb_p, fin_p = _s5_branch(zm, 0, BATCH, SEQ, SEQ, jnp.zeros((BATCH, 4, S5_LANES), F32), *s5_w)
        yb_s, _ = _s5_branch(zm, N_PROMPT_TOK, DEC_BATCH, DEC_SEQ, S5_TILE, h0_s, *s5_w)
        fin_p = fin_p.reshape(BATCH, 2, 2, S5_GROUPS, S5_STATE)

        sre_out.append(fin_p[:, :, 0])
        sim_out.append(fin_p[:, :, 1])
        x = _merge((ya_p, ya_s), (yb_p, yb_s), (yc_p, yc_s), x, mod, l, g1, w_gate, pa, pb, pc, wo)
        x = _ffn(x, mod, l, g2, wg, wu, wd)

    y_prompt, y_sample = _final_norm(x, final_norm_g.reshape(1, D_MODEL))
    return (y_prompt.reshape(BATCH, SEQ, D_MODEL), y_sample.reshape(DEC_BATCH, DEC_SEQ, D_MODEL),
            new_k.reshape(BATCH, DEPTH, SEQ, N_KV_HEADS, HEAD_DIM), new_v.reshape(BATCH, DEPTH, SEQ, N_KV_HEADS, HEAD_DIM),
            jnp.stack(sre_out, axis=1), jnp.stack(sim_out, axis=1))
```
